```python
import math
import jax, jax.numpy as jnp
from jax import lax
import numpy as np

D_MODEL = 1024
BATCH = 4
SEQ = 8192
DEPTH = 2
DEC_BATCH = 8
DEC_SEQ = 32
PAST_LEN = 2048

CHUNK = 64
Q_BLOCK = 128
N_A_LAYERS = DEPTH // 2
N_B_LAYERS = DEPTH - N_A_LAYERS
ALPHA = (2.0 * DEPTH) ** 0.25
BETA = (8.0 * DEPTH) ** -0.25
LN_EPS = 1e-5
RMS_EPS = 1e-6
GDN_HEADS = 8
GDN_DK = 128
GDN_DV = 128
GDN_CONV = 4
GDN_QK = GDN_HEADS * GDN_DK
GDN_V = GDN_HEADS * GDN_DV
GDN_CONV_CH = 2 * GDN_QK + GDN_V
GDN_PROJ = 2 * GDN_QK + 2 * GDN_V + 2 * GDN_HEADS
MLA_HEADS = 8
MLA_NOPE = 128
MLA_ROPE = 64
MLA_V = 128
MLA_KV_LORA = 256
MLA_Q_LORA = 384
ROPE_THETA = 10000.0
MLA_SCALE = (MLA_NOPE + MLA_ROPE) ** -0.5
D_FF = 2816
FFN_CONV = 3

kernel_name = 'hybrid_gdn_mla_streaming_step'


def layer_norm(x, g, b):
    xf = x.astype(jnp.float32)
    mu = jnp.mean(xf, axis=-1, keepdims=True)
    var = jnp.mean(jnp.square(xf - mu), axis=-1, keepdims=True)
    return ((xf - mu) * lax.rsqrt(var + LN_EPS) * g + b).astype(x.dtype)


def rms_norm(x, g):
    xf = x.astype(jnp.float32)
    return (xf * lax.rsqrt(jnp.mean(xf * xf, axis=-1, keepdims=True) + RMS_EPS) * g).astype(x.dtype)


def l2_normalize(x):
    return x * lax.rsqrt(jnp.sum(x * x, axis=-1, keepdims=True) + RMS_EPS)


def causal_dwconv(u, hist, w):
    K = w.shape[0]
    L = u.shape[1]
    full = jnp.concatenate([hist.astype(u.dtype), u], axis=1)
    out = full[:, 0:L] * w[0]
    for j in range(1, K):
        out = out + full[:, j:j + L] * w[j]
    return out, full[:, full.shape[1] - (K - 1):]


def rope_tables(pos):
    inv = 1.0 / (ROPE_THETA ** (jnp.arange(0, MLA_ROPE, 2, dtype=jnp.float32) / MLA_ROPE))
    ang = pos.astype(jnp.float32)[:, None] * inv[None, :]
    ang = jnp.concatenate([ang, ang], axis=-1)
    return jnp.cos(ang), jnp.sin(ang)


def apply_rope(x, cos, sin):
    xf = x.astype(jnp.float32)
    x1, x2 = jnp.split(xf, 2, axis=-1)
    rot = jnp.concatenate([-x2, x1], axis=-1)
    return (xf * cos + rot * sin).astype(x.dtype)


def ada_terms(c, w, b):
    m = jax.nn.silu(c) @ w + b
    shift, scale, gate = jnp.split(m, 3, axis=-1)
    return shift[:, None], scale[:, None], gate[:, None]


def gdn_chunk_step(S, inp):
    q, k, v, g, beta = inp
    C = q.shape[2]
    G = jnp.cumsum(g, axis=-1)
    idx = jnp.arange(C)
    incl = idx[:, None] >= idx[None, :]
    strict = idx[:, None] > idx[None, :]
    gam = jnp.exp(jnp.where(incl, G[..., :, None] - G[..., None, :], -jnp.inf))
    kk = jnp.einsum('bhik,bhjk->bhij', k, k)
    a_mat = jnp.where(strict, beta[..., :, None] * kk * gam, 0.0)
    eG = jnp.exp(G)[..., None]
    rhs = jnp.concatenate([beta[..., None] * v, beta[..., None] * k * eG], axis=-1)
    sol = lax.linalg.triangular_solve(jnp.eye(C, dtype=jnp.float32) + a_mat, rhs,
                                      left_side=True, lower=True, unit_diagonal=True)
    u, w = sol[..., :GDN_DV], sol[..., GDN_DV:]
    v_new = u - jnp.einsum('bhck,bhkv->bhcv', w, S)
    qk = jnp.einsum('bhik,bhjk->bhij', q, k) * gam
    o = jnp.einsum('bhck,bhkv->bhcv', q * eG, S) + jnp.einsum('bhij,bhjv->bhiv', qk, v_new)
    g_last = G[..., -1]
    k_dec = k * jnp.exp(g_last[..., None] - G)[..., None]
    S_new = jnp.exp(g_last)[..., None, None] * S + jnp.einsum('bhck,bhcv->bhkv', k_dec, v_new)
    return S_new, o


def gdn_mixer(h, conv_hist, S0, w_in, conv_w, a_log, dt_bias, norm_g, w_out):
    B, L, _ = h.shape
    proj = h @ w_in
    qkv = proj[..., :GDN_CONV_CH]
    z = proj[..., GDN_CONV_CH:GDN_CONV_CH + GDN_V]
    b_raw = proj[..., GDN_CONV_CH + GDN_V:GDN_CONV_CH + GDN_V + GDN_HEADS]
    a_raw = proj[..., GDN_CONV_CH + GDN_V + GDN_HEADS:]
    qkv, new_hist = causal_dwconv(qkv, conv_hist, conv_w)
    qkv = jax.nn.silu(qkv.astype(jnp.float32))
    q = qkv[..., :GDN_QK].reshape(B, L, GDN_HEADS, GDN_DK)
    k = qkv[..., GDN_QK:2 * GDN_QK].reshape(B, L, GDN_HEADS, GDN_DK)
    v = qkv[..., 2 * GDN_QK:].reshape(B, L, GDN_HEADS, GDN_DV)
    q = l2_normalize(q) * (GDN_DK ** -0.5)
    k = l2_normalize(k)
    beta = jax.nn.sigmoid(b_raw.astype(jnp.float32))
    g = -jnp.exp(a_log.astype(jnp.float32)) * jax.nn.softplus(a_raw.astype(jnp.float32) + dt_bias.astype(jnp.float32))
    c_len = CHUNK if L % CHUNK == 0 else L
    n = L // c_len

    def to_chunks(t):
        t = jnp.moveaxis(t.reshape(B, n, c_len, *t.shape[2:]), 1, 0)
        return jnp.swapaxes(t, 2, 3)

    S_end, o = lax.scan(gdn_chunk_step, S0.astype(jnp.float32),
                        (to_chunks(q), to_chunks(k), to_chunks(v), to_chunks(g), to_chunks(beta)))
    o = jnp.moveaxis(jnp.swapaxes(o, 2, 3), 0, 1).reshape(B, L, GDN_HEADS, GDN_DV)
    o = rms_norm(o, norm_g.astype(jnp.float32)) * jax.nn.silu(z.reshape(B, L, GDN_HEADS, GDN_DV).astype(jnp.float32))
    out = o.reshape(B, L, GDN_V).astype(h.dtype) @ w_out
    return out, new_hist, S_end.astype(S0.dtype)


def chunk_causal_attention(q_nope, q_pe, k_nope, k_pe, v, q_pos, k_pos):
    B, Lq, H, _ = q_nope.shape
    k_chunk = k_pos // CHUNK

    def block(args):
        qn, qp, qpos = args
        s = jnp.einsum('bqhn,bkhn->bhqk', qn, k_nope) + jnp.einsum('bqhr,bkr->bhqk', qp, k_pe)
        s = s.astype(jnp.float32) * MLA_SCALE
        allowed = k_chunk[None, :] <= (qpos // CHUNK)[:, None]
        p = jax.nn.softmax(jnp.where(allowed, s, -jnp.inf), axis=-1).astype(v.dtype)
        return jnp.einsum('bhqk,bkhv->bqhv', p, v)

    if Lq % Q_BLOCK == 0 and Lq > Q_BLOCK:
        nb = Lq // Q_BLOCK
        split = lambda t: jnp.moveaxis(t.reshape(B, nb, Q_BLOCK, *t.shape[2:]), 1, 0)
        out = lax.map(block, (split(q_nope), split(q_pe), q_pos.reshape(nb, Q_BLOCK)))
        return jnp.moveaxis(out, 0, 1).reshape(B, Lq, H, MLA_V)
    return block((q_nope, q_pe, q_pos))


def trunk(x, c, gdn_state, gdn_conv, ffn_conv, ckv_past, kpe_past, p):
    B, L, _ = x.shape
    past = kpe_past.shape[1]
    q_pos = past + jnp.arange(L, dtype=jnp.int32)
    k_pos = jnp.arange(past + L, dtype=jnp.int32)
    cos, sin = rope_tables(q_pos)
    gdn_S_out, gdn_conv_out, ffn_conv_out = [], [], []
    ckv_new = kpe_new = k_nope = v_all = kpe_all = None
    for layer in range(DEPTH):
        shift, scale, gate = ada_terms(c, p['ada_w'][2 * layer], p['ada_b'][2 * layer])
        hin = x * (1.0 + scale) + shift
        if layer < N_A_LAYERS:
            mix, hist, S = gdn_mixer(hin, gdn_conv[:, layer], gdn_state[:, layer],
                                     p['gdn_w_in'][layer], p['gdn_conv_w'][layer], p['gdn_a_log'][layer],
                                     p['gdn_dt_bias'][layer], p['gdn_norm_g'][layer], p['gdn_w_out'][layer])
            gdn_S_out.append(S)
            gdn_conv_out.append(hist)
        else:
            if layer == N_A_LAYERS:
                kv = x @ p['kv_w_down']
                ckv_new = rms_norm(kv[..., :MLA_KV_LORA], p['kv_norm_g'])
                kpe_new = apply_rope(kv[..., MLA_KV_LORA:], cos, sin)
                ckv_all = jnp.concatenate([ckv_past.astype(x.dtype), ckv_new], axis=1)
                kpe_all = jnp.concatenate([kpe_past.astype(x.dtype), kpe_new], axis=1)
                kv_up = (ckv_all @ p['kv_w_up']).reshape(B, past + L, MLA_HEADS, MLA_NOPE + MLA_V)
                k_nope, v_all = kv_up[..., :MLA_NOPE], kv_up[..., MLA_NOPE:]
            j = layer - N_A_LAYERS
            q = rms_norm(hin @ p['mla_w_dq'][j], p['mla_q_norm_g'][j]) @ p['mla_w_uq'][j]
            q = q.reshape(B, L, MLA_HEADS, MLA_NOPE + MLA_ROPE)
            q_pe = apply_rope(q[..., MLA_NOPE:], cos[:, None], sin[:, None])
            o = chunk_causal_attention(q[..., :MLA_NOPE], q_pe, k_nope, kpe_all, v_all, q_pos, k_pos)
            mix = o.reshape(B, L, MLA_HEADS * MLA_V) @ p['mla_w_out'][j]
        x = layer_norm(ALPHA * x + (1.0 + gate) * mix, p['ln_g'][2 * layer], p['ln_b'][2 * layer])
        shift, scale, gate = ada_terms(c, p['ada_w'][2 * layer + 1], p['ada_b'][2 * layer + 1])
        hin = x * (1.0 + scale) + shift
        u = hin @ p['ffn_w_in'][layer] + p['ffn_b_in'][layer]
        u, hist = causal_dwconv(u, ffn_conv[:, layer], p['ffn_conv_w'][layer])
        u = u + p['ffn_conv_b'][layer]
        ffn_conv_out.append(hist)
        f = (jax.nn.silu(u[..., :D_FF]) * u[..., D_FF:]) @ p['ffn_w_down'][layer]
        x = layer_norm(ALPHA * x + (1.0 + gate) * f, p['ln_g'][2 * layer + 1], p['ln_b'][2 * layer + 1])
    return (x, jnp.stack(gdn_S_out, axis=1), jnp.stack(gdn_conv_out, axis=1),
            jnp.stack(ffn_conv_out, axis=1), ckv_new, kpe_new)


def setup_inputs(seed: int = 0) -> dict:
    key = jax.random.key(seed)
    keys = iter(jax.random.split(key, 64))
    f32 = jnp.float32

    def nrm(shape, scale):
        return scale * jax.random.normal(next(keys), shape, f32)

    def unif(shape, lo, hi):
        return jax.random.uniform(next(keys), shape, f32, lo, hi)

    D = D_MODEL
    return {
        'x_prompt': nrm((BATCH, SEQ, D), 1.0),
        'x_sample': nrm((DEC_BATCH, DEC_SEQ, D), 1.0),
        'c_prompt': nrm((BATCH, D), 1.0),
        'c_sample': nrm((DEC_BATCH, D), 1.0),
        'state_gdn': nrm((DEC_BATCH, N_A_LAYERS, GDN_HEADS, GDN_DK, GDN_DV), 0.1),
        'state_gdn_conv': nrm((DEC_BATCH, N_A_LAYERS, GDN_CONV - 1, GDN_CONV_CH), 1.0),
        'state_ffn_conv': nrm((DEC_BATCH, DEPTH, FFN_CONV - 1, 2 * D_FF), 1.0),
        'cache_ckv': nrm((DEC_BATCH, PAST_LEN, MLA_KV_LORA), 1.0),
        'cache_kpe': nrm((DEC_BATCH, PAST_LEN, MLA_ROPE), 1.0),
        'ada_w': nrm((2 * DEPTH, D, 3 * D), 0.1 * D ** -0.5),
        'ada_b': nrm((2 * DEPTH, 3 * D), 0.02),
        'ln_g': 1.0 + nrm((2 * DEPTH, D), 0.02),
        'ln_b': nrm((2 * DEPTH, D), 0.02),
        'gdn_w_in': nrm((N_A_LAYERS, D, GDN_PROJ), D ** -0.5),
        'gdn_conv_w': nrm((N_A_LAYERS, GDN_CONV, GDN_CONV_CH), GDN_CONV ** -0.5),
        'gdn_a_log': jnp.log(unif((N_A_LAYERS, GDN_HEADS), 1.0, 16.0)),
        'gdn_dt_bias': unif((N_A_LAYERS, GDN_HEADS), -4.0, -1.0),
        'gdn_norm_g': 1.0 + nrm((N_A_LAYERS, GDN_DV), 0.02),
        'gdn_w_out': nrm((N_A_LAYERS, GDN_V, D), BETA * GDN_V ** -0.5),
        'kv_w_down': nrm((D, MLA_KV_LORA + MLA_ROPE), D ** -0.5),
        'kv_norm_g': 1.0 + nrm((MLA_KV_LORA,), 0.02),
        'kv_w_up': nrm((MLA_KV_LORA, MLA_HEADS * (MLA_NOPE + MLA_V)), MLA_KV_LORA ** -0.5),
        'mla_w_dq': nrm((N_B_LAYERS, D, MLA_Q_LORA), D ** -0.5),
        'mla_q_norm_g': 1.0 + nrm((N_B_LAYERS, MLA_Q_LORA), 0.02),
        'mla_w_uq': nrm((N_B_LAYERS, MLA_Q_LORA, MLA_HEADS * (MLA_NOPE + MLA_ROPE)), MLA_Q_LORA ** -0.5),
        'mla_w_out': nrm((N_B_LAYERS, MLA_HEADS * MLA_V, D), BETA * (MLA_HEADS * MLA_V) ** -0.5),
        'ffn_w_in': nrm((DEPTH, D, 2 * D_FF), D ** -0.5),
        'ffn_b_in': nrm((DEPTH, 2 * D_FF), 0.02),
        'ffn_conv_w': nrm((DEPTH, FFN_CONV, 2 * D_FF), FFN_CONV ** -0.5),
        'ffn_conv_b': nrm((DEPTH, 2 * D_FF), 0.02),
        'ffn_w_down': nrm((DEPTH, D_FF, D), BETA * D_FF ** -0.5),
    }


def reference(x_prompt, x_sample, c_prompt, c_sample, state_gdn, state_gdn_conv, state_ffn_conv,
              cache_ckv, cache_kpe, ada_w, ada_b, ln_g, ln_b, gdn_w_in, gdn_conv_w, gdn_a_log,
              gdn_dt_bias, gdn_norm_g, gdn_w_out, kv_w_down, kv_norm_g, kv_w_up, mla_w_dq,
              mla_q_norm_g, mla_w_uq, mla_w_out, ffn_w_in, ffn_b_in, ffn_conv_w, ffn_conv_b, ffn_w_down):
    p = {'ada_w': ada_w, 'ada_b': ada_b, 'ln_g': ln_g, 'ln_b': ln_b,
         'gdn_w_in': gdn_w_in, 'gdn_conv_w': gdn_conv_w, 'gdn_a_log': gdn_a_log,
         'gdn_dt_bias': gdn_dt_bias, 'gdn_norm_g': gdn_norm_g, 'gdn_w_out': gdn_w_out,
         'kv_w_down': kv_w_down, 'kv_norm_g': kv_norm_g, 'kv_w_up': kv_w_up,
         'mla_w_dq': mla_w_dq, 'mla_q_norm_g': mla_q_norm_g, 'mla_w_uq': mla_w_uq, 'mla_w_out': mla_w_out,
         'ffn_w_in': ffn_w_in, 'ffn_b_in': ffn_b_in, 'ffn_conv_w': ffn_conv_w,
         'ffn_conv_b': ffn_conv_b, 'ffn_w_down': ffn_w_down}
    bp = x_prompt.shape[0]
    z_gdn = jnp.zeros((bp,) + state_gdn.shape[1:], state_gdn.dtype)
    z_gconv = jnp.zeros((bp,) + state_gdn_conv.shape[1:], state_gdn_conv.dtype)
    z_fconv = jnp.zeros((bp,) + state_ffn_conv.shape[1:], state_ffn_conv.dtype)
    z_ckv = jnp.zeros((bp, 0, MLA_KV_LORA), cache_ckv.dtype)
    z_kpe = jnp.zeros((bp, 0, MLA_ROPE), cache_kpe.dtype)
    y_prompt, p_gdn, p_gconv, p_fconv, p_ckv, p_kpe = trunk(
        x_prompt, c_prompt, z_gdn, z_gconv, z_fconv, z_ckv, z_kpe, p)
    y_sample, s_gdn, s_gconv, s_fconv, s_ckv, s_kpe = trunk(
        x_sample, c_sample, state_gdn, state_gdn_conv, state_ffn_conv, cache_ckv, cache_kpe, p)
    return (y_prompt, y_sample, p_gdn, p_gconv, p_fconv, p_ckv, p_kpe,
            s_gdn, s_gconv, s_fconv, s_ckv, s_kpe)
```

```python
import functools

import jax
import jax.numpy as jnp
from jax import lax
from jax.experimental import pallas as pl
from jax.experimental.pallas import tpu as pltpu

F32 = jnp.float32
BF16 = jnp.bfloat16

D_MODEL = 1024
DEPTH = 2
CHUNK = 64
ALPHA = (2.0 * DEPTH) ** 0.25
LN_EPS = 1e-5
RMS_EPS = 1e-6
GDN_HEADS = 8
GDN_DK = 128
GDN_DV = 128
GDN_CONV = 4
GDN_QK = GDN_HEADS * GDN_DK
GDN_V = GDN_HEADS * GDN_DV
GDN_CONV_CH = 2 * GDN_QK + GDN_V
MLA_HEADS = 8
MLA_NOPE = 128
MLA_ROPE = 64
MLA_V = 128
MLA_KV_LORA = 256
MLA_Q_LORA = 384
ROPE_THETA = 10000.0
MLA_SCALE = (MLA_NOPE + MLA_ROPE) ** -0.5
D_FF = 2816
FFN_CONV = 3

LANES = 128
SUBLANES = 8
HEAD_PAD = 256
VMEM_LIMIT = 56 * 1024 * 1024
NEG_BIG = -1e30


def _row_tile(n, cap):
    if n <= cap:
        return n
    for t in range(cap, 15, -1):
        if n % t == 0 and t % 16 == 0:
            return t
    raise ValueError(f"no row tile for {n}")


def _bdot(a, b):
    return jnp.dot(a.astype(BF16), b.astype(BF16), preferred_element_type=F32)


def _split3(a):
    hi = a.astype(BF16)
    r = a - hi.astype(F32)
    mid = r.astype(BF16)
    lo = (r - mid.astype(F32)).astype(BF16)
    return hi, mid, lo


def _dg(a, b, dims):
    return lax.dot_general(a, b, (dims, ((), ())), preferred_element_type=F32)


def _dot_hp(a, b, dims=((1,), (0,))):
    a0, a1, a2 = _split3(a)
    b0, b1, b2 = _split3(b)
    out = _dg(a0, b0, dims)
    out = out + (_dg(a0, b1, dims) + _dg(a1, b0, dims))
    out = out + (_dg(a0, b2, dims) + _dg(a1, b1, dims) + _dg(a2, b0, dims))
    return out


def _sigmoid(x):
    return 1.0 / (1.0 + jnp.exp(-x))


def _silu(x):
    return x * _sigmoid(x)


def _layer_norm(y, g, b):
    mu = jnp.mean(y, axis=-1, keepdims=True)
    yc = y - mu
    var = jnp.mean(yc * yc, axis=-1, keepdims=True)
    return yc * lax.rsqrt(var + LN_EPS) * g + b


def _const_spec(shape):
    nd = len(shape)
    return pl.BlockSpec(shape, lambda *_: (0,) * nd)


def _params(sem):
    return pltpu.CompilerParams(dimension_semantics=sem, vmem_limit_bytes=VMEM_LIMIT)


def _ada_kernel(c_ref, w_ref, b_ref, o_ref):
    s = _silu(c_ref[...])
    o_ref[0] = _dot_hp(s, w_ref[0]) + b_ref[0]


def _ada_terms(c_all, ada_w, ada_b):
    n_sub, d, n3 = ada_w.shape
    bc = c_all.shape[0]
    tn = 768
    return pl.pallas_call(
        _ada_kernel,
        grid=(n_sub, n3 // tn),
        in_specs=[
            pl.BlockSpec((bc, d), lambda i, j: (0, 0)),
            pl.BlockSpec((1, d, tn), lambda i, j: (i, 0, j)),
            pl.BlockSpec((1, 1, tn), lambda i, j: (i, 0, j)),
        ],
        out_specs=pl.BlockSpec((1, bc, tn), lambda i, j: (i, 0, j)),
        out_shape=jax.ShapeDtypeStruct((n_sub, bc, n3), F32),
        compiler_params=_params(("arbitrary", "arbitrary")),
        name="ada_terms",
    )(c_all, ada_w, ada_b.reshape(n_sub, 1, n3))


def _gdn_in_kernel(x_ref, mod_ref, w_ref, wba_ref, hist_ref, cw_ref, alog_ref, dtb_ref,
                   q_ref, k_ref, v_ref, z_ref, gb_ref, hist_out_ref, ubuf, *, tm, c_len):
    t = pl.program_id(1)

    @pl.when(t == 0)
    def _():
        ubuf[0:SUBLANES, :] = hist_ref[0]

    shift = mod_ref[0, :, 0:D_MODEL]
    scale = mod_ref[0, :, D_MODEL:2 * D_MODEL]
    hin = (x_ref[0] * (1.0 + scale) + shift).astype(BF16)

    outs = (q_ref, k_ref, v_ref)
    for sec in range(3):
        lo = sec * GDN_QK
        u = jnp.dot(hin, w_ref[:, lo:lo + GDN_QK], preferred_element_type=F32)
        ubuf[SUBLANES:SUBLANES + tm, lo:lo + GDN_QK] = u
        acc = u * cw_ref[3:4, lo:lo + GDN_QK]
        for j in range(GDN_CONV - 1):
            off = SUBLANES - (GDN_CONV - 1) + j
            acc = acc + ubuf[off:off + tm, lo:lo + GDN_QK] * cw_ref[j:j + 1, lo:lo + GDN_QK]
        s = _silu(acc)
        if sec == 2:
            v_ref[0] = s
        else:
            post = GDN_DK ** -0.5 if sec == 0 else 1.0
            for h in range(GDN_HEADS):
                sh = s[:, h * GDN_DK:(h + 1) * GDN_DK]
                inv = lax.rsqrt(jnp.sum(sh * sh, axis=-1, keepdims=True) + RMS_EPS)
                outs[sec][0, :, h * GDN_DK:(h + 1) * GDN_DK] = sh * (inv * post)

    z_ref[0] = jnp.dot(hin, w_ref[:, GDN_CONV_CH:GDN_CONV_CH + GDN_V], preferred_element_type=F32)

    ba = jnp.dot(hin, wba_ref[...], preferred_element_type=F32)
    beta = _sigmoid(ba)
    xs = ba + dtb_ref[...]
    softplus = jnp.maximum(xs, 0.0) + jnp.log(1.0 + jnp.exp(-jnp.abs(xs)))
    g = -jnp.exp(alog_ref[...]) * softplus
    ri = lax.broadcasted_iota(jnp.int32, (tm, tm), 0)
    ci = lax.broadcasted_iota(jnp.int32, (tm, tm), 1)
    tri = jnp.where((ri >= ci) & ((ri // c_len) == (ci // c_len)), 1.0, 0.0).astype(BF16)
    g0, g1, g2 = _split3(g)
    gsum = (jnp.dot(tri, g0, preferred_element_type=F32)
            + jnp.dot(tri, g1, preferred_element_type=F32)
            + jnp.dot(tri, g2, preferred_element_type=F32))
    lane = lax.broadcasted_iota(jnp.int32, (tm, LANES), 1)
    gb_ref[0] = jnp.where(lane < GDN_HEADS, beta, gsum)

    tail = ubuf[tm:tm + SUBLANES, :]
    hist_out_ref[0] = tail
    ubuf[0:SUBLANES, :] = tail


def _gdn_in(x, mod, w_main, w_ba, hist8, conv_w, alog_row, dtb_row, c_len):
    b, l, d = x.shape
    tm = _row_tile(l, 256)
    assert tm % c_len == 0
    tok = lambda w: pl.BlockSpec((1, tm, w), lambda i, t: (i, t, 0))
    return pl.pallas_call(
        functools.partial(_gdn_in_kernel, tm=tm, c_len=c_len),
        grid=(b, l // tm),
        in_specs=[
            tok(d),
            pl.BlockSpec((1, 1, 3 * d), lambda i, t: (i, 0, 0)),
            _const_spec(w_main.shape),
            _const_spec(w_ba.shape),
            pl.BlockSpec((1, SUBLANES, GDN_CONV_CH), lambda i, t: (i, 0, 0)),
            _const_spec(conv_w.shape),
            _const_spec(alog_row.shape),
            _const_spec(dtb_row.shape),
        ],
        out_specs=[tok(GDN_QK), tok(GDN_QK), tok(GDN_V), tok(GDN_V), tok(LANES),
                   pl.BlockSpec((1, SUBLANES, GDN_CONV_CH), lambda i, t: (i, 0, 0))],
        out_shape=[
            jax.ShapeDtypeStruct((b, l, GDN_QK), F32),
            jax.ShapeDtypeStruct((b, l, GDN_QK), F32),
            jax.ShapeDtypeStruct((b, l, GDN_V), F32),
            jax.ShapeDtypeStruct((b, l, GDN_V), F32),
            jax.ShapeDtypeStruct((b, l, LANES), F32),
            jax.ShapeDtypeStruct((b, SUBLANES, GDN_CONV_CH), F32),
        ],
        scratch_shapes=[pltpu.VMEM((tm + SUBLANES, GDN_CONV_CH), F32)],
        compiler_params=_params(("arbitrary", "arbitrary")),
        name="gdn_in",
    )(x, mod, w_main, w_ba, hist8, conv_w, alog_row, dtb_row)


def _gdn_chunk_kernel(q_ref, k_ref, v_ref, col_ref, row_ref, s0_ref, o_ref, s_ref, *, c_len):
    n = pl.program_id(1)

    @pl.when(n == 0)
    def _():
        s_ref[...] = s0_ref[...]

    ri = lax.broadcasted_iota(jnp.int32, (c_len, c_len), 0)
    ci = lax.broadcasted_iota(jnp.int32, (c_len, c_len), 1)
    incl = ri >= ci
    strict = ri > ci
    n_sq = c_len.bit_length() - 2
    nt = ((1,), (1,))
    tn = ((0,), (0,))

    col = col_ref[0, 0]
    row = row_ref[0, 0]
    for h in range(GDN_HEADS):
        hs = slice(h * GDN_DK, (h + 1) * GDN_DK)
        qh = q_ref[0, :, hs]
        kh = k_ref[0, :, hs]
        vh = v_ref[0, :, hs]
        beta = col[:, h:h + 1]
        gc = col[:, GDN_HEADS + h:GDN_HEADS + h + 1]
        gr = row[GDN_HEADS + h:GDN_HEADS + h + 1, :]
        gam = jnp.where(incl, jnp.exp(jnp.where(incl, gc - gr, 0.0)), 0.0)
        kk = _dot_hp(kh, kh, nt)
        x = jnp.where(strict, -(beta * kk * gam), 0.0)
        eg = jnp.exp(gc)
        sol = jnp.concatenate([beta * vh, beta * kh * eg], axis=-1)
        sol = sol + _dot_hp(x, sol)
        p = x
        for _ in range(n_sq):
            p = _dot_hp(p, p)
            sol = sol + _dot_hp(p, sol)
        u = sol[:, :GDN_DV]
        w = sol[:, GDN_DV:]
        s_old = s_ref[0, h]
        v_new = u - _dot_hp(w, s_old)
        qk = _dot_hp(qh, kh, nt) * gam
        o_ref[0, :, hs] = _dot_hp(qh * eg, s_old) + _dot_hp(qk, v_new)
        g_last = gc[c_len - 1:c_len, :]
        k_dec = kh * jnp.exp(g_last - gc)
        s_ref[0, h] = jnp.exp(g_last) * s_old + _dot_hp(k_dec, v_new, tn)


def _gdn_chunk(q, k, v, col, row, s0, c_len):
    b, l, _ = q.shape
    n = l // c_len
    tok = pl.BlockSpec((1, c_len, GDN_QK), lambda i, t: (i, t, 0))
    st = pl.BlockSpec((1, GDN_HEADS, GDN_DK, GDN_DV), lambda i, t: (i, 0, 0, 0))
    return pl.pallas_call(
        functools.partial(_gdn_chunk_kernel, c_len=c_len),
        grid=(b, n),
        in_specs=[tok, tok, tok,
                  pl.BlockSpec((1, 1, c_len, 2 * GDN_HEADS), lambda i, t: (i, t, 0, 0)),
                  pl.BlockSpec((1, 1, 2 * GDN_HEADS, c_len), lambda i, t: (i, t, 0, 0)),
                  st],
        out_specs=[tok, st],
        out_shape=[jax.ShapeDtypeStruct((b, l, GDN_V), F32),
                   jax.ShapeDtypeStruct(s0.shape, F32)],
        compiler_params=_params(("arbitrary", "arbitrary")),
        name="gdn_chunk",
    )(q, k, v, col, row, s0)


def _mix_out_kernel(*refs, gated):
    if gated:
        o_ref, z_ref, ng_ref, x_ref, mod_ref, w_ref, lg_ref, lb_ref, y_ref, hbuf = refs
        for h in range(GDN_HEADS):
            hs = slice(h * GDN_DV, (h + 1) * GDN_DV)
            oh = o_ref[0, :, hs]
            r = oh * lax.rsqrt(jnp.mean(oh * oh, axis=-1, keepdims=True) + RMS_EPS) * ng_ref[...]
            hbuf[:, hs] = (r * _silu(z_ref[0, :, hs])).astype(BF16)
        a = hbuf[...]
    else:
        o_ref, x_ref, mod_ref, w_ref, lg_ref, lb_ref, y_ref = refs
        a = o_ref[0].astype(BF16)
    mix = jnp.dot(a, w_ref[...], preferred_element_type=F32)
    gate = mod_ref[0, :, 2 * D_MODEL:3 * D_MODEL]
    y = ALPHA * x_ref[0] + (1.0 + gate) * mix
    y_ref[0] = _layer_norm(y, lg_ref[...], lb_ref[...])


def _mix_out(o, z, norm_g, x, mod, w, ln_g, ln_b):
    b, l, d = x.shape
    tm = _row_tile(l, 512)
    gated = z is not None
    tok = pl.BlockSpec((1, tm, d), lambda i, t: (i, t, 0))
    modspec = pl.BlockSpec((1, 1, 3 * d), lambda i, t: (i, 0, 0))
    row = _const_spec((1, d))
    if gated:
        args = (o, z, norm_g, x, mod, w, ln_g, ln_b)
        in_specs = [tok, tok, _const_spec(norm_g.shape), tok, modspec, _const_spec(w.shape), row, row]
        scratch = [pltpu.VMEM((tm, d), BF16)]
    else:
        args = (o, x, mod, w, ln_g, ln_b)
        in_specs = [tok, tok, modspec, _const_spec(w.shape), row, row]
        scratch = []
    return pl.pallas_call(
        functools.partial(_mix_out_kernel, gated=gated),
        grid=(b, l // tm),
        in_specs=in_specs,
        out_specs=tok,
        out_shape=jax.ShapeDtypeStruct((b, l, d), F32),
        scratch_shapes=scratch,
        compiler_params=_params(("arbitrary", "arbitrary")),
        name="mix_out_gated" if gated else "mix_out",
    )(*args)


FFN_TILE = 256


def _ffn_kernel(x_ref, mod_ref, win_ref, bin_ref, cw_ref, cb_ref, hist_ref, wdn_ref, lg_ref, lb_ref,
                y_ref, hist_out_ref, ubuf, acc_ref, *, tm):
    t = pl.program_id(1)

    @pl.when(t == 0)
    def _():
        ubuf[0:SUBLANES, :] = hist_ref[0]

    x = x_ref[0]
    shift = mod_ref[0, :, 0:D_MODEL]
    scale = mod_ref[0, :, D_MODEL:2 * D_MODEL]
    gate = mod_ref[0, :, 2 * D_MODEL:3 * D_MODEL]
    hin = (x * (1.0 + scale) + shift).astype(BF16)
    acc_ref[...] = jnp.zeros_like(acc_ref)

    def conv_half(lo):
        cols = pl.ds(lo, FFN_TILE)
        u = jnp.dot(hin, win_ref[:, cols], preferred_element_type=F32) + bin_ref[:, cols]
        ubuf[SUBLANES:SUBLANES + tm, cols] = u
        out = u * cw_ref[2:3, cols] + cb_ref[:, cols]
        for j in range(FFN_CONV - 1):
            off = SUBLANES - (FFN_CONV - 1) + j
            out = out + ubuf[off:off + tm, cols] * cw_ref[j:j + 1, cols]
        return out

    def body(i, carry):
        lo = pl.multiple_of(i * FFN_TILE, FFN_TILE)
        ua = conv_half(lo)
        ub = conv_half(pl.multiple_of(lo + D_FF, FFN_TILE))
        f = (_silu(ua) * ub).astype(BF16)
        acc_ref[...] += jnp.dot(f, wdn_ref[pl.ds(lo, FFN_TILE), :], preferred_element_type=F32)
        return carry

    lax.fori_loop(0, D_FF // FFN_TILE, body, 0)

    tail = ubuf[tm:tm + SUBLANES, :]
    hist_out_ref[0] = tail
    ubuf[0:SUBLANES, :] = tail
    y = ALPHA * x + (1.0 + gate) * acc_ref[...]
    y_ref[0] = _layer_norm(y, lg_ref[...], lb_ref[...])


def _ffn(x, mod, w_in, b_in, conv_w, conv_b, hist8, w_down, ln_g, ln_b):
    b, l, d = x.shape
    tm = _row_tile(l, 256)
    tok = pl.BlockSpec((1, tm, d), lambda i, t: (i, t, 0))
    hspec = pl.BlockSpec((1, SUBLANES, 2 * D_FF), lambda i, t: (i, 0, 0))
    return pl.pallas_call(
        functools.partial(_ffn_kernel, tm=tm),
        grid=(b, l // tm),
        in_specs=[tok, pl.BlockSpec((1, 1, 3 * d), lambda i, t: (i, 0, 0)),
                  _const_spec(w_in.shape), _const_spec(b_in.shape), _const_spec(conv_w.shape),
                  _const_spec(conv_b.shape), hspec, _const_spec(w_down.shape),
                  _const_spec((1, d)), _const_spec((1, d))],
        out_specs=[tok, hspec],
        out_shape=[jax.ShapeDtypeStruct((b, l, d), F32),
                   jax.ShapeDtypeStruct((b, SUBLANES, 2 * D_FF), F32)],
        scratch_shapes=[pltpu.VMEM((tm + SUBLANES, 2 * D_FF), F32), pltpu.VMEM((tm, d), F32)],
        compiler_params=_params(("arbitrary", "arbitrary")),
        name="conv_ffn",
    )(x, mod, w_in, b_in, conv_w, conv_b, hist8, w_down, ln_g, ln_b)


def _rope_pair(pair, cs):
    prod = pair * cs
    return prod + pltpu.roll(prod, MLA_ROPE, axis=1)


def _mla_proj_kernel(x_ref, mod_ref, wkv_ref, kvg_ref, cs_ref, wdq_ref, qg_ref, wuq_ref,
                     ckv_ref, kpe_ref, q_ref):
    x = x_ref[0]
    cs = cs_ref[...]
    kv = jnp.dot(x.astype(BF16), wkv_ref[...], preferred_element_type=F32)
    lat = kv[:, :MLA_KV_LORA]
    ckv_ref[0] = lat * lax.rsqrt(jnp.mean(lat * lat, axis=-1, keepdims=True) + RMS_EPS) * kvg_ref[...]
    kpe_ref[0] = _rope_pair(kv[:, MLA_KV_LORA:], cs)[:, :MLA_ROPE]

    shift = mod_ref[0, :, 0:D_MODEL]
    scale = mod_ref[0, :, D_MODEL:2 * D_MODEL]
    hin = (x * (1.0 + scale) + shift).astype(BF16)
    qd = jnp.dot(hin, wdq_ref[...], preferred_element_type=F32)
    qd = qd * lax.rsqrt(jnp.mean(qd * qd, axis=-1, keepdims=True) + RMS_EPS) * qg_ref[...]
    qd = qd.astype(BF16)
    lane = lax.broadcasted_iota(jnp.int32, (x.shape[0], LANES), 1)
    for h in range(MLA_HEADS):
        qh = jnp.dot(qd, wuq_ref[:, h * HEAD_PAD:(h + 1) * HEAD_PAD], preferred_element_type=F32)
        q_ref[0, :, h * HEAD_PAD:h * HEAD_PAD + MLA_NOPE] = (qh[:, :MLA_NOPE] * MLA_SCALE).astype(BF16)
        pe = jnp.where(lane < MLA_ROPE, _rope_pair(qh[:, MLA_NOPE:], cs) * MLA_SCALE, 0.0)
        q_ref[0, :, h * HEAD_PAD + MLA_NOPE:(h + 1) * HEAD_PAD] = pe.astype(BF16)


def _mla_proj(x, mod, w_kv, kv_g, cossin, w_dq, q_g, w_uq):
    b, l, d = x.shape
    tm = _row_tile(l, 256)
    tok = lambda w: pl.BlockSpec((1, tm, w), lambda i, t: (i, t, 0))
    return pl.pallas_call(
        _mla_proj_kernel,
        grid=(b, l // tm),
        in_specs=[tok(d), pl.BlockSpec((1, 1, 3 * d), lambda i, t: (i, 0, 0)),
                  _const_spec(w_kv.shape), _const_spec(kv_g.shape),
                  pl.BlockSpec((tm, LANES), lambda i, t: (t, 0)),
                  _const_spec(w_dq.shape), _const_spec(q_g.shape), _const_spec(w_uq.shape)],
        out_specs=[tok(MLA_KV_LORA), tok(MLA_ROPE), tok(MLA_HEADS * HEAD_PAD)],
        out_shape=[jax.ShapeDtypeStruct((b, l, MLA_KV_LORA), F32),
                   jax.ShapeDtypeStruct((b, l, MLA_ROPE), F32),
                   jax.ShapeDtypeStruct((b, l, MLA_HEADS * HEAD_PAD), BF16)],
        compiler_params=_params(("arbitrary", "arbitrary")),
        name="mla_proj",
    )(x, mod, w_kv, kv_g, cossin, w_dq, q_g, w_uq)


def _kv_up_kernel(ckv_ref, kpe_ref, w_ref, k_ref, v_ref):
    lat = ckv_ref[0].astype(BF16)
    kpe = kpe_ref[0].astype(BF16)
    for h in range(MLA_HEADS):
        kvh = jnp.dot(lat, w_ref[:, h * (MLA_NOPE + MLA_V):(h + 1) * (MLA_NOPE + MLA_V)],
                      preferred_element_type=F32)
        k_ref[0, :, h * HEAD_PAD:h * HEAD_PAD + MLA_NOPE] = kvh[:, :MLA_NOPE].astype(BF16)
        k_ref[0, :, h * HEAD_PAD + MLA_NOPE:(h + 1) * HEAD_PAD] = kpe
        v_ref[0, :, h * MLA_V:(h + 1) * MLA_V] = kvh[:, MLA_NOPE:].astype(BF16)


def _kv_up(ckv_all, kpe_pad, w_up):
    b, lk, _ = ckv_all.shape
    tm = _row_tile(lk, 512)
    tok = lambda w: pl.BlockSpec((1, tm, w), lambda i, t: (i, t, 0))
    return pl.pallas_call(
        _kv_up_kernel,
        grid=(b, lk // tm),
        in_specs=[tok(MLA_KV_LORA), tok(LANES), _const_spec(w_up.shape)],
        out_specs=[tok(MLA_HEADS * HEAD_PAD), tok(MLA_HEADS * MLA_V)],
        out_shape=[jax.ShapeDtypeStruct((b, lk, MLA_HEADS * HEAD_PAD), BF16),
                   jax.ShapeDtypeStruct((b, lk, MLA_HEADS * MLA_V), BF16)],
        compiler_params=_params(("arbitrary", "arbitrary")),
        name="kv_up",
    )(ckv_all, kpe_pad, w_up)


def _last_kblock(i, tq, tk, past, lk):
    kmax = ((past + (i + 1) * tq - 1) // CHUNK + 1) * CHUNK - 1
    return jnp.minimum(kmax, lk - 1) // tk


def _attn_kernel(q_ref, k_ref, v_ref, o_ref, m_ref, l_ref, acc_ref, *, tq, tk, past, lk):
    i = pl.program_id(2)
    j = pl.program_id(3)

    @pl.when(j == 0)
    def _():
        m_ref[...] = jnp.full_like(m_ref, NEG_BIG)
        l_ref[...] = jnp.zeros_like(l_ref)
        acc_ref[...] = jnp.zeros_like(acc_ref)

    @pl.when(j <= _last_kblock(i, tq, tk, past, lk))
    def _():
        s = lax.dot_general(q_ref[0], k_ref[0], (((1,), (1,)), ((), ())), preferred_element_type=F32)
        qpos = past + i * tq + lax.broadcasted_iota(jnp.int32, (tq, tk), 0)
        kpos = j * tk + lax.broadcasted_iota(jnp.int32, (tq, tk), 1)
        s = jnp.where((kpos // CHUNK) <= (qpos // CHUNK), s, NEG_BIG)
        m_old = m_ref[...]
        m_new = jnp.maximum(m_old, jnp.max(s, axis=-1, keepdims=True))
        alpha = jnp.exp(m_old - m_new)
        p = jnp.exp(s - m_new)
        l_ref[...] = alpha * l_ref[...] + jnp.sum(p, axis=-1, keepdims=True)
        acc_ref[...] = alpha * acc_ref[...] + jnp.dot(p.astype(BF16), v_ref[0], preferred_element_type=F32)
        m_ref[...] = m_new

    @pl.when(j == pl.num_programs(3) - 1)
    def _():
        o_ref[0] = acc_ref[...] / l_ref[...]


def _attention(q, k, v, past):
    b, lq, _ = q.shape
    lk = k.shape[1]
    tq = _row_tile(lq, 512)
    tk = tq if lk % tq == 0 and lq > tq else lk
    kidx = lambda bi, h, i, j: (bi, jnp.minimum(j, _last_kblock(i, tq, tk, past, lk)), h)
    return pl.pallas_call(
        functools.partial(_attn_kernel, tq=tq, tk=tk, past=past, lk=lk),
        grid=(b, MLA_HEADS, lq // tq, lk // tk),
        in_specs=[pl.BlockSpec((1, tq, HEAD_PAD), lambda bi, h, i, j: (bi, i, h)),
                  pl.BlockSpec((1, tk, HEAD_PAD), kidx),
                  pl.BlockSpec((1, tk, MLA_V), kidx)],
        out_specs=pl.BlockSpec((1, tq, MLA_V), lambda bi, h, i, j: (bi, i, h)),
        out_shape=jax.ShapeDtypeStruct((b, lq, MLA_HEADS * MLA_V), F32),
        scratch_shapes=[pltpu.VMEM((tq, 1), F32), pltpu.VMEM((tq, 1), F32), pltpu.VMEM((tq, MLA_V), F32)],
        compiler_params=_params(("arbitrary", "arbitrary", "arbitrary", "arbitrary")),
        name="chunk_causal_attention",
    )(q, k, v)


def _pad_hist(hist):
    return jnp.pad(hist, ((0, 0), (SUBLANES - hist.shape[1], 0), (0, 0)))


def _rot_half_cols(w):
    w1, w2 = jnp.split(w, 2, axis=-1)
    return jnp.concatenate([-w2, w1], axis=-1)


def _pack_weights(p):
    d = D_MODEL
    pk = {}
    w_in = p['gdn_w_in'][0]
    pk['gdn_w_main'] = w_in[:, :GDN_CONV_CH + GDN_V].astype(BF16)
    pk['gdn_w_ba'] = jnp.pad(w_in[:, GDN_CONV_CH + GDN_V:], ((0, 0), (0, LANES - 2 * GDN_HEADS))).astype(BF16)
    pad_gate = lambda a: jnp.pad(a.reshape(1, GDN_HEADS), ((0, 0), (GDN_HEADS, LANES - 2 * GDN_HEADS)))
    pk['gdn_alog'] = pad_gate(p['gdn_a_log'][0])
    pk['gdn_dtb'] = pad_gate(p['gdn_dt_bias'][0])
    pk['gdn_conv_w'] = p['gdn_conv_w'][0]
    pk['gdn_norm_g'] = p['gdn_norm_g'][0].reshape(1, GDN_DV)
    pk['gdn_w_out'] = p['gdn_w_out'][0].astype(BF16)
    wkv = p['kv_w_down']
    pk['kv_w_down'] = jnp.concatenate([wkv, _rot_half_cols(wkv[:, MLA_KV_LORA:])], axis=-1).astype(BF16)
    pk['kv_norm_g'] = p['kv_norm_g'].reshape(1, MLA_KV_LORA)
    pk['kv_w_up'] = p['kv_w_up'].astype(BF16)
    pk['mla_w_dq'] = p['mla_w_dq'][0].astype(BF16)
    pk['mla_q_norm_g'] = p['mla_q_norm_g'][0].reshape(1, MLA_Q_LORA)
    wuq = p['mla_w_uq'][0].reshape(MLA_Q_LORA, MLA_HEADS, MLA_NOPE + MLA_ROPE)
    wuq = jnp.concatenate([wuq, _rot_half_cols(wuq[..., MLA_NOPE:])], axis=-1)
    pk['mla_w_uq'] = wuq.reshape(MLA_Q_LORA, MLA_HEADS * HEAD_PAD).astype(BF16)
    pk['mla_w_out'] = p['mla_w_out'][0].astype(BF16)
    pk['ffn_w_in'] = p['ffn_w_in'].astype(BF16)
    pk['ffn_b_in'] = p['ffn_b_in'].reshape(DEPTH, 1, 2 * D_FF)
    pk['ffn_conv_w'] = p['ffn_conv_w']
    pk['ffn_conv_b'] = p['ffn_conv_b'].reshape(DEPTH, 1, 2 * D_FF)
    pk['ffn_w_down'] = p['ffn_w_down'].astype(BF16)
    pk['ln_g'] = p['ln_g'].reshape(2 * DEPTH, 1, d)
    pk['ln_b'] = p['ln_b'].reshape(2 * DEPTH, 1, d)
    return pk


def _rope_table(past, l):
    inv = 1.0 / (ROPE_THETA ** (jnp.arange(0, MLA_ROPE, 2, dtype=F32) / MLA_ROPE))
    ang = (past + jnp.arange(l, dtype=jnp.int32)).astype(F32)[:, None] * inv[None, :]
    ang = jnp.concatenate([ang, ang], axis=-1)
    return jnp.concatenate([jnp.cos(ang), jnp.sin(ang)], axis=-1)


def _ffn_layer(x, mod, hist, pk, layer):
    y, hist8 = _ffn(x, mod, pk['ffn_w_in'][layer], pk['ffn_b_in'][layer], pk['ffn_conv_w'][layer],
                    pk['ffn_conv_b'][layer], _pad_hist(hist), pk['ffn_w_down'][layer],
                    pk['ln_g'][2 * layer + 1], pk['ln_b'][2 * layer + 1])
    return y, hist8[:, SUBLANES - (FFN_CONV - 1):]


def _trunk(x, mods, gdn_state, gdn_conv, ffn_conv, ckv_past, kpe_past, pk):
    b, l, _ = x.shape
    past = kpe_past.shape[1]
    c_len = CHUNK if l % CHUNK == 0 else l
    n = l // c_len

    q, k, v, z, gb, hist8 = _gdn_in(x, mods[0], pk['gdn_w_main'], pk['gdn_w_ba'], _pad_hist(gdn_conv[:, 0]),
                                    pk['gdn_conv_w'], pk['gdn_alog'], pk['gdn_dtb'], c_len)
    col = gb[..., :2 * GDN_HEADS].reshape(b, n, c_len, 2 * GDN_HEADS)
    row = jnp.swapaxes(col, 2, 3)
    o, s_end = _gdn_chunk(q, k, v, col, row, gdn_state[:, 0], c_len)
    x = _mix_out(o, z, pk['gdn_norm_g'], x, mods[0], pk['gdn_w_out'], pk['ln_g'][0], pk['ln_b'][0])
    gdn_conv_out = hist8[:, None, SUBLANES - (GDN_CONV - 1):]
    x, fh0 = _ffn_layer(x, mods[1], ffn_conv[:, 0], pk, 0)

    ckv_new, kpe_new, qm = _mla_proj(x, mods[2], pk['kv_w_down'], pk['kv_norm_g'], _rope_table(past, l),
                                     pk['mla_w_dq'], pk['mla_q_norm_g'], pk['mla_w_uq'])
    ckv_all = jnp.concatenate([ckv_past, ckv_new], axis=1)
    kpe_all = jnp.concatenate([kpe_past, kpe_new], axis=1)
    kpe_pad = jnp.pad(kpe_all, ((0, 0), (0, 0), (0, LANES - MLA_ROPE)))
    km, vm = _kv_up(ckv_all, kpe_pad, pk['kv_w_up'])
    o = _attention(qm, km, vm, past)
    x = _mix_out(o, None, None, x, mods[2], pk['mla_w_out'], pk['ln_g'][2], pk['ln_b'][2])
    x, fh1 = _ffn_layer(x, mods[3], ffn_conv[:, 1], pk, 1)

    return (x, s_end[:, None], gdn_conv_out, jnp.stack([fh0, fh1], axis=1), ckv_new, kpe_new)


def kernel(x_prompt, x_sample, c_prompt, c_sample, state_gdn, state_gdn_conv, state_ffn_conv, cache_ckv, cache_kpe, ada_w, ada_b, ln_g, ln_b, gdn_w_in, gdn_conv_w, gdn_a_log, gdn_dt_bias, gdn_norm_g, gdn_w_out, kv_w_down, kv_norm_g, kv_w_up, mla_w_dq, mla_q_norm_g, mla_w_uq, mla_w_out, ffn_w_in, ffn_b_in, ffn_conv_w, ffn_conv_b, ffn_w_down):
    p = {'ln_g': ln_g, 'ln_b': ln_b, 'gdn_w_in': gdn_w_in, 'gdn_conv_w': gdn_conv_w, 'gdn_a_log': gdn_a_log,
         'gdn_dt_bias': gdn_dt_bias, 'gdn_norm_g': gdn_norm_g, 'gdn_w_out': gdn_w_out,
         'kv_w_down': kv_w_down, 'kv_norm_g': kv_norm_g, 'kv_w_up': kv_w_up,
         'mla_w_dq': mla_w_dq, 'mla_q_norm_g': mla_q_norm_g, 'mla_w_uq': mla_w_uq, 'mla_w_out': mla_w_out,
         'ffn_w_in': ffn_w_in, 'ffn_b_in': ffn_b_in, 'ffn_conv_w': ffn_conv_w,
         'ffn_conv_b': ffn_conv_b, 'ffn_w_down': ffn_w_down}
    pk = _pack_weights(p)
    bp = x_prompt.shape[0]
    mods = _ada_terms(jnp.concatenate([c_prompt, c_sample], axis=0), ada_w, ada_b)[:, :, None, :]
    zeros_like_b = lambda a: jnp.zeros((bp,) + a.shape[1:], a.dtype)
    out_p = _trunk(x_prompt, mods[:, :bp], zeros_like_b(state_gdn), zeros_like_b(state_gdn_conv),
                   zeros_like_b(state_ffn_conv), jnp.zeros((bp, 0, MLA_KV_LORA), cache_ckv.dtype),
                   jnp.zeros((bp, 0, MLA_ROPE), cache_kpe.dtype), pk)
    out_s = _trunk(x_sample, mods[:, bp:], state_gdn, state_gdn_conv, state_ffn_conv, cache_ckv, cache_kpe, pk)
    return (out_p[0], out_s[0]) + out_p[1:] + out_s[1:]
```

```python
import functools

import jax
import jax.numpy as jnp
from jax import lax
from jax.experimental import pallas as pl
from jax.experimental.pallas import tpu as pltpu

F32 = jnp.float32
BF16 = jnp.bfloat16

D_MODEL = 1024
DEPTH = 2
CHUNK = 64
ALPHA = (2.0 * DEPTH) ** 0.25
LN_EPS = 1e-5
RMS_EPS = 1e-6
GDN_HEADS = 8
GDN_DK = 128
GDN_DV = 128
GDN_CONV = 4
GDN_QK = GDN_HEADS * GDN_DK
GDN_V = GDN_HEADS * GDN_DV
GDN_CONV_CH = 2 * GDN_QK + GDN_V
MLA_HEADS = 8
MLA_NOPE = 128
MLA_ROPE = 64
MLA_V = 128
MLA_KV_LORA = 256
MLA_Q_LORA = 384
ROPE_THETA = 10000.0
MLA_SCALE = (MLA_NOPE + MLA_ROPE) ** -0.5
Q_PRESCALE = MLA_SCALE * 1.4426950408889634
D_FF = 2816
FFN_CONV = 3

LANES = 128
SUBLANES = 8
MXU_DIM = 256
HEAD_PAD = 256
VMEM_LIMIT = 56 * 1024 * 1024
NEG_BIG = -1e30

NN = ((1,), (0,))
NT = ((1,), (1,))
TN = ((0,), (0,))


def _row_tile(n, cap):
    if n <= cap:
        return n
    for t in range(cap, 15, -1):
        if n % t == 0 and t % 16 == 0:
            return t
    raise ValueError(f"no row tile for {n}")


def _split3(a):
    hi = a.astype(BF16)
    r = a - hi.astype(F32)
    mid = r.astype(BF16)
    lo = (r - mid.astype(F32)).astype(BF16)
    return hi, mid, lo


def _dg(a, b, dims=NN):
    return lax.dot_general(a, b, (dims, ((), ())), preferred_element_type=F32)


def _dot_hp(a, b, dims=NN):
    a0, a1, a2 = _split3(a)
    b0, b1, b2 = _split3(b)
    out = _dg(a0, b0, dims)
    out = out + (_dg(a0, b1, dims) + _dg(a1, b0, dims))
    out = out + (_dg(a0, b2, dims) + _dg(a1, b1, dims) + _dg(a2, b0, dims))
    return out


def _sigmoid(x):
    return 1.0 / (1.0 + jnp.exp(-x))


def _silu(x):
    return x * _sigmoid(x)


def _layer_norm(y, g, b):
    mu = jnp.mean(y, axis=-1, keepdims=True)
    yc = y - mu
    var = jnp.mean(yc * yc, axis=-1, keepdims=True)
    return yc * lax.rsqrt(var + LN_EPS) * g + b


def _const_spec(shape):
    nd = len(shape)
    return pl.BlockSpec(shape, lambda *_: (0,) * nd, pipeline_mode=pl.Buffered(1))


def _shift_rows(u, carry8, j):
    rolled = pltpu.roll(u, j, axis=0)
    row = lax.broadcasted_iota(jnp.int32, carry8.shape, 0)
    head = jnp.where(row < j, pltpu.roll(carry8, j, axis=0), rolled[:SUBLANES])
    return jnp.concatenate([head, rolled[SUBLANES:]], axis=0)


def _params(sem):
    return pltpu.CompilerParams(dimension_semantics=sem, vmem_limit_bytes=VMEM_LIMIT)


def _ada_kernel(c_ref, w_ref, b_ref, o_ref):
    s = _silu(c_ref[...])
    o_ref[0] = _dot_hp(s, w_ref[0]) + b_ref[0]


def _ada_terms(c_all, ada_w, ada_b):
    n_sub, d, n3 = ada_w.shape
    bc = c_all.shape[0]
    tn = 768
    return pl.pallas_call(
        _ada_kernel,
        grid=(n_sub, n3 // tn),
        in_specs=[
            pl.BlockSpec((bc, d), lambda i, j: (0, 0)),
            pl.BlockSpec((1, d, tn), lambda i, j: (i, 0, j)),
            pl.BlockSpec((1, 1, tn), lambda i, j: (i, 0, j)),
        ],
        out_specs=pl.BlockSpec((1, bc, tn), lambda i, j: (i, 0, j)),
        out_shape=jax.ShapeDtypeStruct((n_sub, bc, n3), F32),
        compiler_params=_params(("arbitrary", "arbitrary")),
        name="ada_terms",
    )(c_all, ada_w, ada_b.reshape(n_sub, 1, n3))


GDN_IN_TILE = 256


def _gdn_in_kernel(x_ref, mod_ref, w_ref, wba_ref, hist_ref, cw_ref, alog_ref, dtb_ref, tri_ref,
                   q_ref, k_ref, v_ref, z_ref, gb_ref, hist_out_ref, carry, *, tm):
    t = pl.program_id(1)

    @pl.when(t == 0)
    def _():
        carry[...] = hist_ref[0]

    shift = mod_ref[0, :, 0:D_MODEL]
    scale = mod_ref[0, :, D_MODEL:2 * D_MODEL]
    hin = (x_ref[0] * (1.0 + scale) + shift).astype(BF16)

    outs = (q_ref, k_ref, v_ref)
    for sec in range(3):
        for c in range(GDN_QK // GDN_IN_TILE):
            lo = sec * GDN_QK + c * GDN_IN_TILE
            cols = slice(lo, lo + GDN_IN_TILE)
            ocols = slice(c * GDN_IN_TILE, (c + 1) * GDN_IN_TILE)
            u = jnp.dot(hin, w_ref[:, cols], preferred_element_type=F32)
            c8 = carry[:, cols]
            acc = u * cw_ref[GDN_CONV - 1:GDN_CONV, cols]
            for j in range(1, GDN_CONV):
                acc = acc + _shift_rows(u, c8, j) * cw_ref[GDN_CONV - 1 - j:GDN_CONV - j, cols]
            carry[:, cols] = u[tm - SUBLANES:tm]
            s = _silu(acc)
            if sec == 2:
                v_ref[0, :, ocols] = s
            else:
                post = GDN_DK ** -0.5 if sec == 0 else 1.0
                parts = []
                for h in range(GDN_IN_TILE // GDN_DK):
                    sh = s[:, h * GDN_DK:(h + 1) * GDN_DK]
                    inv = lax.rsqrt(jnp.sum(sh * sh, axis=-1, keepdims=True) + RMS_EPS)
                    parts.append(sh * (inv * post))
                outs[sec][0, :, ocols] = jnp.concatenate(parts, axis=-1)

    z_ref[0] = jnp.dot(hin, w_ref[:, GDN_CONV_CH:GDN_CONV_CH + GDN_V], preferred_element_type=F32)

    ba = jnp.dot(hin, wba_ref[...], preferred_element_type=F32)
    beta = _sigmoid(ba)
    xs = ba + dtb_ref[...]
    softplus = jnp.maximum(xs, 0.0) + jnp.log(1.0 + jnp.exp(-jnp.abs(xs)))
    g = -jnp.exp(alog_ref[...]) * softplus
    tri = tri_ref[...]
    g0, g1, g2 = _split3(g)
    gsum = (jnp.dot(tri, g0, preferred_element_type=F32)
            + jnp.dot(tri, g1, preferred_element_type=F32)
            + jnp.dot(tri, g2, preferred_element_type=F32))
    lane = lax.broadcasted_iota(jnp.int32, (tm, LANES), 1)
    gb_ref[0] = jnp.where(lane < GDN_HEADS, beta, gsum)
    hist_out_ref[0] = carry[...]


def _gdn_in(x, mod, w_main, w_ba, hist8, conv_w, alog_row, dtb_row, c_len):
    b, l, d = x.shape
    tm = _row_tile(l, 512)
    assert tm % c_len == 0
    idx = jnp.arange(tm, dtype=jnp.int32)
    tri = ((idx[:, None] >= idx[None, :]) & ((idx[:, None] // c_len) == (idx[None, :] // c_len))).astype(BF16)
    tok = lambda w: pl.BlockSpec((1, tm, w), lambda i, t: (i, t, 0))
    return pl.pallas_call(
        functools.partial(_gdn_in_kernel, tm=tm),
        grid=(b, l // tm),
        in_specs=[
            tok(d),
            pl.BlockSpec((1, 1, 3 * d), lambda i, t: (i, 0, 0)),
            _const_spec(w_main.shape),
            _const_spec(w_ba.shape),
            pl.BlockSpec((1, SUBLANES, GDN_CONV_CH), lambda i, t: (i, 0, 0)),
            _const_spec(conv_w.shape),
            _const_spec(alog_row.shape),
            _const_spec(dtb_row.shape),
            _const_spec(tri.shape),
        ],
        out_specs=[tok(GDN_QK), tok(GDN_QK), tok(GDN_V), tok(GDN_V), tok(LANES),
                   pl.BlockSpec((1, SUBLANES, GDN_CONV_CH), lambda i, t: (i, 0, 0))],
        out_shape=[
            jax.ShapeDtypeStruct((b, l, GDN_QK), F32),
            jax.ShapeDtypeStruct((b, l, GDN_QK), F32),
            jax.ShapeDtypeStruct((b, l, GDN_V), F32),
            jax.ShapeDtypeStruct((b, l, GDN_V), F32),
            jax.ShapeDtypeStruct((b, l, LANES), F32),
            jax.ShapeDtypeStruct((b, SUBLANES, GDN_CONV_CH), F32),
        ],
        scratch_shapes=[pltpu.VMEM((SUBLANES, GDN_CONV_CH), F32)],
        compiler_params=_params(("arbitrary", "arbitrary")),
        name="gdn_in",
    )(x, mod, w_main, w_ba, hist8, conv_w, alog_row, dtb_row, tri)


GDN_GROUP = 4


def _bf(xs):
    return [x.astype(BF16) for x in xs]


def _gdn_chunk_kernel(q_ref, k_ref, v_ref, col_ref, row_ref, s0_ref, o_ref, s_ref, *, c_len, group):
    @pl.when(pl.program_id(1) == 0)
    def _():
        s_ref[...] = s0_ref[...]

    ri = lax.broadcasted_iota(jnp.int32, (c_len, c_len), 0)
    ci = lax.broadcasted_iota(jnp.int32, (c_len, c_len), 1)
    incl = ri >= ci
    strict = ri > ci
    n_sq = c_len.bit_length() - 2
    heads = range(GDN_HEADS)

    def chunk(g):
        rows = pl.ds(pl.multiple_of(g * c_len, c_len), c_len)
        col = col_ref[0, g]
        row = row_ref[0, g]
        q = [q_ref[0, rows, h * GDN_DK:(h + 1) * GDN_DK] for h in heads]
        k = [k_ref[0, rows, h * GDN_DK:(h + 1) * GDN_DK] for h in heads]
        v = [v_ref[0, rows, h * GDN_DV:(h + 1) * GDN_DV] for h in heads]
        beta = [col[:, h:h + 1] for h in heads]
        gc = [col[:, GDN_HEADS + h:GDN_HEADS + h + 1] for h in heads]
        gr = [row[GDN_HEADS + h:GDN_HEADS + h + 1, :] for h in heads]
        gam = [jnp.where(incl, jnp.exp(jnp.where(incl, gc[h] - gr[h], 0.0)), 0.0) for h in heads]
        eg = [jnp.exp(gc[h]) for h in heads]
        qb, kb = _bf(q), _bf(k)
        kk = [_dg(kb[h], kb[h], NT) for h in heads]
        qk = [_dg(qb[h], kb[h], NT) for h in heads]
        p = _bf([jnp.where(strict, -(beta[h] * kk[h] * gam[h]), 0.0) for h in heads])
        sol = [jnp.concatenate([beta[h] * v[h], beta[h] * k[h] * eg[h]], axis=-1) for h in heads]
        solb = _bf(sol)
        sol = [sol[h] + _dg(p[h], solb[h]) for h in heads]
        for _ in range(n_sq):
            p = _bf([_dg(p[h], p[h]) for h in heads])
            solb = _bf(sol)
            sol = [sol[h] + _dg(p[h], solb[h]) for h in heads]
        s_old = [s_ref[0, h] for h in heads]
        sb = _bf(s_old)
        wb = _bf([sol[h][:, GDN_DV:] for h in heads])
        v_new = [sol[h][:, :GDN_DV] - _dg(wb[h], sb[h]) for h in heads]
        vb = _bf(v_new)
        qe = _bf([q[h] * eg[h] for h in heads])
        att = _bf([qk[h] * gam[h] for h in heads])
        o = [_dg(qe[h], sb[h]) + _dg(att[h], vb[h]) for h in heads]
        g_last = [gc[h][c_len - 1:c_len, :] for h in heads]
        k_dec = _bf([k[h] * jnp.exp(g_last[h] - gc[h]) for h in heads])
        s_new = [jnp.exp(g_last[h]) * s_old[h] + _dg(k_dec[h], vb[h], TN) for h in heads]
        o_ref[0, rows, :] = jnp.concatenate(o, axis=-1)
        for h in heads:
            s_ref[0, h] = s_new[h]

    if group == 1:
        chunk(0)
    else:
        def body(g, c):
            chunk(g)
            return c
        lax.fori_loop(0, group, body, 0)


def _gdn_chunk(q, k, v, col, row, s0, c_len):
    b, l, _ = q.shape
    n = l // c_len
    group = GDN_GROUP if n % GDN_GROUP == 0 else 1
    rows = c_len * group
    tok = pl.BlockSpec((1, rows, GDN_QK), lambda i, t: (i, t, 0))
    st = pl.BlockSpec((1, GDN_HEADS, GDN_DK, GDN_DV), lambda i, t: (i, 0, 0, 0))
    return pl.pallas_call(
        functools.partial(_gdn_chunk_kernel, c_len=c_len, group=group),
        grid=(b, n // group),
        in_specs=[tok, tok, tok,
                  pl.BlockSpec((1, group, c_len, 2 * GDN_HEADS), lambda i, t: (i, t, 0, 0)),
                  pl.BlockSpec((1, group, 2 * GDN_HEADS, c_len), lambda i, t: (i, t, 0, 0)),
                  st],
        out_specs=[tok, st],
        out_shape=[jax.ShapeDtypeStruct((b, l, GDN_V), F32),
                   jax.ShapeDtypeStruct(s0.shape, F32)],
        compiler_params=_params(("arbitrary", "arbitrary")),
        name="gdn_chunk",
    )(q, k, v, col, row, s0)


def _mix_out_kernel(*refs, gated):
    if gated:
        o_ref, z_ref, ng_ref, x_ref, mod_ref, w_ref, lg_ref, lb_ref, y_ref, hbuf = refs
        for h in range(GDN_HEADS):
            hs = slice(h * GDN_DV, (h + 1) * GDN_DV)
            oh = o_ref[0, :, hs]
            r = oh * lax.rsqrt(jnp.mean(oh * oh, axis=-1, keepdims=True) + RMS_EPS) * ng_ref[...]
            hbuf[:, hs] = (r * _silu(z_ref[0, :, hs])).astype(BF16)
        a = hbuf[...]
    else:
        o_ref, x_ref, mod_ref, w_ref, lg_ref, lb_ref, y_ref = refs
        a = o_ref[0].astype(BF16)
    mix = jnp.dot(a, w_ref[...], preferred_element_type=F32)
    gate = mod_ref[0, :, 2 * D_MODEL:3 * D_MODEL]
    y = ALPHA * x_ref[0] + (1.0 + gate) * mix
    y_ref[0] = _layer_norm(y, lg_ref[...], lb_ref[...])


def _mix_out(o, z, norm_g, x, mod, w, ln_g, ln_b):
    b, l, d = x.shape
    tm = _row_tile(l, 512)
    gated = z is not None
    tok = pl.BlockSpec((1, tm, d), lambda i, t: (i, t, 0))
    modspec = pl.BlockSpec((1, 1, 3 * d), lambda i, t: (i, 0, 0))
    row = _const_spec((1, d))
    if gated:
        args = (o, z, norm_g, x, mod, w, ln_g, ln_b)
        in_specs = [tok, tok, _const_spec(norm_g.shape), tok, modspec, _const_spec(w.shape), row, row]
        scratch = [pltpu.VMEM((tm, d), BF16)]
    else:
        args = (o, x, mod, w, ln_g, ln_b)
        in_specs = [tok, tok, modspec, _const_spec(w.shape), row, row]
        scratch = []
    return pl.pallas_call(
        functools.partial(_mix_out_kernel, gated=gated),
        grid=(b, l // tm),
        in_specs=in_specs,
        out_specs=tok,
        out_shape=jax.ShapeDtypeStruct((b, l, d), F32),
        scratch_shapes=scratch,
        compiler_params=_params(("arbitrary", "arbitrary")),
        name="mix_out_gated" if gated else "mix_out",
    )(*args)


FFN_TILE = MXU_DIM
FFN_ROWS = 512


def _ffn_kernel(x_ref, mod_ref, win_ref, bin_ref, cw_ref, cb_ref, hist_ref, wdn_ref, lg_ref, lb_ref,
                y_ref, hist_out_ref, carry, hbuf, fbuf, *, tm):
    t = pl.program_id(1)

    @pl.when(t == 0)
    def _():
        carry[...] = hist_ref[0]

    shift = mod_ref[0, :, 0:D_MODEL]
    scale = mod_ref[0, :, D_MODEL:2 * D_MODEL]
    gate = mod_ref[0, :, 2 * D_MODEL:3 * D_MODEL]
    hbuf[...] = (x_ref[0] * (1.0 + scale) + shift).astype(BF16)

    def conv_half(lo):
        cols = pl.ds(lo, FFN_TILE)
        u = jnp.dot(hbuf[...], win_ref[:, cols], preferred_element_type=F32) + bin_ref[:, cols]
        c8 = carry[:, cols]
        out = u * cw_ref[FFN_CONV - 1:FFN_CONV, cols] + cb_ref[:, cols]
        for j in range(1, FFN_CONV):
            out = out + _shift_rows(u, c8, j) * cw_ref[FFN_CONV - 1 - j:FFN_CONV - j, cols]
        carry[:, cols] = u[tm - SUBLANES:tm]
        return out

    def body(i, c):
        lo = pl.multiple_of(i * FFN_TILE, FFN_TILE)
        ua = conv_half(lo)
        ub = conv_half(pl.multiple_of(lo + D_FF, FFN_TILE))
        fbuf[:, pl.ds(lo, FFN_TILE)] = (_silu(ua) * ub).astype(BF16)
        return c

    lax.fori_loop(0, D_FF // FFN_TILE, body, 0)

    hist_out_ref[0] = carry[...]
    f = jnp.dot(fbuf[...], wdn_ref[...], preferred_element_type=F32)
    y = ALPHA * x_ref[0] + (1.0 + gate) * f
    y_ref[0] = _layer_norm(y, lg_ref[...], lb_ref[...])


def _ffn(x, mod, w_in, b_in, conv_w, conv_b, hist8, w_down, ln_g, ln_b):
    b, l, d = x.shape
    tm = _row_tile(l, FFN_ROWS)
    tok = pl.BlockSpec((1, tm, d), lambda i, t: (i, t, 0))
    hspec = pl.BlockSpec((1, SUBLANES, 2 * D_FF), lambda i, t: (i, 0, 0))
    return pl.pallas_call(
        functools.partial(_ffn_kernel, tm=tm),
        grid=(b, l // tm),
        in_specs=[tok, pl.BlockSpec((1, 1, 3 * d), lambda i, t: (i, 0, 0)),
                  _const_spec(w_in.shape), _const_spec(b_in.shape), _const_spec(conv_w.shape),
                  _const_spec(conv_b.shape), hspec, _const_spec(w_down.shape),
                  _const_spec((1, d)), _const_spec((1, d))],
        out_specs=[tok, hspec],
        out_shape=[jax.ShapeDtypeStruct((b, l, d), F32),
                   jax.ShapeDtypeStruct((b, SUBLANES, 2 * D_FF), F32)],
        scratch_shapes=[pltpu.VMEM((SUBLANES, 2 * D_FF), F32), pltpu.VMEM((tm, d), BF16),
                        pltpu.VMEM((tm, D_FF), BF16)],
        compiler_params=_params(("arbitrary", "arbitrary")),
        name="conv_ffn",
    )(x, mod, w_in, b_in, conv_w, conv_b, hist8, w_down, ln_g, ln_b)


def _rope_pair(pair, cs):
    prod = pair * cs
    return prod + pltpu.roll(prod, MLA_ROPE, axis=1)


def _mla_proj_kernel(x_ref, mod_ref, wkv_ref, kvg_ref, cs_ref, wdq_ref, qg_ref, wuq_ref,
                     ckv_ref, kpe_ref, q_ref):
    x = x_ref[0]
    cs = cs_ref[...]
    kv = jnp.dot(x.astype(BF16), wkv_ref[...], preferred_element_type=F32)
    lat = kv[:, :MLA_KV_LORA]
    ckv_ref[0] = lat * lax.rsqrt(jnp.mean(lat * lat, axis=-1, keepdims=True) + RMS_EPS) * kvg_ref[...]
    kpe_ref[0] = _rope_pair(kv[:, MLA_KV_LORA:], cs)[:, :MLA_ROPE]

    shift = mod_ref[0, :, 0:D_MODEL]
    scale = mod_ref[0, :, D_MODEL:2 * D_MODEL]
    hin = (x * (1.0 + scale) + shift).astype(BF16)
    qd = jnp.dot(hin, wdq_ref[...], preferred_element_type=F32)
    qd = qd * lax.rsqrt(jnp.mean(qd * qd, axis=-1, keepdims=True) + RMS_EPS) * qg_ref[...]
    qd = qd.astype(BF16)
    lane = lax.broadcasted_iota(jnp.int32, (x.shape[0], LANES), 1)
    for h in range(MLA_HEADS):
        qh = jnp.dot(qd, wuq_ref[:, h * HEAD_PAD:(h + 1) * HEAD_PAD], preferred_element_type=F32)
        q_ref[0, :, h * HEAD_PAD:h * HEAD_PAD + MLA_NOPE] = (qh[:, :MLA_NOPE] * Q_PRESCALE).astype(BF16)
        pe = jnp.where(lane < MLA_ROPE, _rope_pair(qh[:, MLA_NOPE:], cs) * Q_PRESCALE, 0.0)
        q_ref[0, :, h * HEAD_PAD + MLA_NOPE:(h + 1) * HEAD_PAD] = pe.astype(BF16)


def _mla_proj(x, mod, w_kv, kv_g, cossin, w_dq, q_g, w_uq):
    b, l, d = x.shape
    tm = _row_tile(l, 256)
    tok = lambda w: pl.BlockSpec((1, tm, w), lambda i, t: (i, t, 0))
    return pl.pallas_call(
        _mla_proj_kernel,
        grid=(b, l // tm),
        in_specs=[tok(d), pl.BlockSpec((1, 1, 3 * d), lambda i, t: (i, 0, 0)),
                  _const_spec(w_kv.shape), _const_spec(kv_g.shape),
                  pl.BlockSpec((tm, LANES), lambda i, t: (t, 0)),
                  _const_spec(w_dq.shape), _const_spec(q_g.shape), _const_spec(w_uq.shape)],
        out_specs=[tok(MLA_KV_LORA), tok(MLA_ROPE), tok(MLA_HEADS * HEAD_PAD)],
        out_shape=[jax.ShapeDtypeStruct((b, l, MLA_KV_LORA), F32),
                   jax.ShapeDtypeStruct((b, l, MLA_ROPE), F32),
                   jax.ShapeDtypeStruct((b, l, MLA_HEADS * HEAD_PAD), BF16)],
        compiler_params=_params(("arbitrary", "arbitrary")),
        name="mla_proj",
    )(x, mod, w_kv, kv_g, cossin, w_dq, q_g, w_uq)


def _kv_up_kernel(ckv_ref, kpe_ref, wk_ref, wvt_ref, k_ref, vt_ref):
    lat = ckv_ref[0].astype(BF16)
    kpe = kpe_ref[0].astype(BF16)
    for h in range(MLA_HEADS):
        kn = jnp.dot(lat, wk_ref[:, h * MLA_NOPE:(h + 1) * MLA_NOPE], preferred_element_type=F32)
        k_ref[0, :, h * HEAD_PAD:h * HEAD_PAD + MLA_NOPE] = kn.astype(BF16)
        k_ref[0, :, h * HEAD_PAD + MLA_NOPE:(h + 1) * HEAD_PAD] = kpe
        vt_ref[0, h * MLA_V:(h + 1) * MLA_V, :] = _dg(wvt_ref[h], lat, NT).astype(BF16)


def _kv_up(ckv_all, kpe_pad, w_k, w_vt):
    b, lk, _ = ckv_all.shape
    tm = ATTN_TK if lk % ATTN_TK == 0 else lk
    tok = lambda w: pl.BlockSpec((1, tm, w), lambda i, t: (i, t, 0))
    return pl.pallas_call(
        _kv_up_kernel,
        grid=(b, lk // tm),
        in_specs=[tok(MLA_KV_LORA), tok(LANES), _const_spec(w_k.shape), _const_spec(w_vt.shape)],
        out_specs=[tok(MLA_HEADS * HEAD_PAD),
                   pl.BlockSpec((1, MLA_HEADS * MLA_V, tm), lambda i, t: (i, 0, t))],
        out_shape=[jax.ShapeDtypeStruct((b, lk, MLA_HEADS * HEAD_PAD), BF16),
                   jax.ShapeDtypeStruct((b, MLA_HEADS * MLA_V, lk), BF16)],
        compiler_params=_params(("arbitrary", "arbitrary")),
        name="kv_up",
    )(ckv_all, kpe_pad, w_k, w_vt)


ATTN_TQ = 512
ATTN_TK = 512
ATTN_HEADS_PER_STEP = 4
CHUNK_SHIFT = CHUNK.bit_length() - 1


def _chunk_end(pos):
    return (lax.shift_right_logical(pos, CHUNK_SHIFT) + 1) * CHUNK


def _attn_kernel(q_ref, k_ref, vt_ref, o_ref, acc_ref, *, tq, tk, past, lk, nh):
    i = pl.program_id(2)
    acc_ref[...] = jnp.zeros_like(acc_ref)
    q0 = past + i * tq
    n_full = lax.div(jnp.minimum(_chunk_end(q0), lk), tk)
    n_end = lax.div(jnp.minimum(_chunk_end(q0 + tq - 1), lk) + tk - 1, tk)
    heads = range(nh)

    def block(j, carry, masked):
        m_old, l_old = carry
        k0 = pl.multiple_of(j * tk, tk) if lk > tk else 0
        st = [_dg(k_ref[0, pl.ds(k0, tk), h * HEAD_PAD:(h + 1) * HEAD_PAD],
                  q_ref[0, :, h * HEAD_PAD:(h + 1) * HEAD_PAD], NT) for h in heads]
        if masked:
            kc = lax.shift_right_logical(k0 + lax.broadcasted_iota(jnp.int32, (tk, tq), 0), CHUNK_SHIFT)
            qc = lax.shift_right_logical(q0 + lax.broadcasted_iota(jnp.int32, (tk, tq), 1), CHUNK_SHIFT)
            visible = kc <= qc
            st = [jnp.where(visible, s, NEG_BIG) for s in st]
        m_new = [jnp.maximum(m_old[h], jnp.max(st[h], axis=0, keepdims=True)) for h in heads]
        alpha = [jnp.exp2(m_old[h] - m_new[h]) for h in heads]
        pt = [jnp.exp2(st[h] - m_new[h]) for h in heads]
        l_new = [alpha[h] * l_old[h] + jnp.sum(pt[h], axis=0, keepdims=True) for h in heads]
        pv = [_dg(vt_ref[0, h * MLA_V:(h + 1) * MLA_V, pl.ds(k0, tk)], pt[h].astype(BF16)) for h in heads]
        for h in heads:
            acc_ref[h] = alpha[h] * acc_ref[h] + pv[h]
        return tuple(m_new), tuple(l_new)

    init = (tuple(jnp.full((1, tq), NEG_BIG, F32) for _ in heads), tuple(jnp.zeros((1, tq), F32) for _ in heads))
    carry = lax.fori_loop(0, n_full, functools.partial(block, masked=False), init)
    _, l_fin = lax.fori_loop(n_full, n_end, functools.partial(block, masked=True), carry)
    for h in heads:
        o_ref[0, :, h * MLA_V:(h + 1) * MLA_V] = (acc_ref[h] / l_fin[h]).T


def _attention(q, k, vt, past):
    b, lq, _ = q.shape
    lk = k.shape[1]
    lq_pad = -(-lq // LANES) * LANES
    if lq_pad != lq:
        q = jnp.pad(q, ((0, 0), (0, lq_pad - lq), (0, 0)))
    tq = _row_tile(lq_pad, ATTN_TQ)
    tk = ATTN_TK if lk % ATTN_TK == 0 else lk
    nh = ATTN_HEADS_PER_STEP
    out = pl.pallas_call(
        functools.partial(_attn_kernel, tq=tq, tk=tk, past=past, lk=lk, nh=nh),
        grid=(b, MLA_HEADS // nh, lq_pad // tq),
        in_specs=[pl.BlockSpec((1, tq, nh * HEAD_PAD), lambda bi, h, i: (bi, i, h)),
                  pl.BlockSpec((1, lk, nh * HEAD_PAD), lambda bi, h, i: (bi, 0, h), pipeline_mode=pl.Buffered(1)),
                  pl.BlockSpec((1, nh * MLA_V, lk), lambda bi, h, i: (bi, h, 0), pipeline_mode=pl.Buffered(1))],
        out_specs=pl.BlockSpec((1, tq, nh * MLA_V), lambda bi, h, i: (bi, i, h)),
        out_shape=jax.ShapeDtypeStruct((b, lq_pad, MLA_HEADS * MLA_V), F32),
        scratch_shapes=[pltpu.VMEM((nh, MLA_V, tq), F32)],
        compiler_params=_params(("arbitrary", "arbitrary", "arbitrary")),
        name="chunk_causal_attention",
    )(q, k, vt)
    return out[:, :lq]


def _pad_hist(hist):
    return jnp.pad(hist, ((0, 0), (SUBLANES - hist.shape[1], 0), (0, 0)))


def _rot_half_cols(w):
    w1, w2 = jnp.split(w, 2, axis=-1)
    return jnp.concatenate([-w2, w1], axis=-1)


def _pack_weights(p):
    d = D_MODEL
    pk = {}
    w_in = p['gdn_w_in'][0]
    pk['gdn_w_main'] = w_in[:, :GDN_CONV_CH + GDN_V].astype(BF16)
    pk['gdn_w_ba'] = jnp.pad(w_in[:, GDN_CONV_CH + GDN_V:], ((0, 0), (0, LANES - 2 * GDN_HEADS))).astype(BF16)
    pad_gate = lambda a: jnp.pad(a.reshape(1, GDN_HEADS), ((0, 0), (GDN_HEADS, LANES - 2 * GDN_HEADS)))
    pk['gdn_alog'] = pad_gate(p['gdn_a_log'][0])
    pk['gdn_dtb'] = pad_gate(p['gdn_dt_bias'][0])
    pk['gdn_conv_w'] = p['gdn_conv_w'][0]
    pk['gdn_norm_g'] = p['gdn_norm_g'][0].reshape(1, GDN_DV)
    pk['gdn_w_out'] = p['gdn_w_out'][0].astype(BF16)
    wkv = p['kv_w_down']
    pk['kv_w_down'] = jnp.concatenate([wkv, _rot_half_cols(wkv[:, MLA_KV_LORA:])], axis=-1).astype(BF16)
    pk['kv_norm_g'] = p['kv_norm_g'].reshape(1, MLA_KV_LORA)
    w_up = p['kv_w_up'].reshape(MLA_KV_LORA, MLA_HEADS, MLA_NOPE + MLA_V)
    pk['kv_w_k'] = w_up[..., :MLA_NOPE].reshape(MLA_KV_LORA, MLA_HEADS * MLA_NOPE).astype(BF16)
    pk['kv_w_vt'] = jnp.transpose(w_up[..., MLA_NOPE:], (1, 2, 0)).astype(BF16)
    pk['mla_w_dq'] = p['mla_w_dq'][0].astype(BF16)
    pk['mla_q_norm_g'] = p['mla_q_norm_g'][0].reshape(1, MLA_Q_LORA)
    wuq = p['mla_w_uq'][0].reshape(MLA_Q_LORA, MLA_HEADS, MLA_NOPE + MLA_ROPE)
    wuq = jnp.concatenate([wuq, _rot_half_cols(wuq[..., MLA_NOPE:])], axis=-1)
    pk['mla_w_uq'] = wuq.reshape(MLA_Q_LORA, MLA_HEADS * HEAD_PAD).astype(BF16)
    pk['mla_w_out'] = p['mla_w_out'][0].astype(BF16)
    pk['ffn_w_in'] = p['ffn_w_in'].astype(BF16)
    pk['ffn_b_in'] = p['ffn_b_in'].reshape(DEPTH, 1, 2 * D_FF)
    pk['ffn_conv_w'] = p['ffn_conv_w']
    pk['ffn_conv_b'] = p['ffn_conv_b'].reshape(DEPTH, 1, 2 * D_FF)
    pk['ffn_w_down'] = p['ffn_w_down'].astype(BF16)
    pk['ln_g'] = p['ln_g'].reshape(2 * DEPTH, 1, d)
    pk['ln_b'] = p['ln_b'].reshape(2 * DEPTH, 1, d)
    return pk


def _rope_table(past, l):
    inv = 1.0 / (ROPE_THETA ** (jnp.arange(0, MLA_ROPE, 2, dtype=F32) / MLA_ROPE))
    ang = (past + jnp.arange(l, dtype=jnp.int32)).astype(F32)[:, None] * inv[None, :]
    ang = jnp.concatenate([ang, ang], axis=-1)
    return jnp.concatenate([jnp.cos(ang), jnp.sin(ang)], axis=-1)


def _ffn_layer(x, mod, hist, pk, layer):
    y, hist8 = _ffn(x, mod, pk['ffn_w_in'][layer], pk['ffn_b_in'][layer], pk['ffn_conv_w'][layer],
                    pk['ffn_conv_b'][layer], _pad_hist(hist), pk['ffn_w_down'][layer],
                    pk['ln_g'][2 * layer + 1], pk['ln_b'][2 * layer + 1])
    return y, hist8[:, SUBLANES - (FFN_CONV - 1):]


def _trunk(x, mods, gdn_state, gdn_conv, ffn_conv, ckv_past, kpe_past, pk):
    b, l, _ = x.shape
    past = kpe_past.shape[1]
    c_len = CHUNK if l % CHUNK == 0 else l
    n = l // c_len

    q, k, v, z, gb, hist8 = _gdn_in(x, mods[0], pk['gdn_w_main'], pk['gdn_w_ba'], _pad_hist(gdn_conv[:, 0]),
                                    pk['gdn_conv_w'], pk['gdn_alog'], pk['gdn_dtb'], c_len)
    col = gb[..., :2 * GDN_HEADS].reshape(b, n, c_len, 2 * GDN_HEADS)
    row = jnp.swapaxes(col, 2, 3)
    o, s_end = _gdn_chunk(q, k, v, col, row, gdn_state[:, 0], c_len)
    x = _mix_out(o, z, pk['gdn_norm_g'], x, mods[0], pk['gdn_w_out'], pk['ln_g'][0], pk['ln_b'][0])
    gdn_conv_out = hist8[:, None, SUBLANES - (GDN_CONV - 1):]
    x, fh0 = _ffn_layer(x, mods[1], ffn_conv[:, 0], pk, 0)

    ckv_new, kpe_new, qm = _mla_proj(x, mods[2], pk['kv_w_down'], pk['kv_norm_g'], _rope_table(past, l),
                                     pk['mla_w_dq'], pk['mla_q_norm_g'], pk['mla_w_uq'])
    ckv_all = jnp.concatenate([ckv_past, ckv_new], axis=1)
    kpe_all = jnp.concatenate([kpe_past, kpe_new], axis=1)
    kpe_pad = jnp.pad(kpe_all, ((0, 0), (0, 0), (0, LANES - MLA_ROPE)))
    km, vm = _kv_up(ckv_all, kpe_pad, pk['kv_w_k'], pk['kv_w_vt'])
    o = _attention(qm, km, vm, past)
    x = _mix_out(o, None, None, x, mods[2], pk['mla_w_out'], pk['ln_g'][2], pk['ln_b'][2])
    x, fh1 = _ffn_layer(x, mods[3], ffn_conv[:, 1], pk, 1)

    return (x, s_end[:, None], gdn_conv_out, jnp.stack([fh0, fh1], axis=1), ckv_new, kpe_new)


def kernel(x_prompt, x_sample, c_prompt, c_sample, state_gdn, state_gdn_conv, state_ffn_conv, cache_ckv, cache_kpe, ada_w, ada_b, ln_g, ln_b, gdn_w_in, gdn_conv_w, gdn_a_log, gdn_dt_bias, gdn_norm_g, gdn_w_out, kv_w_down, kv_norm_g, kv_w_up, mla_w_dq, mla_q_norm_g, mla_w_uq, mla_w_out, ffn_w_in, ffn_b_in, ffn_conv_w, ffn_conv_b, ffn_w_down):
    p = {'ln_g': ln_g, 'ln_b': ln_b, 'gdn_w_in': gdn_w_in, 'gdn_conv_w': gdn_conv_w, 'gdn_a_log': gdn_a_log,
         'gdn_dt_bias': gdn_dt_bias, 'gdn_norm_g': gdn_norm_g, 'gdn_w_out': gdn_w_out,
         'kv_w_down': kv_w_down, 'kv_norm_g': kv_norm_g, 'kv_w_up': kv_w_up,
         'mla_w_dq': mla_w_dq, 'mla_q_norm_g': mla_q_norm_g, 'mla_w_uq': mla_w_uq, 'mla_w_out': mla_w_out,
         'ffn_w_in': ffn_w_in, 'ffn_b_in': ffn_b_in, 'ffn_conv_w': ffn_conv_w,
         'ffn_conv_b': ffn_conv_b, 'ffn_w_down': ffn_w_down}
    pk = _pack_weights(p)
    bp = x_prompt.shape[0]
    mods = _ada_terms(jnp.concatenate([c_prompt, c_sample], axis=0), ada_w, ada_b)[:, :, None, :]
    zeros_like_b = lambda a: jnp.zeros((bp,) + a.shape[1:], a.dtype)
    out_p = _trunk(x_prompt, mods[:, :bp], zeros_like_b(state_gdn), zeros_like_b(state_gdn_conv),
                   zeros_like_b(state_ffn_conv), jnp.zeros((bp, 0, MLA_KV_LORA), cache_ckv.dtype),
                   jnp.zeros((bp, 0, MLA_ROPE), cache_kpe.dtype), pk)
    out_s = _trunk(x_sample, mods[:, bp:], state_gdn, state_gdn_conv, state_ffn_conv, cache_ckv, cache_kpe, pk)
    return (out_p[0], out_s[0]) + out_p[1:] + out_s[1:]
```

```python
import functools

import jax
import jax.numpy as jnp
from jax import lax
from jax.experimental import pallas as pl
from jax.experimental.pallas import tpu as pltpu

F32 = jnp.float32
BF16 = jnp.bfloat16

D_MODEL = 1024
DEPTH = 2
CHUNK = 64
ALPHA = (2.0 * DEPTH) ** 0.25
LN_EPS = 1e-5
RMS_EPS = 1e-6
GDN_HEADS = 8
GDN_DK = 128
GDN_DV = 128
GDN_CONV = 4
GDN_QK = GDN_HEADS * GDN_DK
GDN_V = GDN_HEADS * GDN_DV
GDN_CONV_CH = 2 * GDN_QK + GDN_V
MLA_HEADS = 8
MLA_NOPE = 128
MLA_ROPE = 64
MLA_V = 128
MLA_KV_LORA = 256
MLA_Q_LORA = 384
ROPE_THETA = 10000.0
MLA_SCALE = (MLA_NOPE + MLA_ROPE) ** -0.5
Q_PRESCALE = MLA_SCALE * 1.4426950408889634
D_FF = 2816
FFN_CONV = 3

LANES = 128
SUBLANES = 8
MXU_DIM = 256
HEAD_PAD = 256
VMEM_LIMIT = 56 * 1024 * 1024
NEG_BIG = -1e30

NN = ((1,), (0,))
NT = ((1,), (1,))
TN = ((0,), (0,))


def _row_tile(n, cap):
    if n <= cap:
        return n
    for t in range(cap, 15, -1):
        if n % t == 0 and t % 16 == 0:
            return t
    raise ValueError(f"no row tile for {n}")


def _split3(a):
    hi = a.astype(BF16)
    r = a - hi.astype(F32)
    mid = r.astype(BF16)
    lo = (r - mid.astype(F32)).astype(BF16)
    return hi, mid, lo


def _dg(a, b, dims=NN):
    return lax.dot_general(a, b, (dims, ((), ())), preferred_element_type=F32)


def _dot_hp(a, b, dims=NN):
    a0, a1, a2 = _split3(a)
    b0, b1, b2 = _split3(b)
    out = _dg(a0, b0, dims)
    out = out + (_dg(a0, b1, dims) + _dg(a1, b0, dims))
    out = out + (_dg(a0, b2, dims) + _dg(a1, b1, dims) + _dg(a2, b0, dims))
    return out


def _sigmoid(x):
    return 1.0 / (1.0 + jnp.exp(-x))


def _silu(x):
    return x * _sigmoid(x)


def _layer_norm(y, g, b):
    mu = jnp.mean(y, axis=-1, keepdims=True)
    yc = y - mu
    var = jnp.mean(yc * yc, axis=-1, keepdims=True)
    return yc * lax.rsqrt(var + LN_EPS) * g + b


def _const_spec(shape):
    nd = len(shape)
    return pl.BlockSpec(shape, lambda *_: (0,) * nd, pipeline_mode=pl.Buffered(1))


def _shift_rows(u, carry8, j):
    rows, width = u.shape
    groups = rows // SUBLANES
    rot = pltpu.roll(u.reshape(groups, SUBLANES, width), j, axis=1)
    above = jnp.concatenate([pltpu.roll(carry8, j, axis=0)[None], rot[:groups - 1]], axis=0)
    sub = lax.broadcasted_iota(jnp.int32, rot.shape, 1)
    return jnp.where(sub < j, above, rot).reshape(rows, width)


def _params(sem):
    return pltpu.CompilerParams(dimension_semantics=sem, vmem_limit_bytes=VMEM_LIMIT)


def _ada_kernel(c_ref, w_ref, b_ref, o_ref):
    s = _silu(c_ref[...])
    o_ref[0] = _dot_hp(s, w_ref[0]) + b_ref[0]


def _ada_terms(c_all, ada_w, ada_b):
    n_sub, d, n3 = ada_w.shape
    bc = c_all.shape[0]
    tn = 768
    return pl.pallas_call(
        _ada_kernel,
        grid=(n_sub, n3 // tn),
        in_specs=[
            pl.BlockSpec((bc, d), lambda i, j: (0, 0)),
            pl.BlockSpec((1, d, tn), lambda i, j: (i, 0, j)),
            pl.BlockSpec((1, 1, tn), lambda i, j: (i, 0, j)),
        ],
        out_specs=pl.BlockSpec((1, bc, tn), lambda i, j: (i, 0, j)),
        out_shape=jax.ShapeDtypeStruct((n_sub, bc, n3), F32),
        compiler_params=_params(("arbitrary", "arbitrary")),
        name="ada_terms",
    )(c_all, ada_w, ada_b.reshape(n_sub, 1, n3))


GDN_IN_TILE = 256


def _gdn_in_kernel(x_ref, mod_ref, w_ref, wba_ref, hist_ref, cw_ref, alog_ref, dtb_ref, tri_ref,
                   q_ref, k_ref, v_ref, z_ref, gb_ref, hist_out_ref, carry, *, tm):
    t = pl.program_id(1)

    @pl.when(t == 0)
    def _():
        carry[...] = hist_ref[0]

    shift = mod_ref[0, :, 0:D_MODEL]
    scale = mod_ref[0, :, D_MODEL:2 * D_MODEL]
    hin = (x_ref[0] * (1.0 + scale) + shift).astype(BF16)

    outs = (q_ref, k_ref, v_ref)
    for sec in range(3):
        for c in range(GDN_QK // GDN_IN_TILE):
            lo = sec * GDN_QK + c * GDN_IN_TILE
            cols = slice(lo, lo + GDN_IN_TILE)
            ocols = slice(c * GDN_IN_TILE, (c + 1) * GDN_IN_TILE)
            u = jnp.dot(hin, w_ref[:, cols], preferred_element_type=F32)
            c8 = carry[:, cols]
            acc = u * cw_ref[GDN_CONV - 1:GDN_CONV, cols]
            for j in range(1, GDN_CONV):
                acc = acc + _shift_rows(u, c8, j) * cw_ref[GDN_CONV - 1 - j:GDN_CONV - j, cols]
            carry[:, cols] = u[tm - SUBLANES:tm]
            s = _silu(acc)
            if sec == 2:
                v_ref[0, :, ocols] = s.astype(BF16)
            else:
                post = GDN_DK ** -0.5 if sec == 0 else 1.0
                parts = []
                for h in range(GDN_IN_TILE // GDN_DK):
                    sh = s[:, h * GDN_DK:(h + 1) * GDN_DK]
                    inv = lax.rsqrt(jnp.sum(sh * sh, axis=-1, keepdims=True) + RMS_EPS)
                    parts.append(sh * (inv * post))
                outs[sec][0, :, ocols] = jnp.concatenate(parts, axis=-1).astype(BF16)

    z_ref[0] = jnp.dot(hin, w_ref[:, GDN_CONV_CH:GDN_CONV_CH + GDN_V], preferred_element_type=F32).astype(BF16)

    ba = jnp.dot(hin, wba_ref[...], preferred_element_type=F32)
    beta = _sigmoid(ba)
    xs = ba + dtb_ref[...]
    softplus = jnp.maximum(xs, 0.0) + jnp.log(1.0 + jnp.exp(-jnp.abs(xs)))
    g = -jnp.exp(alog_ref[...]) * softplus
    tri = tri_ref[...]
    g0, g1, g2 = _split3(g)
    gsum = (jnp.dot(tri, g0, preferred_element_type=F32)
            + jnp.dot(tri, g1, preferred_element_type=F32)
            + jnp.dot(tri, g2, preferred_element_type=F32))
    lane = lax.broadcasted_iota(jnp.int32, (tm, LANES), 1)
    gb_ref[0] = jnp.where(lane < GDN_HEADS, beta, gsum)
    hist_out_ref[0] = carry[...]


def _gdn_in(x, mod, w_main, w_ba, hist8, conv_w, alog_row, dtb_row, c_len):
    b, l, d = x.shape
    tm = _row_tile(l, 512)
    assert tm % c_len == 0
    idx = jnp.arange(tm, dtype=jnp.int32)
    tri = ((idx[:, None] >= idx[None, :]) & ((idx[:, None] // c_len) == (idx[None, :] // c_len))).astype(BF16)
    tok = lambda w: pl.BlockSpec((1, tm, w), lambda i, t: (i, t, 0))
    return pl.pallas_call(
        functools.partial(_gdn_in_kernel, tm=tm),
        grid=(b, l // tm),
        in_specs=[
            tok(d),
            pl.BlockSpec((1, 1, 3 * d), lambda i, t: (i, 0, 0)),
            _const_spec(w_main.shape),
            _const_spec(w_ba.shape),
            pl.BlockSpec((1, SUBLANES, GDN_CONV_CH), lambda i, t: (i, 0, 0)),
            _const_spec(conv_w.shape),
            _const_spec(alog_row.shape),
            _const_spec(dtb_row.shape),
            _const_spec(tri.shape),
        ],
        out_specs=[tok(GDN_QK), tok(GDN_QK), tok(GDN_V), tok(GDN_V), tok(LANES),
                   pl.BlockSpec((1, SUBLANES, GDN_CONV_CH), lambda i, t: (i, 0, 0))],
        out_shape=[
            jax.ShapeDtypeStruct((b, l, GDN_QK), BF16),
            jax.ShapeDtypeStruct((b, l, GDN_QK), BF16),
            jax.ShapeDtypeStruct((b, l, GDN_V), BF16),
            jax.ShapeDtypeStruct((b, l, GDN_V), BF16),
            jax.ShapeDtypeStruct((b, l, LANES), F32),
            jax.ShapeDtypeStruct((b, SUBLANES, GDN_CONV_CH), F32),
        ],
        scratch_shapes=[pltpu.VMEM((SUBLANES, GDN_CONV_CH), F32)],
        compiler_params=_params(("arbitrary", "arbitrary")),
        name="gdn_in",
    )(x, mod, w_main, w_ba, hist8, conv_w, alog_row, dtb_row, tri)


GDN_GROUP = 4


def _bf(xs):
    return [x.astype(BF16) for x in xs]


def _gdn_chunk_kernel(q_ref, k_ref, v_ref, col_ref, row_ref, s0_ref, o_ref, s_ref, *, c_len, group):
    @pl.when(pl.program_id(1) == 0)
    def _():
        s_ref[...] = s0_ref[...]

    ri = lax.broadcasted_iota(jnp.int32, (c_len, c_len), 0)
    ci = lax.broadcasted_iota(jnp.int32, (c_len, c_len), 1)
    incl = ri >= ci
    strict = ri > ci
    n_sq = c_len.bit_length() - 2
    heads = range(GDN_HEADS)
    items = [(g, h) for g in range(group) for h in heads]
    each = range(len(items))
    rows = lambda g: slice(g * c_len, (g + 1) * c_len)
    lanes = lambda h: slice(h * GDN_DK, (h + 1) * GDN_DK)

    col = [col_ref[0, g] for g in range(group)]
    row = [row_ref[0, g] for g in range(group)]
    qb = [q_ref[0, rows(g), lanes(h)] for g, h in items]
    kb = [k_ref[0, rows(g), lanes(h)] for g, h in items]
    k = [x.astype(F32) for x in kb]
    v = [v_ref[0, rows(g), lanes(h)].astype(F32) for g, h in items]
    beta = [col[g][:, h:h + 1] for g, h in items]
    gc = [col[g][:, GDN_HEADS + h:GDN_HEADS + h + 1] for g, h in items]
    gr = [row[g][GDN_HEADS + h:GDN_HEADS + h + 1, :] for g, h in items]
    gam = [jnp.where(incl, jnp.exp(jnp.where(incl, gc[i] - gr[i], 0.0)), 0.0) for i in each]
    eg = [jnp.exp(gc[i]) for i in each]
    qkk = [_dg(jnp.concatenate([qb[i], kb[i]], axis=0), kb[i], NT) for i in each]
    p = _bf([jnp.where(strict, -(beta[i] * qkk[i][c_len:] * gam[i]), 0.0) for i in each])
    sol = [jnp.concatenate([beta[i] * v[i], beta[i] * k[i] * eg[i]], axis=-1) for i in each]
    solb = _bf(sol)
    sol = [sol[i] + _dg(p[i], solb[i]) for i in each]
    for _ in range(n_sq):
        p = _bf([_dg(p[i], p[i]) for i in each])
        solb = _bf(sol)
        sol = [sol[i] + _dg(p[i], solb[i]) for i in each]
    u = [sol[i][:, :GDN_DV] for i in each]
    wq = [jnp.concatenate([sol[i][:, GDN_DV:].astype(BF16), (qb[i].astype(F32) * eg[i]).astype(BF16)], axis=0)
          for i in each]
    att = _bf([qkk[i][:c_len] * gam[i] for i in each])
    g_last = [gc[i][c_len - 1:c_len, :] for i in each]
    k_dec = _bf([k[i] * jnp.exp(g_last[i] - gc[i]) for i in each])
    decay = [jnp.exp(g_last[i]) for i in each]

    s_cur = [s_ref[0, h] for h in heads]
    for g in range(group):
        it = [g * GDN_HEADS + h for h in heads]
        sb = _bf(s_cur)
        ws = [_dg(wq[it[h]], sb[h]) for h in heads]
        vb = _bf([u[it[h]] - ws[h][:c_len] for h in heads])
        o = [ws[h][c_len:] + _dg(att[it[h]], vb[h]) for h in heads]
        s_cur = [decay[it[h]] * s_cur[h] + _dg(k_dec[it[h]], vb[h], TN) for h in heads]
        o_ref[0, rows(g), :] = jnp.concatenate(o, axis=-1).astype(BF16)
    for h in heads:
        s_ref[0, h] = s_cur[h]


def _gdn_chunk(q, k, v, col, row, s0, c_len):
    b, l, _ = q.shape
    n = l // c_len
    group = GDN_GROUP if n % GDN_GROUP == 0 else 1
    rows = c_len * group
    tok = pl.BlockSpec((1, rows, GDN_QK), lambda i, t: (i, t, 0))
    st = pl.BlockSpec((1, GDN_HEADS, GDN_DK, GDN_DV), lambda i, t: (i, 0, 0, 0))
    return pl.pallas_call(
        functools.partial(_gdn_chunk_kernel, c_len=c_len, group=group),
        grid=(b, n // group),
        in_specs=[tok, tok, tok,
                  pl.BlockSpec((1, group, c_len, 2 * GDN_HEADS), lambda i, t: (i, t, 0, 0)),
                  pl.BlockSpec((1, group, 2 * GDN_HEADS, c_len), lambda i, t: (i, t, 0, 0)),
                  st],
        out_specs=[tok, st],
        out_shape=[jax.ShapeDtypeStruct((b, l, GDN_V), BF16),
                   jax.ShapeDtypeStruct(s0.shape, F32)],
        compiler_params=_params(("arbitrary", "arbitrary")),
        name="gdn_chunk",
    )(q, k, v, col, row, s0)


def _mix_out_kernel(*refs, gated):
    if gated:
        o_ref, z_ref, ng_ref, x_ref, mod_ref, w_ref, lg_ref, lb_ref, y_ref, hbuf = refs
        for h in range(GDN_HEADS):
            hs = slice(h * GDN_DV, (h + 1) * GDN_DV)
            oh = o_ref[0, :, hs].astype(F32)
            r = oh * lax.rsqrt(jnp.mean(oh * oh, axis=-1, keepdims=True) + RMS_EPS) * ng_ref[...]
            hbuf[:, hs] = (r * _silu(z_ref[0, :, hs].astype(F32))).astype(BF16)
        a = hbuf[...]
    else:
        o_ref, x_ref, mod_ref, w_ref, lg_ref, lb_ref, y_ref = refs
        a = o_ref[0]
    mix = jnp.dot(a, w_ref[...], preferred_element_type=F32)
    gate = mod_ref[0, :, 2 * D_MODEL:3 * D_MODEL]
    y = ALPHA * x_ref[0] + (1.0 + gate) * mix
    y_ref[0] = _layer_norm(y, lg_ref[...], lb_ref[...])


def _mix_out(o, z, norm_g, x, mod, w, ln_g, ln_b):
    b, l, d = x.shape
    tm = _row_tile(l, 512)
    gated = z is not None
    tok = pl.BlockSpec((1, tm, d), lambda i, t: (i, t, 0))
    modspec = pl.BlockSpec((1, 1, 3 * d), lambda i, t: (i, 0, 0))
    row = _const_spec((1, d))
    if gated:
        args = (o, z, norm_g, x, mod, w, ln_g, ln_b)
        in_specs = [tok, tok, _const_spec(norm_g.shape), tok, modspec, _const_spec(w.shape), row, row]
        scratch = [pltpu.VMEM((tm, d), BF16)]
    else:
        args = (o, x, mod, w, ln_g, ln_b)
        in_specs = [tok, tok, modspec, _const_spec(w.shape), row, row]
        scratch = []
    return pl.pallas_call(
        functools.partial(_mix_out_kernel, gated=gated),
        grid=(b, l // tm),
        in_specs=in_specs,
        out_specs=tok,
        out_shape=jax.ShapeDtypeStruct((b, l, d), F32),
        scratch_shapes=scratch,
        compiler_params=_params(("arbitrary", "arbitrary")),
        name="mix_out_gated" if gated else "mix_out",
    )(*args)


FFN_TILE = MXU_DIM
FFN_ROWS = 512


def _ffn_kernel(x_ref, mod_ref, win_ref, bin_ref, cw_ref, cb_ref, hist_ref, wdn_ref, lg_ref, lb_ref,
                y_ref, hist_out_ref, carry, hbuf, fbuf, *, tm):
    t = pl.program_id(1)

    @pl.when(t == 0)
    def _():
        carry[...] = hist_ref[0]

    shift = mod_ref[0, :, 0:D_MODEL]
    scale = mod_ref[0, :, D_MODEL:2 * D_MODEL]
    gate = mod_ref[0, :, 2 * D_MODEL:3 * D_MODEL]
    hbuf[...] = (x_ref[0] * (1.0 + scale) + shift).astype(BF16)

    def conv_half(lo):
        cols = slice(lo, lo + FFN_TILE)
        u = jnp.dot(hbuf[...], win_ref[:, cols], preferred_element_type=F32) + bin_ref[:, cols]
        c8 = carry[:, cols]
        out = u * cw_ref[FFN_CONV - 1:FFN_CONV, cols] + cb_ref[:, cols]
        for j in range(1, FFN_CONV):
            out = out + _shift_rows(u, c8, j) * cw_ref[FFN_CONV - 1 - j:FFN_CONV - j, cols]
        carry[:, cols] = u[tm - SUBLANES:tm]
        return out

    for i in range(D_FF // FFN_TILE):
        lo = i * FFN_TILE
        ua = conv_half(lo)
        ub = conv_half(lo + D_FF)
        fbuf[:, lo:lo + FFN_TILE] = (_silu(ua) * ub).astype(BF16)

    hist_out_ref[0] = carry[...]
    f = jnp.dot(fbuf[...], wdn_ref[...], preferred_element_type=F32)
    y = ALPHA * x_ref[0] + (1.0 + gate) * f
    y_ref[0] = _layer_norm(y, lg_ref[...], lb_ref[...])


def _ffn(x, mod, w_in, b_in, conv_w, conv_b, hist8, w_down, ln_g, ln_b):
    b, l, d = x.shape
    tm = _row_tile(l, FFN_ROWS)
    tok = pl.BlockSpec((1, tm, d), lambda i, t: (i, t, 0))
    hspec = pl.BlockSpec((1, SUBLANES, 2 * D_FF), lambda i, t: (i, 0, 0))
    return pl.pallas_call(
        functools.partial(_ffn_kernel, tm=tm),
        grid=(b, l // tm),
        in_specs=[tok, pl.BlockSpec((1, 1, 3 * d), lambda i, t: (i, 0, 0)),
                  _const_spec(w_in.shape), _const_spec(b_in.shape), _const_spec(conv_w.shape),
                  _const_spec(conv_b.shape), hspec, _const_spec(w_down.shape),
                  _const_spec((1, d)), _const_spec((1, d))],
        out_specs=[tok, hspec],
        out_shape=[jax.ShapeDtypeStruct((b, l, d), F32),
                   jax.ShapeDtypeStruct((b, SUBLANES, 2 * D_FF), F32)],
        scratch_shapes=[pltpu.VMEM((SUBLANES, 2 * D_FF), F32), pltpu.VMEM((tm, d), BF16),
                        pltpu.VMEM((tm, D_FF), BF16)],
        compiler_params=_params(("arbitrary", "arbitrary")),
        name="conv_ffn",
    )(x, mod, w_in, b_in, conv_w, conv_b, hist8, w_down, ln_g, ln_b)


def _rope_pair(pair, cs):
    prod = pair * cs
    return prod + pltpu.roll(prod, MLA_ROPE, axis=1)


def _mla_proj_kernel(x_ref, mod_ref, wkv_ref, kvg_ref, cs_ref, wdq_ref, qg_ref, wuq_ref,
                     ckv_ref, kpe_ref, q_ref):
    x = x_ref[0]
    cs = cs_ref[...]
    kv = jnp.dot(x.astype(BF16), wkv_ref[...], preferred_element_type=F32)
    lat = kv[:, :MLA_KV_LORA]
    ckv_ref[0] = lat * lax.rsqrt(jnp.mean(lat * lat, axis=-1, keepdims=True) + RMS_EPS) * kvg_ref[...]
    kpe_ref[0] = _rope_pair(kv[:, MLA_KV_LORA:], cs)[:, :MLA_ROPE]

    shift = mod_ref[0, :, 0:D_MODEL]
    scale = mod_ref[0, :, D_MODEL:2 * D_MODEL]
    hin = (x * (1.0 + scale) + shift).astype(BF16)
    qd = jnp.dot(hin, wdq_ref[...], preferred_element_type=F32)
    qd = qd * lax.rsqrt(jnp.mean(qd * qd, axis=-1, keepdims=True) + RMS_EPS) * qg_ref[...]
    qd = qd.astype(BF16)
    lane = lax.broadcasted_iota(jnp.int32, (x.shape[0], LANES), 1)
    for h in range(MLA_HEADS):
        qh = jnp.dot(qd, wuq_ref[:, h * HEAD_PAD:(h + 1) * HEAD_PAD], preferred_element_type=F32)
        q_ref[0, :, h * HEAD_PAD:h * HEAD_PAD + MLA_NOPE] = (qh[:, :MLA_NOPE] * Q_PRESCALE).astype(BF16)
        pe = jnp.where(lane < MLA_ROPE, _rope_pair(qh[:, MLA_NOPE:], cs) * Q_PRESCALE, 0.0)
        q_ref[0, :, h * HEAD_PAD + MLA_NOPE:(h + 1) * HEAD_PAD] = pe.astype(BF16)


def _mla_proj(x, mod, w_kv, kv_g, cossin, w_dq, q_g, w_uq):
    b, l, d = x.shape
    tm = _row_tile(l, 256)
    tok = lambda w: pl.BlockSpec((1, tm, w), lambda i, t: (i, t, 0))
    return pl.pallas_call(
        _mla_proj_kernel,
        grid=(b, l // tm),
        in_specs=[tok(d), pl.BlockSpec((1, 1, 3 * d), lambda i, t: (i, 0, 0)),
                  _const_spec(w_kv.shape), _const_spec(kv_g.shape),
                  pl.BlockSpec((tm, LANES), lambda i, t: (t, 0)),
                  _const_spec(w_dq.shape), _const_spec(q_g.shape), _const_spec(w_uq.shape)],
        out_specs=[tok(MLA_KV_LORA), tok(MLA_ROPE), tok(MLA_HEADS * HEAD_PAD)],
        out_shape=[jax.ShapeDtypeStruct((b, l, MLA_KV_LORA), F32),
                   jax.ShapeDtypeStruct((b, l, MLA_ROPE), F32),
                   jax.ShapeDtypeStruct((b, l, MLA_HEADS * HEAD_PAD), BF16)],
        compiler_params=_params(("arbitrary", "arbitrary")),
        name="mla_proj",
    )(x, mod, w_kv, kv_g, cossin, w_dq, q_g, w_uq)


def _kv_up_kernel(ckv_ref, kpe_ref, wk_ref, wvt_ref, k_ref, vt_ref):
    lat = ckv_ref[0].astype(BF16)
    kpe = kpe_ref[0].astype(BF16)
    for h in range(MLA_HEADS):
        kn = jnp.dot(lat, wk_ref[:, h * MLA_NOPE:(h + 1) * MLA_NOPE], preferred_element_type=F32)
        k_ref[0, :, h * HEAD_PAD:h * HEAD_PAD + MLA_NOPE] = kn.astype(BF16)
        k_ref[0, :, h * HEAD_PAD + MLA_NOPE:(h + 1) * HEAD_PAD] = kpe
        vt_ref[0, h * MLA_V:(h + 1) * MLA_V, :] = _dg(wvt_ref[h], lat, NT).astype(BF16)


def _kv_up(ckv_all, kpe_pad, w_k, w_vt):
    b, lk, _ = ckv_all.shape
    tm = ATTN_TK if lk % ATTN_TK == 0 else lk
    tok = lambda w: pl.BlockSpec((1, tm, w), lambda i, t: (i, t, 0))
    return pl.pallas_call(
        _kv_up_kernel,
        grid=(b, lk // tm),
        in_specs=[tok(MLA_KV_LORA), tok(LANES), _const_spec(w_k.shape), _const_spec(w_vt.shape)],
        out_specs=[tok(MLA_HEADS * HEAD_PAD),
                   pl.BlockSpec((1, MLA_HEADS * MLA_V, tm), lambda i, t: (i, 0, t))],
        out_shape=[jax.ShapeDtypeStruct((b, lk, MLA_HEADS * HEAD_PAD), BF16),
                   jax.ShapeDtypeStruct((b, MLA_HEADS * MLA_V, lk), BF16)],
        compiler_params=_params(("arbitrary", "arbitrary")),
        name="kv_up",
    )(ckv_all, kpe_pad, w_k, w_vt)


ATTN_TQ = 512
ATTN_TK = 512
ATTN_HEADS_PER_STEP = 4
CHUNK_SHIFT = CHUNK.bit_length() - 1


def _chunk_end(pos):
    return (lax.shift_right_logical(pos, CHUNK_SHIFT) + 1) * CHUNK


def _attn_kernel(q_ref, k_ref, vt_ref, o_ref, acc_ref, m_ref, l_ref, sa_ref, sb_ref, *, tq, tk, past, lk, nh):
    i = pl.program_id(2)
    acc_ref[...] = jnp.zeros_like(acc_ref)
    m_ref[...] = jnp.full_like(m_ref, NEG_BIG)
    l_ref[...] = jnp.zeros_like(l_ref)
    q0 = past + i * tq
    n_full = lax.div(jnp.minimum(_chunk_end(q0), lk), tk)
    n_end = lax.div(jnp.minimum(_chunk_end(q0 + tq - 1), lk) + tk - 1, tk)
    n_blk = lk // tk
    heads = range(nh)

    def key_start(j):
        return pl.multiple_of(jnp.minimum(j, n_blk - 1) * tk, tk) if lk > tk else 0

    def scores(j, dst):
        k0 = key_start(j)
        for h in heads:
            dst[h] = _dg(k_ref[0, pl.ds(k0, tk), h * HEAD_PAD:(h + 1) * HEAD_PAD],
                         q_ref[0, :, h * HEAD_PAD:(h + 1) * HEAD_PAD], NT)

    def step(j, src, dst, masked):
        if dst is not None:
            scores(j + 1, dst)
        k0 = key_start(j)
        st = [src[h] for h in heads]
        if masked:
            kc = lax.shift_right_logical(k0 + lax.broadcasted_iota(jnp.int32, (tk, tq), 0), CHUNK_SHIFT)
            qc = lax.shift_right_logical(q0 + lax.broadcasted_iota(jnp.int32, (tk, tq), 1), CHUNK_SHIFT)
            visible = kc <= qc
            st = [jnp.where(visible, s, NEG_BIG) for s in st]
        m_old = [m_ref[h] for h in heads]
        m_new = [jnp.maximum(m_old[h], jnp.max(st[h], axis=0, keepdims=True)) for h in heads]
        alpha = [jnp.exp2(m_old[h] - m_new[h]) for h in heads]
        pt = [jnp.exp2(st[h] - m_new[h]) for h in heads]
        l_new = [alpha[h] * l_ref[h] + jnp.sum(pt[h], axis=0, keepdims=True) for h in heads]
        pv = [_dg(vt_ref[0, h * MLA_V:(h + 1) * MLA_V, pl.ds(k0, tk)], pt[h].astype(BF16)) for h in heads]
        for h in heads:
            acc_ref[h] = alpha[h] * acc_ref[h] + pv[h]
            m_ref[h] = m_new[h]
            l_ref[h] = l_new[h]

    odd = lax.rem(n_full, 2)

    @pl.when(odd == 1)
    def _():
        scores(0, sb_ref)
        step(0, sb_ref, sa_ref, False)

    @pl.when(odd == 0)
    def _():
        scores(0, sa_ref)

    def pair(p, c):
        j = odd + 2 * p
        step(j, sa_ref, sb_ref, False)
        step(j + 1, sb_ref, sa_ref, False)
        return c

    lax.fori_loop(0, lax.div(n_full, 2), pair, 0)

    @pl.when(n_end > n_full)
    def _():
        step(n_full, sa_ref, None, True)

    def rest(j, c):
        scores(j, sa_ref)
        step(j, sa_ref, None, True)
        return c

    lax.fori_loop(n_full + 1, n_end, rest, 0)
    for h in heads:
        o_ref[0, :, h * MLA_V:(h + 1) * MLA_V] = (acc_ref[h] / l_ref[h]).T.astype(BF16)


def _attention(q, k, vt, past):
    b, lq, _ = q.shape
    lk = k.shape[1]
    lq_pad = -(-lq // LANES) * LANES
    if lq_pad != lq:
        q = jnp.pad(q, ((0, 0), (0, lq_pad - lq), (0, 0)))
    tq = _row_tile(lq_pad, ATTN_TQ)
    tk = ATTN_TK if lk % ATTN_TK == 0 else lk
    nh = ATTN_HEADS_PER_STEP
    out = pl.pallas_call(
        functools.partial(_attn_kernel, tq=tq, tk=tk, past=past, lk=lk, nh=nh),
        grid=(b, MLA_HEADS // nh, lq_pad // tq),
        in_specs=[pl.BlockSpec((1, tq, nh * HEAD_PAD), lambda bi, h, i: (bi, i, h)),
                  pl.BlockSpec((1, lk, nh * HEAD_PAD), lambda bi, h, i: (bi, 0, h), pipeline_mode=pl.Buffered(1)),
                  pl.BlockSpec((1, nh * MLA_V, lk), lambda bi, h, i: (bi, h, 0), pipeline_mode=pl.Buffered(1))],
        out_specs=pl.BlockSpec((1, tq, nh * MLA_V), lambda bi, h, i: (bi, i, h)),
        out_shape=jax.ShapeDtypeStruct((b, lq_pad, MLA_HEADS * MLA_V), BF16),
        scratch_shapes=[pltpu.VMEM((nh, MLA_V, tq), F32), pltpu.VMEM((nh, 1, tq), F32),
                        pltpu.VMEM((nh, 1, tq), F32), pltpu.VMEM((nh, tk, tq), F32),
                        pltpu.VMEM((nh, tk, tq), F32)],
        compiler_params=_params(("arbitrary", "arbitrary", "arbitrary")),
        name="chunk_causal_attention",
    )(q, k, vt)
    return out[:, :lq]


def _pad_hist(hist):
    return jnp.pad(hist, ((0, 0), (SUBLANES - hist.shape[1], 0), (0, 0)))


def _rot_half_cols(w):
    w1, w2 = jnp.split(w, 2, axis=-1)
    return jnp.concatenate([-w2, w1], axis=-1)


def _pack_weights(p):
    d = D_MODEL
    pk = {}
    w_in = p['gdn_w_in'][0]
    pk['gdn_w_main'] = w_in[:, :GDN_CONV_CH + GDN_V].astype(BF16)
    pk['gdn_w_ba'] = jnp.pad(w_in[:, GDN_CONV_CH + GDN_V:], ((0, 0), (0, LANES - 2 * GDN_HEADS))).astype(BF16)
    pad_gate = lambda a: jnp.pad(a.reshape(1, GDN_HEADS), ((0, 0), (GDN_HEADS, LANES - 2 * GDN_HEADS)))
    pk['gdn_alog'] = pad_gate(p['gdn_a_log'][0])
    pk['gdn_dtb'] = pad_gate(p['gdn_dt_bias'][0])
    pk['gdn_conv_w'] = p['gdn_conv_w'][0]
    pk['gdn_norm_g'] = p['gdn_norm_g'][0].reshape(1, GDN_DV)
    pk['gdn_w_out'] = p['gdn_w_out'][0].astype(BF16)
    wkv = p['kv_w_down']
    pk['kv_w_down'] = jnp.concatenate([wkv, _rot_half_cols(wkv[:, MLA_KV_LORA:])], axis=-1).astype(BF16)
    pk['kv_norm_g'] = p['kv_norm_g'].reshape(1, MLA_KV_LORA)
    w_up = p['kv_w_up'].reshape(MLA_KV_LORA, MLA_HEADS, MLA_NOPE + MLA_V)
    pk['kv_w_k'] = w_up[..., :MLA_NOPE].reshape(MLA_KV_LORA, MLA_HEADS * MLA_NOPE).astype(BF16)
    pk['kv_w_vt'] = jnp.transpose(w_up[..., MLA_NOPE:], (1, 2, 0)).astype(BF16)
    pk['mla_w_dq'] = p['mla_w_dq'][0].astype(BF16)
    pk['mla_q_norm_g'] = p['mla_q_norm_g'][0].reshape(1, MLA_Q_LORA)
    wuq = p['mla_w_uq'][0].reshape(MLA_Q_LORA, MLA_HEADS, MLA_NOPE + MLA_ROPE)
    wuq = jnp.concatenate([wuq, _rot_half_cols(wuq[..., MLA_NOPE:])], axis=-1)
    pk['mla_w_uq'] = wuq.reshape(MLA_Q_LORA, MLA_HEADS * HEAD_PAD).astype(BF16)
    pk['mla_w_out'] = p['mla_w_out'][0].astype(BF16)
    pk['ffn_w_in'] = p['ffn_w_in'].astype(BF16)
    pk['ffn_b_in'] = p['ffn_b_in'].reshape(DEPTH, 1, 2 * D_FF)
    pk['ffn_conv_w'] = p['ffn_conv_w']
    pk['ffn_conv_b'] = p['ffn_conv_b'].reshape(DEPTH, 1, 2 * D_FF)
    pk['ffn_w_down'] = p['ffn_w_down'].astype(BF16)
    pk['ln_g'] = p['ln_g'].reshape(2 * DEPTH, 1, d)
    pk['ln_b'] = p['ln_b'].reshape(2 * DEPTH, 1, d)
    return pk


def _rope_table(past, l):
    inv = 1.0 / (ROPE_THETA ** (jnp.arange(0, MLA_ROPE, 2, dtype=F32) / MLA_ROPE))
    ang = (past + jnp.arange(l, dtype=jnp.int32)).astype(F32)[:, None] * inv[None, :]
    ang = jnp.concatenate([ang, ang], axis=-1)
    return jnp.concatenate([jnp.cos(ang), jnp.sin(ang)], axis=-1)


def _ffn_layer(x, mod, hist, pk, layer):
    y, hist8 = _ffn(x, mod, pk['ffn_w_in'][layer], pk['ffn_b_in'][layer], pk['ffn_conv_w'][layer],
                    pk['ffn_conv_b'][layer], _pad_hist(hist), pk['ffn_w_down'][layer],
                    pk['ln_g'][2 * layer + 1], pk['ln_b'][2 * layer + 1])
    return y, hist8[:, SUBLANES - (FFN_CONV - 1):]


def _trunk(x, mods, gdn_state, gdn_conv, ffn_conv, ckv_past, kpe_past, pk):
    b, l, _ = x.shape
    past = kpe_past.shape[1]
    c_len = CHUNK if l % CHUNK == 0 else l
    n = l // c_len

    q, k, v, z, gb, hist8 = _gdn_in(x, mods[0], pk['gdn_w_main'], pk['gdn_w_ba'], _pad_hist(gdn_conv[:, 0]),
                                    pk['gdn_conv_w'], pk['gdn_alog'], pk['gdn_dtb'], c_len)
    col = gb[..., :2 * GDN_HEADS].reshape(b, n, c_len, 2 * GDN_HEADS)
    row = jnp.swapaxes(col, 2, 3)
    o, s_end = _gdn_chunk(q, k, v, col, row, gdn_state[:, 0], c_len)
    x = _mix_out(o, z, pk['gdn_norm_g'], x, mods[0], pk['gdn_w_out'], pk['ln_g'][0], pk['ln_b'][0])
    gdn_conv_out = hist8[:, None, SUBLANES - (GDN_CONV - 1):]
    x, fh0 = _ffn_layer(x, mods[1], ffn_conv[:, 0], pk, 0)

    ckv_new, kpe_new, qm = _mla_proj(x, mods[2], pk['kv_w_down'], pk['kv_norm_g'], _rope_table(past, l),
                                     pk['mla_w_dq'], pk['mla_q_norm_g'], pk['mla_w_uq'])
    ckv_all = jnp.concatenate([ckv_past, ckv_new], axis=1)
    kpe_all = jnp.concatenate([kpe_past, kpe_new], axis=1)
    kpe_pad = jnp.pad(kpe_all, ((0, 0), (0, 0), (0, LANES - MLA_ROPE)))
    km, vm = _kv_up(ckv_all, kpe_pad, pk['kv_w_k'], pk['kv_w_vt'])
    o = _attention(qm, km, vm, past)
    x = _mix_out(o, None, None, x, mods[2], pk['mla_w_out'], pk['ln_g'][2], pk['ln_b'][2])
    x, fh1 = _ffn_layer(x, mods[3], ffn_conv[:, 1], pk, 1)

    return (x, s_end[:, None], gdn_conv_out, jnp.stack([fh0, fh1], axis=1), ckv_new, kpe_new)


def kernel(x_prompt, x_sample, c_prompt, c_sample, state_gdn, state_gdn_conv, state_ffn_conv, cache_ckv, cache_kpe, ada_w, ada_b, ln_g, ln_b, gdn_w_in, gdn_conv_w, gdn_a_log, gdn_dt_bias, gdn_norm_g, gdn_w_out, kv_w_down, kv_norm_g, kv_w_up, mla_w_dq, mla_q_norm_g, mla_w_uq, mla_w_out, ffn_w_in, ffn_b_in, ffn_conv_w, ffn_conv_b, ffn_w_down):
    p = {'ln_g': ln_g, 'ln_b': ln_b, 'gdn_w_in': gdn_w_in, 'gdn_conv_w': gdn_conv_w, 'gdn_a_log': gdn_a_log,
         'gdn_dt_bias': gdn_dt_bias, 'gdn_norm_g': gdn_norm_g, 'gdn_w_out': gdn_w_out,
         'kv_w_down': kv_w_down, 'kv_norm_g': kv_norm_g, 'kv_w_up': kv_w_up,
         'mla_w_dq': mla_w_dq, 'mla_q_norm_g': mla_q_norm_g, 'mla_w_uq': mla_w_uq, 'mla_w_out': mla_w_out,
         'ffn_w_in': ffn_w_in, 'ffn_b_in': ffn_b_in, 'ffn_conv_w': ffn_conv_w,
         'ffn_conv_b': ffn_conv_b, 'ffn_w_down': ffn_w_down}
    pk = _pack_weights(p)
    bp = x_prompt.shape[0]
    mods = _ada_terms(jnp.concatenate([c_prompt, c_sample], axis=0), ada_w, ada_b)[:, :, None, :]
    zeros_like_b = lambda a: jnp.zeros((bp,) + a.shape[1:], a.dtype)
    out_p = _trunk(x_prompt, mods[:, :bp], zeros_like_b(state_gdn), zeros_like_b(state_gdn_conv),
                   zeros_like_b(state_ffn_conv), jnp.zeros((bp, 0, MLA_KV_LORA), cache_ckv.dtype),
                   jnp.zeros((bp, 0, MLA_ROPE), cache_kpe.dtype), pk)
    out_s = _trunk(x_sample, mods[:, bp:], state_gdn, state_gdn_conv, state_ffn_conv, cache_ckv, cache_kpe, pk)
    return (out_p[0], out_s[0]) + out_p[1:] + out_s[1:]
```

```python
import functools

import jax
import jax.numpy as jnp
from jax import lax
from jax.experimental import pallas as pl
from jax.experimental.pallas import tpu as pltpu

F32 = jnp.float32
BF16 = jnp.bfloat16

D_MODEL = 1024
DEPTH = 2
CHUNK = 64
ALPHA = (2.0 * DEPTH) ** 0.25
LN_EPS = 1e-5
RMS_EPS = 1e-6
GDN_HEADS = 8
GDN_DK = 128
GDN_DV = 128
GDN_CONV = 4
GDN_QK = GDN_HEADS * GDN_DK
GDN_V = GDN_HEADS * GDN_DV
GDN_CONV_CH = 2 * GDN_QK + GDN_V
MLA_HEADS = 8
MLA_NOPE = 128
MLA_ROPE = 64
MLA_V = 128
MLA_KV_LORA = 256
MLA_Q_LORA = 384
ROPE_THETA = 10000.0
MLA_SCALE = (MLA_NOPE + MLA_ROPE) ** -0.5
Q_PRESCALE = MLA_SCALE * 1.4426950408889634
D_FF = 2816
FFN_CONV = 3

LANES = 128
SUBLANES = 8
MXU_DIM = 256
HEAD_PAD = 256
VMEM_LIMIT = 56 * 1024 * 1024
NEG_BIG = -1e30

NN = ((1,), (0,))
NT = ((1,), (1,))
TN = ((0,), (0,))


def _row_tile(n, cap):
    if n <= cap:
        return n
    for t in range(cap, 15, -1):
        if n % t == 0 and t % 16 == 0:
            return t
    raise ValueError(f"no row tile for {n}")


def _split3(a):
    hi = a.astype(BF16)
    r = a - hi.astype(F32)
    mid = r.astype(BF16)
    lo = (r - mid.astype(F32)).astype(BF16)
    return hi, mid, lo


def _dg(a, b, dims=NN):
    return lax.dot_general(a, b, (dims, ((), ())), preferred_element_type=F32)


def _dot_hp(a, b, dims=NN):
    a0, a1, a2 = _split3(a)
    b0, b1, b2 = _split3(b)
    out = _dg(a0, b0, dims)
    out = out + (_dg(a0, b1, dims) + _dg(a1, b0, dims))
    out = out + (_dg(a0, b2, dims) + _dg(a1, b1, dims) + _dg(a2, b0, dims))
    return out


def _sigmoid(x):
    return 1.0 / (1.0 + jnp.exp(-x))


def _silu(x):
    return x * _sigmoid(x)


def _layer_norm(y, g, b):
    mu = jnp.mean(y, axis=-1, keepdims=True)
    yc = y - mu
    var = jnp.mean(yc * yc, axis=-1, keepdims=True)
    return yc * lax.rsqrt(var + LN_EPS) * g + b


def _const_spec(shape):
    nd = len(shape)
    return pl.BlockSpec(shape, lambda *_: (0,) * nd, pipeline_mode=pl.Buffered(1))


def _shift_rows(u, carry8, j):
    rows, width = u.shape
    groups = rows // SUBLANES
    rot = pltpu.roll(u.reshape(groups, SUBLANES, width), j, axis=1)
    above = jnp.concatenate([pltpu.roll(carry8, j, axis=0)[None], rot[:groups - 1]], axis=0)
    sub = lax.broadcasted_iota(jnp.int32, rot.shape, 1)
    return jnp.where(sub < j, above, rot).reshape(rows, width)


def _params(sem):
    return pltpu.CompilerParams(dimension_semantics=sem, vmem_limit_bytes=VMEM_LIMIT)


def _ada_kernel(c_ref, w_ref, b_ref, o_ref):
    s = _silu(c_ref[...])
    o_ref[0] = _dot_hp(s, w_ref[0]) + b_ref[0]


def _ada_terms(c_all, ada_w, ada_b):
    n_sub, d, n3 = ada_w.shape
    bc = c_all.shape[0]
    tn = 768
    return pl.pallas_call(
        _ada_kernel,
        grid=(n_sub, n3 // tn),
        in_specs=[
            pl.BlockSpec((bc, d), lambda i, j: (0, 0)),
            pl.BlockSpec((1, d, tn), lambda i, j: (i, 0, j)),
            pl.BlockSpec((1, 1, tn), lambda i, j: (i, 0, j)),
        ],
        out_specs=pl.BlockSpec((1, bc, tn), lambda i, j: (i, 0, j)),
        out_shape=jax.ShapeDtypeStruct((n_sub, bc, n3), F32),
        compiler_params=_params(("arbitrary", "arbitrary")),
        name="ada_terms",
    )(c_all, ada_w, ada_b.reshape(n_sub, 1, n3))


GDN_IN_TILE = 256


def _gdn_in_kernel(x_ref, mod_ref, w_ref, wba_ref, hist_ref, cw_ref, alog_ref, dtb_ref, tri_ref,
                   q_ref, k_ref, v_ref, z_ref, gb_ref, hist_out_ref, carry, *, tm):
    t = pl.program_id(1)

    @pl.when(t == 0)
    def _():
        carry[...] = hist_ref[0]

    shift = mod_ref[0, :, 0:D_MODEL]
    scale = mod_ref[0, :, D_MODEL:2 * D_MODEL]
    hin = (x_ref[0] * (1.0 + scale) + shift).astype(BF16)

    outs = (q_ref, k_ref, v_ref)
    for sec in range(3):
        for c in range(GDN_QK // GDN_IN_TILE):
            lo = sec * GDN_QK + c * GDN_IN_TILE
            cols = slice(lo, lo + GDN_IN_TILE)
            ocols = slice(c * GDN_IN_TILE, (c + 1) * GDN_IN_TILE)
            u = jnp.dot(hin, w_ref[:, cols], preferred_element_type=F32)
            c8 = carry[:, cols]
            acc = u * cw_ref[GDN_CONV - 1:GDN_CONV, cols]
            for j in range(1, GDN_CONV):
                acc = acc + _shift_rows(u, c8, j) * cw_ref[GDN_CONV - 1 - j:GDN_CONV - j, cols]
            carry[:, cols] = u[tm - SUBLANES:tm]
            s = _silu(acc)
            if sec == 2:
                v_ref[0, :, ocols] = s.astype(BF16)
            else:
                post = GDN_DK ** -0.5 if sec == 0 else 1.0
                parts = []
                for h in range(GDN_IN_TILE // GDN_DK):
                    sh = s[:, h * GDN_DK:(h + 1) * GDN_DK]
                    inv = lax.rsqrt(jnp.sum(sh * sh, axis=-1, keepdims=True) + RMS_EPS)
                    parts.append(sh * (inv * post))
                outs[sec][0, :, ocols] = jnp.concatenate(parts, axis=-1).astype(BF16)

    z_ref[0] = jnp.dot(hin, w_ref[:, GDN_CONV_CH:GDN_CONV_CH + GDN_V], preferred_element_type=F32).astype(BF16)

    ba = jnp.dot(hin, wba_ref[...], preferred_element_type=F32)
    beta = _sigmoid(ba)
    xs = ba + dtb_ref[...]
    softplus = jnp.maximum(xs, 0.0) + jnp.log(1.0 + jnp.exp(-jnp.abs(xs)))
    g = -jnp.exp(alog_ref[...]) * softplus
    tri = tri_ref[...]
    g0, g1, g2 = _split3(g)
    gsum = (jnp.dot(tri, g0, preferred_element_type=F32)
            + jnp.dot(tri, g1, preferred_element_type=F32)
            + jnp.dot(tri, g2, preferred_element_type=F32))
    lane = lax.broadcasted_iota(jnp.int32, (tm, LANES), 1)
    gb_ref[0] = jnp.where(lane < GDN_HEADS, beta, gsum)[:, :2 * GDN_HEADS]
    hist_out_ref[0] = carry[...]


def _gdn_in(x, mod, w_main, w_ba, hist8, conv_w, alog_row, dtb_row, c_len):
    b, l, d = x.shape
    tm = _row_tile(l, 512)
    assert tm % c_len == 0
    idx = jnp.arange(tm, dtype=jnp.int32)
    tri = ((idx[:, None] >= idx[None, :]) & ((idx[:, None] // c_len) == (idx[None, :] // c_len))).astype(BF16)
    tok = lambda w: pl.BlockSpec((1, tm, w), lambda i, t: (i, t, 0))
    return pl.pallas_call(
        functools.partial(_gdn_in_kernel, tm=tm),
        grid=(b, l // tm),
        in_specs=[
            tok(d),
            pl.BlockSpec((1, 1, 3 * d), lambda i, t: (i, 0, 0)),
            _const_spec(w_main.shape),
            _const_spec(w_ba.shape),
            pl.BlockSpec((1, SUBLANES, GDN_CONV_CH), lambda i, t: (i, 0, 0)),
            _const_spec(conv_w.shape),
            _const_spec(alog_row.shape),
            _const_spec(dtb_row.shape),
            _const_spec(tri.shape),
        ],
        out_specs=[tok(GDN_QK), tok(GDN_QK), tok(GDN_V), tok(GDN_V), tok(2 * GDN_HEADS),
                   pl.BlockSpec((1, SUBLANES, GDN_CONV_CH), lambda i, t: (i, 0, 0))],
        out_shape=[
            jax.ShapeDtypeStruct((b, l, GDN_QK), BF16),
            jax.ShapeDtypeStruct((b, l, GDN_QK), BF16),
            jax.ShapeDtypeStruct((b, l, GDN_V), BF16),
            jax.ShapeDtypeStruct((b, l, GDN_V), BF16),
            jax.ShapeDtypeStruct((b, l, 2 * GDN_HEADS), F32),
            jax.ShapeDtypeStruct((b, SUBLANES, GDN_CONV_CH), F32),
        ],
        scratch_shapes=[pltpu.VMEM((SUBLANES, GDN_CONV_CH), F32)],
        compiler_params=_params(("arbitrary", "arbitrary")),
        name="gdn_in",
    )(x, mod, w_main, w_ba, hist8, conv_w, alog_row, dtb_row, tri)


GDN_GROUP = 4


def _bf(xs):
    return [x.astype(BF16) for x in xs]


def _gdn_chunk_kernel(q_ref, k_ref, v_ref, col_ref, row_ref, s0_ref, o_ref, s_ref, *, c_len, group):
    @pl.when(pl.program_id(1) == 0)
    def _():
        s_ref[...] = s0_ref[...]

    ri = lax.broadcasted_iota(jnp.int32, (c_len, c_len), 0)
    ci = lax.broadcasted_iota(jnp.int32, (c_len, c_len), 1)
    incl = ri >= ci
    strict = ri > ci
    n_sq = c_len.bit_length() - 2
    heads = range(GDN_HEADS)
    items = [(g, h) for g in range(group) for h in heads]
    each = range(len(items))
    rows = lambda g: slice(g * c_len, (g + 1) * c_len)
    lanes = lambda h: slice(h * GDN_DK, (h + 1) * GDN_DK)

    col = [col_ref[0, g] for g in range(group)]
    row = [row_ref[0, g] for g in range(group)]
    qb = [q_ref[0, rows(g), lanes(h)] for g, h in items]
    kb = [k_ref[0, rows(g), lanes(h)] for g, h in items]
    k = [x.astype(F32) for x in kb]
    v = [v_ref[0, rows(g), lanes(h)].astype(F32) for g, h in items]
    beta = [col[g][:, h:h + 1] for g, h in items]
    gc = [col[g][:, GDN_HEADS + h:GDN_HEADS + h + 1] for g, h in items]
    gr = [row[g][GDN_HEADS + h:GDN_HEADS + h + 1, :] for g, h in items]
    gam = [jnp.where(incl, jnp.exp(jnp.where(incl, gc[i] - gr[i], 0.0)), 0.0) for i in each]
    eg = [jnp.exp(gc[i]) for i in each]
    qkk = [_dg(jnp.concatenate([qb[i], kb[i]], axis=0), kb[i], NT) for i in each]
    p = _bf([jnp.where(strict, -(beta[i] * qkk[i][c_len:] * gam[i]), 0.0) for i in each])
    sol = [jnp.concatenate([beta[i] * v[i], beta[i] * k[i] * eg[i]], axis=-1) for i in each]
    solb = _bf(sol)
    sol = [sol[i] + _dg(p[i], solb[i]) for i in each]
    for _ in range(n_sq):
        p = _bf([_dg(p[i], p[i]) for i in each])
        solb = _bf(sol)
        sol = [sol[i] + _dg(p[i], solb[i]) for i in each]
    u = [sol[i][:, :GDN_DV] for i in each]
    wq = [jnp.concatenate([sol[i][:, GDN_DV:].astype(BF16), (qb[i].astype(F32) * eg[i]).astype(BF16)], axis=0)
          for i in each]
    att = _bf([qkk[i][:c_len] * gam[i] for i in each])
    g_last = [gc[i][c_len - 1:c_len, :] for i in each]
    k_dec = _bf([k[i] * jnp.exp(g_last[i] - gc[i]) for i in each])
    decay = [jnp.exp(g_last[i]) for i in each]

    s_cur = [s_ref[0, h] for h in heads]
    for g in range(group):
        it = [g * GDN_HEADS + h for h in heads]
        sb = _bf(s_cur)
        ws = [_dg(wq[it[h]], sb[h]) for h in heads]
        vb = _bf([u[it[h]] - ws[h][:c_len] for h in heads])
        o = [ws[h][c_len:] + _dg(att[it[h]], vb[h]) for h in heads]
        s_cur = [decay[it[h]] * s_cur[h] + _dg(k_dec[it[h]], vb[h], TN) for h in heads]
        o_ref[0, rows(g), :] = jnp.concatenate(o, axis=-1).astype(BF16)
    for h in heads:
        s_ref[0, h] = s_cur[h]


def _gdn_chunk(q, k, v, col, row, s0, c_len):
    b, l, _ = q.shape
    n = l // c_len
    group = GDN_GROUP if n % GDN_GROUP == 0 else 1
    rows = c_len * group
    tok = pl.BlockSpec((1, rows, GDN_QK), lambda i, t: (i, t, 0))
    st = pl.BlockSpec((1, GDN_HEADS, GDN_DK, GDN_DV), lambda i, t: (i, 0, 0, 0))
    return pl.pallas_call(
        functools.partial(_gdn_chunk_kernel, c_len=c_len, group=group),
        grid=(b, n // group),
        in_specs=[tok, tok, tok,
                  pl.BlockSpec((1, group, c_len, 2 * GDN_HEADS), lambda i, t: (i, t, 0, 0)),
                  pl.BlockSpec((1, group, 2 * GDN_HEADS, c_len), lambda i, t: (i, t, 0, 0)),
                  st],
        out_specs=[tok, st],
        out_shape=[jax.ShapeDtypeStruct((b, l, GDN_V), BF16),
                   jax.ShapeDtypeStruct(s0.shape, F32)],
        compiler_params=_params(("arbitrary", "arbitrary")),
        name="gdn_chunk",
    )(q, k, v, col, row, s0)


FFN_TILE = MXU_DIM
FFN_ROWS = 512


def _mix_ffn_kernel(*refs, gated, tm):
    if gated:
        o_ref, z_ref, ng_ref = refs[:3]
        refs = refs[3:]
    else:
        o_ref = refs[0]
        refs = refs[1:]
    (x_ref, mmod_ref, wo_ref, mlg_ref, mlb_ref, fmod_ref, win_ref, bin_ref, cw_ref, cb_ref, hist_ref, wdn_ref,
     lg_ref, lb_ref, y_ref, hist_out_ref, carry, x1_ref, hbuf, fbuf) = refs
    t = pl.program_id(1)

    @pl.when(t == 0)
    def _():
        carry[...] = hist_ref[0]

    if gated:
        for h in range(GDN_HEADS):
            hs = slice(h * GDN_DV, (h + 1) * GDN_DV)
            oh = o_ref[0, :, hs].astype(F32)
            r = oh * lax.rsqrt(jnp.mean(oh * oh, axis=-1, keepdims=True) + RMS_EPS) * ng_ref[...]
            hbuf[:, hs] = (r * _silu(z_ref[0, :, hs].astype(F32))).astype(BF16)
        a = hbuf[...]
    else:
        a = o_ref[0]
    mix = jnp.dot(a, wo_ref[...], preferred_element_type=F32)
    mgate = mmod_ref[0, :, 2 * D_MODEL:3 * D_MODEL]
    x1_ref[...] = _layer_norm(ALPHA * x_ref[0] + (1.0 + mgate) * mix, mlg_ref[...], mlb_ref[...])

    shift = fmod_ref[0, :, 0:D_MODEL]
    scale = fmod_ref[0, :, D_MODEL:2 * D_MODEL]
    gate = fmod_ref[0, :, 2 * D_MODEL:3 * D_MODEL]
    hbuf[...] = (x1_ref[...] * (1.0 + scale) + shift).astype(BF16)

    def conv_half(lo):
        cols = slice(lo, lo + FFN_TILE)
        u = jnp.dot(hbuf[...], win_ref[:, cols], preferred_element_type=F32) + bin_ref[:, cols]
        c8 = carry[:, cols]
        out = u * cw_ref[FFN_CONV - 1:FFN_CONV, cols] + cb_ref[:, cols]
        for j in range(1, FFN_CONV):
            out = out + _shift_rows(u, c8, j) * cw_ref[FFN_CONV - 1 - j:FFN_CONV - j, cols]
        carry[:, cols] = u[tm - SUBLANES:tm]
        return out

    for i in range(D_FF // FFN_TILE):
        lo = i * FFN_TILE
        ua = conv_half(lo)
        ub = conv_half(lo + D_FF)
        fbuf[:, lo:lo + FFN_TILE] = (_silu(ua) * ub).astype(BF16)

    hist_out_ref[0] = carry[...]
    f = jnp.dot(fbuf[...], wdn_ref[...], preferred_element_type=F32)
    y = ALPHA * x1_ref[...] + (1.0 + gate) * f
    y_ref[0] = _layer_norm(y, lg_ref[...], lb_ref[...])


def _mix_ffn(o, z, norm_g, x, mix_mod, w_out, mix_ln_g, mix_ln_b,
             ffn_mod, w_in, b_in, conv_w, conv_b, hist8, w_down, ln_g, ln_b):
    b, l, d = x.shape
    tm = _row_tile(l, FFN_ROWS)
    gated = z is not None
    tok = pl.BlockSpec((1, tm, d), lambda i, t: (i, t, 0))
    modspec = pl.BlockSpec((1, 1, 3 * d), lambda i, t: (i, 0, 0))
    hspec = pl.BlockSpec((1, SUBLANES, 2 * D_FF), lambda i, t: (i, 0, 0))
    row = _const_spec((1, d))
    head_args, head_specs = ((o, z, norm_g), [tok, tok, _const_spec(norm_g.shape)]) if gated else ((o,), [tok])
    return pl.pallas_call(
        functools.partial(_mix_ffn_kernel, gated=gated, tm=tm),
        grid=(b, l // tm),
        in_specs=head_specs + [tok, modspec, _const_spec(w_out.shape), row, row,
                               modspec, _const_spec(w_in.shape), _const_spec(b_in.shape),
                               _const_spec(conv_w.shape), _const_spec(conv_b.shape), hspec,
                               _const_spec(w_down.shape), row, row],
        out_specs=[tok, hspec],
        out_shape=[jax.ShapeDtypeStruct((b, l, d), F32),
                   jax.ShapeDtypeStruct((b, SUBLANES, 2 * D_FF), F32)],
        scratch_shapes=[pltpu.VMEM((SUBLANES, 2 * D_FF), F32), pltpu.VMEM((tm, d), F32),
                        pltpu.VMEM((tm, d), BF16), pltpu.VMEM((tm, D_FF), BF16)],
        compiler_params=_params(("arbitrary", "arbitrary")),
        name="mix_ffn_gated" if gated else "mix_ffn",
    )(*head_args, x, mix_mod, w_out, mix_ln_g, mix_ln_b,
      ffn_mod, w_in, b_in, conv_w, conv_b, hist8, w_down, ln_g, ln_b)


def _rope_pair(pair, cs):
    prod = pair * cs
    return prod + pltpu.roll(prod, MLA_ROPE, axis=1)


def _mla_proj_kernel(x_ref, mod_ref, wkv_ref, kvg_ref, cs_ref, wdq_ref, qg_ref, wuq_ref, *rest, with_kv):
    if with_kv:
        wk_ref, wvt_ref, ckv_ref, kpe_ref, q_ref, k_ref, vt_ref = rest
    else:
        ckv_ref, kpe_ref, q_ref = rest
    x = x_ref[0]
    cs = cs_ref[...]
    lane = lax.broadcasted_iota(jnp.int32, (x.shape[0], LANES), 1)
    kv = jnp.dot(x.astype(BF16), wkv_ref[...], preferred_element_type=F32)
    lat = kv[:, :MLA_KV_LORA]
    ckv = lat * lax.rsqrt(jnp.mean(lat * lat, axis=-1, keepdims=True) + RMS_EPS) * kvg_ref[...]
    kpe = _rope_pair(kv[:, MLA_KV_LORA:], cs)
    ckv_ref[0] = ckv
    kpe_ref[0] = kpe[:, :MLA_ROPE]
    if with_kv:
        _kv_heads(ckv.astype(BF16), jnp.where(lane < MLA_ROPE, kpe, 0.0).astype(BF16), wk_ref, wvt_ref, k_ref, vt_ref)

    shift = mod_ref[0, :, 0:D_MODEL]
    scale = mod_ref[0, :, D_MODEL:2 * D_MODEL]
    hin = (x * (1.0 + scale) + shift).astype(BF16)
    qd = jnp.dot(hin, wdq_ref[...], preferred_element_type=F32)
    qd = qd * lax.rsqrt(jnp.mean(qd * qd, axis=-1, keepdims=True) + RMS_EPS) * qg_ref[...]
    qd = qd.astype(BF16)
    for h in range(MLA_HEADS):
        qh = jnp.dot(qd, wuq_ref[:, h * HEAD_PAD:(h + 1) * HEAD_PAD], preferred_element_type=F32)
        q_ref[0, :, h * HEAD_PAD:h * HEAD_PAD + MLA_NOPE] = (qh[:, :MLA_NOPE] * Q_PRESCALE).astype(BF16)
        pe = jnp.where(lane < MLA_ROPE, _rope_pair(qh[:, MLA_NOPE:], cs) * Q_PRESCALE, 0.0)
        q_ref[0, :, h * HEAD_PAD + MLA_NOPE:(h + 1) * HEAD_PAD] = pe.astype(BF16)


def _mla_proj(x, mod, w_kv, kv_g, cossin, w_dq, q_g, w_uq, w_k, w_vt, with_kv):
    b, l, d = x.shape
    tm = _row_tile(l, 256)
    tok = lambda w: pl.BlockSpec((1, tm, w), lambda i, t: (i, t, 0))
    in_specs = [tok(d), pl.BlockSpec((1, 1, 3 * d), lambda i, t: (i, 0, 0)),
                _const_spec(w_kv.shape), _const_spec(kv_g.shape),
                pl.BlockSpec((tm, LANES), lambda i, t: (t, 0)),
                _const_spec(w_dq.shape), _const_spec(q_g.shape), _const_spec(w_uq.shape)]
    out_specs = [tok(MLA_KV_LORA), tok(MLA_ROPE), tok(MLA_HEADS * HEAD_PAD)]
    out_shape = [jax.ShapeDtypeStruct((b, l, MLA_KV_LORA), F32),
                 jax.ShapeDtypeStruct((b, l, MLA_ROPE), F32),
                 jax.ShapeDtypeStruct((b, l, MLA_HEADS * HEAD_PAD), BF16)]
    args = (x, mod, w_kv, kv_g, cossin, w_dq, q_g, w_uq)
    if with_kv:
        in_specs += [_const_spec(w_k.shape), _const_spec(w_vt.shape)]
        out_specs += [tok(MLA_HEADS * HEAD_PAD), pl.BlockSpec((1, MLA_HEADS * MLA_V, tm), lambda i, t: (i, 0, t))]
        out_shape += [jax.ShapeDtypeStruct((b, l, MLA_HEADS * HEAD_PAD), BF16),
                      jax.ShapeDtypeStruct((b, MLA_HEADS * MLA_V, l), BF16)]
        args += (w_k, w_vt)
    return pl.pallas_call(
        functools.partial(_mla_proj_kernel, with_kv=with_kv),
        grid=(b, l // tm),
        in_specs=in_specs,
        out_specs=out_specs,
        out_shape=out_shape,
        compiler_params=_params(("arbitrary", "arbitrary")),
        name="mla_proj_kv" if with_kv else "mla_proj",
    )(*args)


def _kv_heads(lat, kpe, wk_ref, wvt_ref, k_ref, vt_ref):
    for h in range(MLA_HEADS):
        kn = jnp.dot(lat, wk_ref[:, h * MLA_NOPE:(h + 1) * MLA_NOPE], preferred_element_type=F32)
        k_ref[0, :, h * HEAD_PAD:h * HEAD_PAD + MLA_NOPE] = kn.astype(BF16)
        k_ref[0, :, h * HEAD_PAD + MLA_NOPE:(h + 1) * HEAD_PAD] = kpe
        vt_ref[0, h * MLA_V:(h + 1) * MLA_V, :] = _dg(wvt_ref[h], lat, NT).astype(BF16)


def _kv_up_kernel(ckv_ref, kpe_ref, wk_ref, wvt_ref, k_ref, vt_ref):
    _kv_heads(ckv_ref[0].astype(BF16), kpe_ref[0].astype(BF16), wk_ref, wvt_ref, k_ref, vt_ref)


def _kv_up(ckv_all, kpe_pad, w_k, w_vt):
    b, lk, _ = ckv_all.shape
    tm = ATTN_TK if lk % ATTN_TK == 0 else lk
    tok = lambda w: pl.BlockSpec((1, tm, w), lambda i, t: (i, t, 0))
    return pl.pallas_call(
        _kv_up_kernel,
        grid=(b, lk // tm),
        in_specs=[tok(MLA_KV_LORA), tok(LANES), _const_spec(w_k.shape), _const_spec(w_vt.shape)],
        out_specs=[tok(MLA_HEADS * HEAD_PAD),
                   pl.BlockSpec((1, MLA_HEADS * MLA_V, tm), lambda i, t: (i, 0, t))],
        out_shape=[jax.ShapeDtypeStruct((b, lk, MLA_HEADS * HEAD_PAD), BF16),
                   jax.ShapeDtypeStruct((b, MLA_HEADS * MLA_V, lk), BF16)],
        compiler_params=_params(("arbitrary", "arbitrary")),
        name="kv_up",
    )(ckv_all, kpe_pad, w_k, w_vt)


ATTN_TQ = 512
ATTN_TK = 512
ATTN_HEADS_PER_STEP = 4
CHUNK_SHIFT = CHUNK.bit_length() - 1


def _chunk_end(pos):
    return (lax.shift_right_logical(pos, CHUNK_SHIFT) + 1) * CHUNK


def _attn_kernel(q_ref, k_ref, vt_ref, o_ref, acc_ref, m_ref, l_ref, sa_ref, sb_ref, *, tq, tk, past, lk, nh):
    i = pl.program_id(2)
    acc_ref[...] = jnp.zeros_like(acc_ref)
    m_ref[...] = jnp.full_like(m_ref, NEG_BIG)
    l_ref[...] = jnp.zeros_like(l_ref)
    q0 = past + i * tq
    n_full = lax.div(jnp.minimum(_chunk_end(q0), lk), tk)
    n_end = lax.div(jnp.minimum(_chunk_end(q0 + tq - 1), lk) + tk - 1, tk)
    n_blk = lk // tk
    heads = range(nh)

    def key_start(j):
        return pl.multiple_of(jnp.minimum(j, n_blk - 1) * tk, tk) if lk > tk else 0

    def scores(j, dst):
        k0 = key_start(j)
        for h in heads:
            dst[h] = _dg(k_ref[0, pl.ds(k0, tk), h * HEAD_PAD:(h + 1) * HEAD_PAD],
                         q_ref[0, :, h * HEAD_PAD:(h + 1) * HEAD_PAD], NT)

    def step(j, src, dst, masked):
        if dst is not None:
            scores(j + 1, dst)
        k0 = key_start(j)
        st = [src[h] for h in heads]
        if masked:
            kc = lax.shift_right_logical(k0 + lax.broadcasted_iota(jnp.int32, (tk, tq), 0), CHUNK_SHIFT)
            qc = lax.shift_right_logical(q0 + lax.broadcasted_iota(jnp.int32, (tk, tq), 1), CHUNK_SHIFT)
            visible = kc <= qc
            st = [jnp.where(visible, s, NEG_BIG) for s in st]
        m_old = [m_ref[h] for h in heads]
        m_new = [jnp.maximum(m_old[h], jnp.max(st[h], axis=0, keepdims=True)) for h in heads]
        alpha = [jnp.exp2(m_old[h] - m_new[h]) for h in heads]
        pt = [jnp.exp2(st[h] - m_new[h]) for h in heads]
        l_new = [alpha[h] * l_ref[h] + jnp.sum(pt[h], axis=0, keepdims=True) for h in heads]
        pv = [_dg(vt_ref[0, h * MLA_V:(h + 1) * MLA_V, pl.ds(k0, tk)], pt[h].astype(BF16)) for h in heads]
        for h in heads:
            acc_ref[h] = alpha[h] * acc_ref[h] + pv[h]
            m_ref[h] = m_new[h]
            l_ref[h] = l_new[h]

    if n_blk == 1:
        scores(0, sa_ref)

        @pl.when(n_full == 1)
        def _():
            step(0, sa_ref, None, False)

        @pl.when(n_full == 0)
        def _():
            step(0, sa_ref, None, True)
    else:
        odd = lax.rem(n_full, 2)

        @pl.when(odd == 1)
        def _():
            scores(0, sb_ref)
            step(0, sb_ref, sa_ref, False)

        @pl.when(odd == 0)
        def _():
            scores(0, sa_ref)

        def pair(p, c):
            j = odd + 2 * p
            step(j, sa_ref, sb_ref, False)
            step(j + 1, sb_ref, sa_ref, False)
            return c

        lax.fori_loop(0, lax.div(n_full, 2), pair, 0)

        @pl.when(n_end > n_full)
        def _():
            step(n_full, sa_ref, None, True)

        def rest(j, c):
            scores(j, sa_ref)
            step(j, sa_ref, None, True)
            return c

        lax.fori_loop(n_full + 1, n_end, rest, 0)
    for h in heads:
        o_ref[0, :, h * MLA_V:(h + 1) * MLA_V] = (acc_ref[h] / l_ref[h]).T.astype(BF16)


def _attention(q, k, vt, past):
    b, lq, _ = q.shape
    lk = k.shape[1]
    lq_pad = -(-lq // LANES) * LANES
    if lq_pad != lq:
        q = jnp.pad(q, ((0, 0), (0, lq_pad - lq), (0, 0)))
    tq = _row_tile(lq_pad, ATTN_TQ)
    tk = ATTN_TK if lk % ATTN_TK == 0 else lk
    nh = ATTN_HEADS_PER_STEP
    kv_bytes = nh * lk * (HEAD_PAD + MLA_V) * jnp.dtype(BF16).itemsize
    kv_mode = pl.Buffered(2 if 2 * kv_bytes <= VMEM_LIMIT // 4 else 1)
    out = pl.pallas_call(
        functools.partial(_attn_kernel, tq=tq, tk=tk, past=past, lk=lk, nh=nh),
        grid=(b, MLA_HEADS // nh, lq_pad // tq),
        in_specs=[pl.BlockSpec((1, tq, nh * HEAD_PAD), lambda bi, h, i: (bi, i, h)),
                  pl.BlockSpec((1, lk, nh * HEAD_PAD), lambda bi, h, i: (bi, 0, h), pipeline_mode=kv_mode),
                  pl.BlockSpec((1, nh * MLA_V, lk), lambda bi, h, i: (bi, h, 0), pipeline_mode=kv_mode)],
        out_specs=pl.BlockSpec((1, tq, nh * MLA_V), lambda bi, h, i: (bi, i, h)),
        out_shape=jax.ShapeDtypeStruct((b, lq_pad, MLA_HEADS * MLA_V), BF16),
        scratch_shapes=[pltpu.VMEM((nh, MLA_V, tq), F32), pltpu.VMEM((nh, 1, tq), F32),
                        pltpu.VMEM((nh, 1, tq), F32), pltpu.VMEM((nh, tk, tq), F32),
                        pltpu.VMEM((nh, tk, tq), F32)],
        compiler_params=_params(("arbitrary", "arbitrary", "arbitrary")),
        name="chunk_causal_attention",
    )(q, k, vt)
    return out[:, :lq]


def _pad_hist(hist):
    return jnp.pad(hist, ((0, 0), (SUBLANES - hist.shape[1], 0), (0, 0)))


def _rot_half_cols(w):
    w1, w2 = jnp.split(w, 2, axis=-1)
    return jnp.concatenate([-w2, w1], axis=-1)


def _pack_weights(p):
    d = D_MODEL
    pk = {}
    w_in = p['gdn_w_in'][0]
    pk['gdn_w_main'] = w_in[:, :GDN_CONV_CH + GDN_V].astype(BF16)
    pk['gdn_w_ba'] = jnp.pad(w_in[:, GDN_CONV_CH + GDN_V:], ((0, 0), (0, LANES - 2 * GDN_HEADS))).astype(BF16)
    pad_gate = lambda a: jnp.pad(a.reshape(1, GDN_HEADS), ((0, 0), (GDN_HEADS, LANES - 2 * GDN_HEADS)))
    pk['gdn_alog'] = pad_gate(p['gdn_a_log'][0])
    pk['gdn_dtb'] = pad_gate(p['gdn_dt_bias'][0])
    pk['gdn_conv_w'] = p['gdn_conv_w'][0]
    pk['gdn_norm_g'] = p['gdn_norm_g'][0].reshape(1, GDN_DV)
    pk['gdn_w_out'] = p['gdn_w_out'][0].astype(BF16)
    wkv = p['kv_w_down']
    pk['kv_w_down'] = jnp.concatenate([wkv, _rot_half_cols(wkv[:, MLA_KV_LORA:])], axis=-1).astype(BF16)
    pk['kv_norm_g'] = p['kv_norm_g'].reshape(1, MLA_KV_LORA)
    w_up = p['kv_w_up'].reshape(MLA_KV_LORA, MLA_HEADS, MLA_NOPE + MLA_V)
    pk['kv_w_k'] = w_up[..., :MLA_NOPE].reshape(MLA_KV_LORA, MLA_HEADS * MLA_NOPE).astype(BF16)
    pk['kv_w_vt'] = jnp.transpose(w_up[..., MLA_NOPE:], (1, 2, 0)).astype(BF16)
    pk['mla_w_dq'] = p['mla_w_dq'][0].astype(BF16)
    pk['mla_q_norm_g'] = p['mla_q_norm_g'][0].reshape(1, MLA_Q_LORA)
    wuq = p['mla_w_uq'][0].reshape(MLA_Q_LORA, MLA_HEADS, MLA_NOPE + MLA_ROPE)
    wuq = jnp.concatenate([wuq, _rot_half_cols(wuq[..., MLA_NOPE:])], axis=-1)
    pk['mla_w_uq'] = wuq.reshape(MLA_Q_LORA, MLA_HEADS * HEAD_PAD).astype(BF16)
    pk['mla_w_out'] = p['mla_w_out'][0].astype(BF16)
    pk['ffn_w_in'] = p['ffn_w_in'].astype(BF16)
    pk['ffn_b_in'] = p['ffn_b_in'].reshape(DEPTH, 1, 2 * D_FF)
    pk['ffn_conv_w'] = p['ffn_conv_w']
    pk['ffn_conv_b'] = p['ffn_conv_b'].reshape(DEPTH, 1, 2 * D_FF)
    pk['ffn_w_down'] = p['ffn_w_down'].astype(BF16)
    pk['ln_g'] = p['ln_g'].reshape(2 * DEPTH, 1, d)
    pk['ln_b'] = p['ln_b'].reshape(2 * DEPTH, 1, d)
    return pk


def _rope_table(past, l):
    inv = 1.0 / (ROPE_THETA ** (jnp.arange(0, MLA_ROPE, 2, dtype=F32) / MLA_ROPE))
    ang = (past + jnp.arange(l, dtype=jnp.int32)).astype(F32)[:, None] * inv[None, :]
    ang = jnp.concatenate([ang, ang], axis=-1)
    return jnp.concatenate([jnp.cos(ang), jnp.sin(ang)], axis=-1)


def _block_tail(o, z, norm_g, w_out, x, mods, hist, pk, layer):
    y, hist8 = _mix_ffn(o, z, norm_g, x, mods[2 * layer], w_out, pk['ln_g'][2 * layer], pk['ln_b'][2 * layer],
                        mods[2 * layer + 1], pk['ffn_w_in'][layer], pk['ffn_b_in'][layer],
                        pk['ffn_conv_w'][layer], pk['ffn_conv_b'][layer], _pad_hist(hist),
                        pk['ffn_w_down'][layer], pk['ln_g'][2 * layer + 1], pk['ln_b'][2 * layer + 1])
    return y, hist8[:, SUBLANES - (FFN_CONV - 1):]


def _trunk(x, mods, gdn_state, gdn_conv, ffn_conv, ckv_past, kpe_past, pk):
    b, l, _ = x.shape
    past = kpe_past.shape[1]
    c_len = CHUNK if l % CHUNK == 0 else l
    n = l // c_len

    q, k, v, z, gb, hist8 = _gdn_in(x, mods[0], pk['gdn_w_main'], pk['gdn_w_ba'], _pad_hist(gdn_conv[:, 0]),
                                    pk['gdn_conv_w'], pk['gdn_alog'], pk['gdn_dtb'], c_len)
    col = gb.reshape(b, n, c_len, 2 * GDN_HEADS)
    row = jnp.swapaxes(col, 2, 3)
    o, s_end = _gdn_chunk(q, k, v, col, row, gdn_state[:, 0], c_len)
    gdn_conv_out = hist8[:, None, SUBLANES - (GDN_CONV - 1):]
    x, fh0 = _block_tail(o, z, pk['gdn_norm_g'], pk['gdn_w_out'], x, mods, ffn_conv[:, 0], pk, 0)

    proj = _mla_proj(x, mods[2], pk['kv_w_down'], pk['kv_norm_g'], _rope_table(past, l),
                     pk['mla_w_dq'], pk['mla_q_norm_g'], pk['mla_w_uq'], pk['kv_w_k'], pk['kv_w_vt'],
                     with_kv=(past == 0))
    ckv_new, kpe_new, qm = proj[:3]
    if past == 0:
        km, vm = proj[3:]
    else:
        ckv_all = jnp.concatenate([ckv_past, ckv_new], axis=1)
        kpe_all = jnp.concatenate([kpe_past, kpe_new], axis=1)
        kpe_pad = jnp.pad(kpe_all, ((0, 0), (0, 0), (0, LANES - MLA_ROPE)))
        km, vm = _kv_up(ckv_all, kpe_pad, pk['kv_w_k'], pk['kv_w_vt'])
    o = _attention(qm, km, vm, past)
    x, fh1 = _block_tail(o, None, None, pk['mla_w_out'], x, mods, ffn_conv[:, 1], pk, 1)

    return (x, s_end[:, None], gdn_conv_out, jnp.stack([fh0, fh1], axis=1), ckv_new, kpe_new)


def kernel(x_prompt, x_sample, c_prompt, c_sample, state_gdn, state_gdn_conv, state_ffn_conv, cache_ckv, cache_kpe, ada_w, ada_b, ln_g, ln_b, gdn_w_in, gdn_conv_w, gdn_a_log, gdn_dt_bias, gdn_norm_g, gdn_w_out, kv_w_down, kv_norm_g, kv_w_up, mla_w_dq, mla_q_norm_g, mla_w_uq, mla_w_out, ffn_w_in, ffn_b_in, ffn_conv_w, ffn_conv_b, ffn_w_down):
    p = {'ln_g': ln_g, 'ln_b': ln_b, 'gdn_w_in': gdn_w_in, 'gdn_conv_w': gdn_conv_w, 'gdn_a_log': gdn_a_log,
         'gdn_dt_bias': gdn_dt_bias, 'gdn_norm_g': gdn_norm_g, 'gdn_w_out': gdn_w_out,
         'kv_w_down': kv_w_down, 'kv_norm_g': kv_norm_g, 'kv_w_up': kv_w_up,
         'mla_w_dq': mla_w_dq, 'mla_q_norm_g': mla_q_norm_g, 'mla_w_uq': mla_w_uq, 'mla_w_out': mla_w_out,
         'ffn_w_in': ffn_w_in, 'ffn_b_in': ffn_b_in, 'ffn_conv_w': ffn_conv_w,
         'ffn_conv_b': ffn_conv_b, 'ffn_w_down': ffn_w_down}
    pk = _pack_weights(p)
    bp = x_prompt.shape[0]
    mods = _ada_terms(jnp.concatenate([c_prompt, c_sample], axis=0), ada_w, ada_b)[:, :, None, :]
    zeros_like_b = lambda a: jnp.zeros((bp,) + a.shape[1:], a.dtype)
    out_p = _trunk(x_prompt, mods[:, :bp], zeros_like_b(state_gdn), zeros_like_b(state_gdn_conv),
                   zeros_like_b(state_ffn_conv), jnp.zeros((bp, 0, MLA_KV_LORA), cache_ckv.dtype),
                   jnp.zeros((bp, 0, MLA_ROPE), cache_kpe.dtype), pk)
    out_s = _trunk(x_sample, mods[:, bp:], state_gdn, state_gdn_conv, state_ffn_conv, cache_ckv, cache_kpe, pk)
    return (out_p[0], out_s[0]) + out_p[1:] + out_s[1:]
```

```python
import functools

import jax
import jax.numpy as jnp
from jax import lax
from jax.experimental import pallas as pl
from jax.experimental.pallas import tpu as pltpu

F32 = jnp.float32
BF16 = jnp.bfloat16

D_MODEL = 1024
DEPTH = 2
CHUNK = 64
ALPHA = (2.0 * DEPTH) ** 0.25
LN_EPS = 1e-5
RMS_EPS = 1e-6
GDN_HEADS = 8
GDN_DK = 128
GDN_DV = 128
GDN_CONV = 4
GDN_QK = GDN_HEADS * GDN_DK
GDN_V = GDN_HEADS * GDN_DV
GDN_CONV_CH = 2 * GDN_QK + GDN_V
MLA_HEADS = 8
MLA_NOPE = 128
MLA_ROPE = 64
MLA_V = 128
MLA_KV_LORA = 256
MLA_Q_LORA = 384
ROPE_THETA = 10000.0
MLA_SCALE = (MLA_NOPE + MLA_ROPE) ** -0.5
Q_PRESCALE = MLA_SCALE * 1.4426950408889634
D_FF = 2816
FFN_CONV = 3

LANES = 128
SUBLANES = 8
MXU_DIM = 256
HEAD_PAD = 256
VMEM_LIMIT = 56 * 1024 * 1024
NEG_BIG = -1e30

NN = ((1,), (0,))
NT = ((1,), (1,))
TN = ((0,), (0,))


def _row_tile(n, cap):
    if n <= cap:
        return n
    for t in range(cap, 15, -1):
        if n % t == 0 and t % 16 == 0:
            return t
    raise ValueError(f"no row tile for {n}")


def _split3(a):
    hi = a.astype(BF16)
    r = a - hi.astype(F32)
    mid = r.astype(BF16)
    lo = (r - mid.astype(F32)).astype(BF16)
    return hi, mid, lo


def _dg(a, b, dims=NN):
    return lax.dot_general(a, b, (dims, ((), ())), preferred_element_type=F32)


def _dot_hp(a, b, dims=NN):
    a0, a1, a2 = _split3(a)
    b0, b1, b2 = _split3(b)
    out = _dg(a0, b0, dims)
    out = out + (_dg(a0, b1, dims) + _dg(a1, b0, dims))
    out = out + (_dg(a0, b2, dims) + _dg(a1, b1, dims) + _dg(a2, b0, dims))
    return out


def _sigmoid(x):
    return 1.0 / (1.0 + jnp.exp(-x))


def _silu(x):
    return x * _sigmoid(x)


def _layer_norm(y, g, b):
    mu = jnp.mean(y, axis=-1, keepdims=True)
    yc = y - mu
    var = jnp.mean(yc * yc, axis=-1, keepdims=True)
    return yc * lax.rsqrt(var + LN_EPS) * g + b


def _const_spec(shape):
    nd = len(shape)
    return pl.BlockSpec(shape, lambda *_: (0,) * nd, pipeline_mode=pl.Buffered(1))


def _shift_rows(u, carry8, j):
    rows, width = u.shape
    groups = rows // SUBLANES
    rot = pltpu.roll(u.reshape(groups, SUBLANES, width), j, axis=1)
    above = jnp.concatenate([pltpu.roll(carry8, j, axis=0)[None], rot[:groups - 1]], axis=0)
    sub = lax.broadcasted_iota(jnp.int32, rot.shape, 1)
    return jnp.where(sub < j, above, rot).reshape(rows, width)


def _params(sem):
    return pltpu.CompilerParams(dimension_semantics=sem, vmem_limit_bytes=VMEM_LIMIT)


def _ada_kernel(c_ref, w_ref, b_ref, o_ref):
    s = _silu(c_ref[...])
    o_ref[0] = _dot_hp(s, w_ref[0]) + b_ref[0]


def _ada_terms(c_all, ada_w, ada_b):
    n_sub, d, n3 = ada_w.shape
    bc = c_all.shape[0]
    tn = 768
    return pl.pallas_call(
        _ada_kernel,
        grid=(n_sub, n3 // tn),
        in_specs=[
            pl.BlockSpec((bc, d), lambda i, j: (0, 0)),
            pl.BlockSpec((1, d, tn), lambda i, j: (i, 0, j)),
            pl.BlockSpec((1, 1, tn), lambda i, j: (i, 0, j)),
        ],
        out_specs=pl.BlockSpec((1, bc, tn), lambda i, j: (i, 0, j)),
        out_shape=jax.ShapeDtypeStruct((n_sub, bc, n3), F32),
        compiler_params=_params(("arbitrary", "arbitrary")),
        name="ada_terms",
    )(c_all, ada_w, ada_b.reshape(n_sub, 1, n3))


GDN_IN_TILE = 256


def _gdn_in_kernel(x_ref, mod_ref, w_ref, wba_ref, hist_ref, cw_ref, alog_ref, dtb_ref, tri_ref,
                   q_ref, k_ref, v_ref, z_ref, gb_ref, hist_out_ref, carry, *, tm):
    t = pl.program_id(1)

    @pl.when(t == 0)
    def _():
        carry[...] = hist_ref[0]

    shift = mod_ref[0, :, 0:D_MODEL]
    scale = mod_ref[0, :, D_MODEL:2 * D_MODEL]
    hin = (x_ref[0] * (1.0 + scale) + shift).astype(BF16)

    outs = (q_ref, k_ref, v_ref)
    for sec in range(3):
        for c in range(GDN_QK // GDN_IN_TILE):
            lo = sec * GDN_QK + c * GDN_IN_TILE
            cols = slice(lo, lo + GDN_IN_TILE)
            ocols = slice(c * GDN_IN_TILE, (c + 1) * GDN_IN_TILE)
            u = jnp.dot(hin, w_ref[:, cols], preferred_element_type=F32)
            c8 = carry[:, cols]
            acc = u * cw_ref[GDN_CONV - 1:GDN_CONV, cols]
            for j in range(1, GDN_CONV):
                acc = acc + _shift_rows(u, c8, j) * cw_ref[GDN_CONV - 1 - j:GDN_CONV - j, cols]
            carry[:, cols] = u[tm - SUBLANES:tm]
            s = _silu(acc)
            if sec == 2:
                v_ref[0, :, ocols] = s.astype(BF16)
            else:
                post = GDN_DK ** -0.5 if sec == 0 else 1.0
                parts = []
                for h in range(GDN_IN_TILE // GDN_DK):
                    sh = s[:, h * GDN_DK:(h + 1) * GDN_DK]
                    inv = lax.rsqrt(jnp.sum(sh * sh, axis=-1, keepdims=True) + RMS_EPS)
                    parts.append(sh * (inv * post))
                outs[sec][0, :, ocols] = jnp.concatenate(parts, axis=-1).astype(BF16)

    z_ref[0] = jnp.dot(hin, w_ref[:, GDN_CONV_CH:GDN_CONV_CH + GDN_V], preferred_element_type=F32).astype(BF16)

    ba = jnp.dot(hin, wba_ref[...], preferred_element_type=F32)
    beta = _sigmoid(ba)
    xs = ba + dtb_ref[...]
    softplus = jnp.maximum(xs, 0.0) + jnp.log(1.0 + jnp.exp(-jnp.abs(xs)))
    g = -jnp.exp(alog_ref[...]) * softplus
    tri = tri_ref[...]
    g0, g1, g2 = _split3(g)
    gsum = (jnp.dot(tri, g0, preferred_element_type=F32)
            + jnp.dot(tri, g1, preferred_element_type=F32)
            + jnp.dot(tri, g2, preferred_element_type=F32))
    lane = lax.broadcasted_iota(jnp.int32, (tm, LANES), 1)
    gb_ref[0] = jnp.where(lane < GDN_HEADS, beta, gsum)[:, :2 * GDN_HEADS]
    hist_out_ref[0] = carry[...]


def _gdn_in(x, mod, w_main, w_ba, hist8, conv_w, alog_row, dtb_row, c_len):
    b, l, d = x.shape
    tm = _row_tile(l, 512)
    assert tm % c_len == 0
    idx = jnp.arange(tm, dtype=jnp.int32)
    tri = ((idx[:, None] >= idx[None, :]) & ((idx[:, None] // c_len) == (idx[None, :] // c_len))).astype(BF16)
    tok = lambda w: pl.BlockSpec((1, tm, w), lambda i, t: (i, t, 0))
    return pl.pallas_call(
        functools.partial(_gdn_in_kernel, tm=tm),
        grid=(b, l // tm),
        in_specs=[
            tok(d),
            pl.BlockSpec((1, 1, 3 * d), lambda i, t: (i, 0, 0)),
            _const_spec(w_main.shape),
            _const_spec(w_ba.shape),
            pl.BlockSpec((1, SUBLANES, GDN_CONV_CH), lambda i, t: (i, 0, 0)),
            _const_spec(conv_w.shape),
            _const_spec(alog_row.shape),
            _const_spec(dtb_row.shape),
            _const_spec(tri.shape),
        ],
        out_specs=[tok(GDN_QK), tok(GDN_QK), tok(GDN_V), tok(GDN_V), tok(2 * GDN_HEADS),
                   pl.BlockSpec((1, SUBLANES, GDN_CONV_CH), lambda i, t: (i, 0, 0))],
        out_shape=[
            jax.ShapeDtypeStruct((b, l, GDN_QK), BF16),
            jax.ShapeDtypeStruct((b, l, GDN_QK), BF16),
            jax.ShapeDtypeStruct((b, l, GDN_V), BF16),
            jax.ShapeDtypeStruct((b, l, GDN_V), BF16),
            jax.ShapeDtypeStruct((b, l, 2 * GDN_HEADS), F32),
            jax.ShapeDtypeStruct((b, SUBLANES, GDN_CONV_CH), F32),
        ],
        scratch_shapes=[pltpu.VMEM((SUBLANES, GDN_CONV_CH), F32)],
        compiler_params=_params(("arbitrary", "arbitrary")),
        name="gdn_in",
    )(x, mod, w_main, w_ba, hist8, conv_w, alog_row, dtb_row, tri)


GDN_GROUP = 4


def _bf(xs):
    return [x.astype(BF16) for x in xs]


def _gdn_chunk_kernel(q_ref, k_ref, v_ref, col_ref, row_ref, s0_ref, o_ref, s_ref, *, c_len, group):
    @pl.when(pl.program_id(1) == 0)
    def _():
        s_ref[...] = s0_ref[...]

    ri = lax.broadcasted_iota(jnp.int32, (c_len, c_len), 0)
    ci = lax.broadcasted_iota(jnp.int32, (c_len, c_len), 1)
    incl = ri >= ci
    strict = ri > ci
    n_sq = c_len.bit_length() - 2
    heads = range(GDN_HEADS)
    items = [(g, h) for g in range(group) for h in heads]
    each = range(len(items))
    rows = lambda g: slice(g * c_len, (g + 1) * c_len)
    lanes = lambda h: slice(h * GDN_DK, (h + 1) * GDN_DK)

    col = [col_ref[0, g] for g in range(group)]
    row = [row_ref[0, g] for g in range(group)]
    qb = [q_ref[0, rows(g), lanes(h)] for g, h in items]
    kb = [k_ref[0, rows(g), lanes(h)] for g, h in items]
    k = [x.astype(F32) for x in kb]
    v = [v_ref[0, rows(g), lanes(h)].astype(F32) for g, h in items]
    beta = [col[g][:, h:h + 1] for g, h in items]
    gc = [col[g][:, GDN_HEADS + h:GDN_HEADS + h + 1] for g, h in items]
    gr = [row[g][GDN_HEADS + h:GDN_HEADS + h + 1, :] for g, h in items]
    gam = [jnp.where(incl, jnp.exp(jnp.where(incl, gc[i] - gr[i], 0.0)), 0.0) for i in each]
    eg = [jnp.exp(gc[i]) for i in each]
    qkk = [_dg(jnp.concatenate([qb[i], kb[i]], axis=0), kb[i], NT) for i in each]
    p = _bf([jnp.where(strict, -(beta[i] * qkk[i][c_len:] * gam[i]), 0.0) for i in each])
    sol = [jnp.concatenate([beta[i] * v[i], beta[i] * k[i] * eg[i]], axis=-1) for i in each]
    solb = _bf(sol)
    sol = [sol[i] + _dg(p[i], solb[i]) for i in each]
    for _ in range(n_sq):
        p = _bf([_dg(p[i], p[i]) for i in each])
        solb = _bf(sol)
        sol = [sol[i] + _dg(p[i], solb[i]) for i in each]
    u = [sol[i][:, :GDN_DV] for i in each]
    wq = [jnp.concatenate([sol[i][:, GDN_DV:].astype(BF16), (qb[i].astype(F32) * eg[i]).astype(BF16)], axis=0)
          for i in each]
    att = _bf([qkk[i][:c_len] * gam[i] for i in each])
    g_last = [gc[i][c_len - 1:c_len, :] for i in each]
    k_dec = _bf([k[i] * jnp.exp(g_last[i] - gc[i]) for i in each])
    decay = [jnp.exp(g_last[i]) for i in each]

    s_cur = [s_ref[0, h] for h in heads]
    for g in range(group):
        it = [g * GDN_HEADS + h for h in heads]
        sb = _bf(s_cur)
        ws = [_dg(wq[it[h]], sb[h]) for h in heads]
        vb = _bf([u[it[h]] - ws[h][:c_len] for h in heads])
        o = [ws[h][c_len:] + _dg(att[it[h]], vb[h]) for h in heads]
        s_cur = [decay[it[h]] * s_cur[h] + _dg(k_dec[it[h]], vb[h], TN) for h in heads]
        o_ref[0, rows(g), :] = jnp.concatenate(o, axis=-1).astype(BF16)
    for h in heads:
        s_ref[0, h] = s_cur[h]


def _gdn_chunk(q, k, v, col, row, s0, c_len):
    b, l, _ = q.shape
    n = l // c_len
    group = GDN_GROUP if n % GDN_GROUP == 0 else 1
    rows = c_len * group
    tok = pl.BlockSpec((1, rows, GDN_QK), lambda i, t: (i, t, 0))
    st = pl.BlockSpec((1, GDN_HEADS, GDN_DK, GDN_DV), lambda i, t: (i, 0, 0, 0))
    return pl.pallas_call(
        functools.partial(_gdn_chunk_kernel, c_len=c_len, group=group),
        grid=(b, n // group),
        in_specs=[tok, tok, tok,
                  pl.BlockSpec((1, group, c_len, 2 * GDN_HEADS), lambda i, t: (i, t, 0, 0)),
                  pl.BlockSpec((1, group, 2 * GDN_HEADS, c_len), lambda i, t: (i, t, 0, 0)),
                  st],
        out_specs=[tok, st],
        out_shape=[jax.ShapeDtypeStruct((b, l, GDN_V), BF16),
                   jax.ShapeDtypeStruct(s0.shape, F32)],
        compiler_params=_params(("arbitrary", "arbitrary")),
        name="gdn_chunk",
    )(q, k, v, col, row, s0)


FFN_TILE = MXU_DIM
FFN_ROWS = 512


def _mix_ffn_kernel(*refs, gated, tm):
    if gated:
        o_ref, z_ref, ng_ref = refs[:3]
        refs = refs[3:]
    else:
        o_ref = refs[0]
        refs = refs[1:]
    (x_ref, mmod_ref, wo_ref, mlg_ref, mlb_ref, fmod_ref, win_ref, bin_ref, cw_ref, cb_ref, hist_ref, wdn_ref,
     lg_ref, lb_ref, y_ref, hist_out_ref, carry, x1_ref, hbuf, fbuf) = refs
    t = pl.program_id(1)

    @pl.when(t == 0)
    def _():
        carry[...] = hist_ref[0]

    if gated:
        for h in range(GDN_HEADS):
            hs = slice(h * GDN_DV, (h + 1) * GDN_DV)
            oh = o_ref[0, :, hs].astype(F32)
            r = oh * lax.rsqrt(jnp.mean(oh * oh, axis=-1, keepdims=True) + RMS_EPS) * ng_ref[...]
            hbuf[:, hs] = (r * _silu(z_ref[0, :, hs].astype(F32))).astype(BF16)
        a = hbuf[...]
    else:
        a = o_ref[0]
    mix = jnp.dot(a, wo_ref[...], preferred_element_type=F32)
    mgate = mmod_ref[0, :, 2 * D_MODEL:3 * D_MODEL]
    x1_ref[...] = _layer_norm(ALPHA * x_ref[0] + (1.0 + mgate) * mix, mlg_ref[...], mlb_ref[...])

    shift = fmod_ref[0, :, 0:D_MODEL]
    scale = fmod_ref[0, :, D_MODEL:2 * D_MODEL]
    gate = fmod_ref[0, :, 2 * D_MODEL:3 * D_MODEL]
    hbuf[...] = (x1_ref[...] * (1.0 + scale) + shift).astype(BF16)

    def conv_half(lo):
        cols = slice(lo, lo + FFN_TILE)
        u = jnp.dot(hbuf[...], win_ref[:, cols], preferred_element_type=F32) + bin_ref[:, cols]
        c8 = carry[:, cols]
        out = u * cw_ref[FFN_CONV - 1:FFN_CONV, cols] + cb_ref[:, cols]
        for j in range(1, FFN_CONV):
            out = out + _shift_rows(u, c8, j) * cw_ref[FFN_CONV - 1 - j:FFN_CONV - j, cols]
        carry[:, cols] = u[tm - SUBLANES:tm]
        return out

    for i in range(D_FF // FFN_TILE):
        lo = i * FFN_TILE
        ua = conv_half(lo)
        ub = conv_half(lo + D_FF)
        fbuf[:, lo:lo + FFN_TILE] = (_silu(ua) * ub).astype(BF16)

    hist_out_ref[0] = carry[...]
    f = jnp.dot(fbuf[...], wdn_ref[...], preferred_element_type=F32)
    y = ALPHA * x1_ref[...] + (1.0 + gate) * f
    y_ref[0] = _layer_norm(y, lg_ref[...], lb_ref[...])


def _layer_spec(stacked, layer):
    nd = stacked.ndim - 1
    return pl.BlockSpec((None,) + stacked.shape[1:], lambda *_: (layer,) + (0,) * nd, pipeline_mode=pl.Buffered(1))


def _mix_ffn(o, z, norm_g, x, mix_mod, w_out, mix_ln_g, mix_ln_b,
             ffn_mod, w_in_all, b_in, conv_w, conv_b, hist8, w_down_all, ln_g, ln_b, layer):
    b, l, d = x.shape
    tm = _row_tile(l, FFN_ROWS)
    gated = z is not None
    tok = pl.BlockSpec((1, tm, d), lambda i, t: (i, t, 0))
    modspec = pl.BlockSpec((1, 1, 3 * d), lambda i, t: (i, 0, 0))
    hspec = pl.BlockSpec((1, SUBLANES, 2 * D_FF), lambda i, t: (i, 0, 0))
    row = _const_spec((1, d))
    head_args, head_specs = ((o, z, norm_g), [tok, tok, _const_spec(norm_g.shape)]) if gated else ((o,), [tok])
    return pl.pallas_call(
        functools.partial(_mix_ffn_kernel, gated=gated, tm=tm),
        grid=(b, l // tm),
        in_specs=head_specs + [tok, modspec, _const_spec(w_out.shape), row, row,
                               modspec, _layer_spec(w_in_all, layer), _const_spec(b_in.shape),
                               _const_spec(conv_w.shape), _const_spec(conv_b.shape), hspec,
                               _layer_spec(w_down_all, layer), row, row],
        out_specs=[tok, hspec],
        out_shape=[jax.ShapeDtypeStruct((b, l, d), F32),
                   jax.ShapeDtypeStruct((b, SUBLANES, 2 * D_FF), F32)],
        scratch_shapes=[pltpu.VMEM((SUBLANES, 2 * D_FF), F32), pltpu.VMEM((tm, d), F32),
                        pltpu.VMEM((tm, d), BF16), pltpu.VMEM((tm, D_FF), BF16)],
        compiler_params=_params(("arbitrary", "arbitrary")),
        name="mix_ffn_gated" if gated else "mix_ffn",
    )(*head_args, x, mix_mod, w_out, mix_ln_g, mix_ln_b,
      ffn_mod, w_in_all, b_in, conv_w, conv_b, hist8, w_down_all, ln_g, ln_b)


def _rope_pair(pair, cs):
    prod = pair * cs
    return prod + pltpu.roll(prod, MLA_ROPE, axis=1)


def _mla_proj_kernel(x_ref, mod_ref, wkv_ref, kvg_ref, cs_ref, wdq_ref, qg_ref, wuq_ref, *rest, with_kv):
    if with_kv:
        wk_ref, wvt_ref, ckv_ref, kpe_ref, q_ref, k_ref, vt_ref = rest
    else:
        ckv_ref, kpe_ref, q_ref = rest
    x = x_ref[0]
    cs = cs_ref[...]
    lane = lax.broadcasted_iota(jnp.int32, (x.shape[0], LANES), 1)
    kv = jnp.dot(x.astype(BF16), wkv_ref[...], preferred_element_type=F32)
    lat = kv[:, :MLA_KV_LORA]
    ckv = lat * lax.rsqrt(jnp.mean(lat * lat, axis=-1, keepdims=True) + RMS_EPS) * kvg_ref[...]
    kpe = _rope_pair(kv[:, MLA_KV_LORA:], cs)
    ckv_ref[0] = ckv
    kpe_ref[0] = kpe[:, :MLA_ROPE]
    if with_kv:
        _kv_heads(ckv.astype(BF16), jnp.where(lane < MLA_ROPE, kpe, 0.0).astype(BF16), wk_ref, wvt_ref, k_ref, vt_ref)

    shift = mod_ref[0, :, 0:D_MODEL]
    scale = mod_ref[0, :, D_MODEL:2 * D_MODEL]
    hin = (x * (1.0 + scale) + shift).astype(BF16)
    qd = jnp.dot(hin, wdq_ref[...], preferred_element_type=F32)
    qd = qd * lax.rsqrt(jnp.mean(qd * qd, axis=-1, keepdims=True) + RMS_EPS) * qg_ref[...]
    qd = qd.astype(BF16)
    for h in range(MLA_HEADS):
        qh = jnp.dot(qd, wuq_ref[:, h * HEAD_PAD:(h + 1) * HEAD_PAD], preferred_element_type=F32)
        q_ref[0, :, h * HEAD_PAD:h * HEAD_PAD + MLA_NOPE] = (qh[:, :MLA_NOPE] * Q_PRESCALE).astype(BF16)
        pe = jnp.where(lane < MLA_ROPE, _rope_pair(qh[:, MLA_NOPE:], cs) * Q_PRESCALE, 0.0)
        q_ref[0, :, h * HEAD_PAD + MLA_NOPE:(h + 1) * HEAD_PAD] = pe.astype(BF16)


def _mla_proj(x, mod, w_kv, kv_g, cossin, w_dq, q_g, w_uq, w_k, w_vt, with_kv):
    b, l, d = x.shape
    tm = _row_tile(l, 512)
    tok = lambda w: pl.BlockSpec((1, tm, w), lambda i, t: (i, t, 0))
    in_specs = [tok(d), pl.BlockSpec((1, 1, 3 * d), lambda i, t: (i, 0, 0)),
                _const_spec(w_kv.shape), _const_spec(kv_g.shape),
                pl.BlockSpec((tm, LANES), lambda i, t: (t, 0)),
                _const_spec(w_dq.shape), _const_spec(q_g.shape), _const_spec(w_uq.shape)]
    out_specs = [tok(MLA_KV_LORA), tok(MLA_ROPE), tok(MLA_HEADS * HEAD_PAD)]
    out_shape = [jax.ShapeDtypeStruct((b, l, MLA_KV_LORA), F32),
                 jax.ShapeDtypeStruct((b, l, MLA_ROPE), F32),
                 jax.ShapeDtypeStruct((b, l, MLA_HEADS * HEAD_PAD), BF16)]
    args = (x, mod, w_kv, kv_g, cossin, w_dq, q_g, w_uq)
    if with_kv:
        in_specs += [_const_spec(w_k.shape), _const_spec(w_vt.shape)]
        out_specs += [tok(MLA_HEADS * HEAD_PAD), pl.BlockSpec((1, MLA_HEADS * MLA_V, tm), lambda i, t: (i, 0, t))]
        out_shape += [jax.ShapeDtypeStruct((b, l, MLA_HEADS * HEAD_PAD), BF16),
                      jax.ShapeDtypeStruct((b, MLA_HEADS * MLA_V, l), BF16)]
        args += (w_k, w_vt)
    return pl.pallas_call(
        functools.partial(_mla_proj_kernel, with_kv=with_kv),
        grid=(b, l // tm),
        in_specs=in_specs,
        out_specs=out_specs,
        out_shape=out_shape,
        compiler_params=_params(("arbitrary", "arbitrary")),
        name="mla_proj_kv" if with_kv else "mla_proj",
    )(*args)


def _kv_heads(lat, kpe, wk_ref, wvt_ref, k_ref, vt_ref):
    for h in range(MLA_HEADS):
        kn = jnp.dot(lat, wk_ref[:, h * MLA_NOPE:(h + 1) * MLA_NOPE], preferred_element_type=F32)
        k_ref[0, :, h * HEAD_PAD:h * HEAD_PAD + MLA_NOPE] = kn.astype(BF16)
        k_ref[0, :, h * HEAD_PAD + MLA_NOPE:(h + 1) * HEAD_PAD] = kpe
        vt_ref[0, h * MLA_V:(h + 1) * MLA_V, :] = _dg(wvt_ref[h], lat, NT).astype(BF16)


def _kv_up_kernel(ckv_ref, kpe_ref, wk_ref, wvt_ref, k_ref, vt_ref):
    _kv_heads(ckv_ref[0].astype(BF16), kpe_ref[0].astype(BF16), wk_ref, wvt_ref, k_ref, vt_ref)


def _kv_up(ckv_all, kpe_pad, w_k, w_vt):
    b, lk, _ = ckv_all.shape
    tm = ATTN_TK if lk % ATTN_TK == 0 else lk
    tok = lambda w: pl.BlockSpec((1, tm, w), lambda i, t: (i, t, 0))
    return pl.pallas_call(
        _kv_up_kernel,
        grid=(b, lk // tm),
        in_specs=[tok(MLA_KV_LORA), tok(LANES), _const_spec(w_k.shape), _const_spec(w_vt.shape)],
        out_specs=[tok(MLA_HEADS * HEAD_PAD),
                   pl.BlockSpec((1, MLA_HEADS * MLA_V, tm), lambda i, t: (i, 0, t))],
        out_shape=[jax.ShapeDtypeStruct((b, lk, MLA_HEADS * HEAD_PAD), BF16),
                   jax.ShapeDtypeStruct((b, MLA_HEADS * MLA_V, lk), BF16)],
        compiler_params=_params(("arbitrary", "arbitrary")),
        name="kv_up",
    )(ckv_all, kpe_pad, w_k, w_vt)


ATTN_TQ = 512
ATTN_TK = 512
ATTN_HEADS_PER_STEP = 4
CHUNK_SHIFT = CHUNK.bit_length() - 1


def _chunk_end(pos):
    return (lax.shift_right_logical(pos, CHUNK_SHIFT) + 1) * CHUNK


def _attn_kernel(q_ref, k_ref, vt_ref, o_ref, acc_ref, m_ref, l_ref, sa_ref, sb_ref, *, tq, tk, past, lk, nh):
    i = pl.program_id(2)
    acc_ref[...] = jnp.zeros_like(acc_ref)
    m_ref[...] = jnp.full_like(m_ref, NEG_BIG)
    l_ref[...] = jnp.zeros_like(l_ref)
    q0 = past + i * tq
    n_full = lax.div(jnp.minimum(_chunk_end(q0), lk), tk)
    n_end = lax.div(jnp.minimum(_chunk_end(q0 + tq - 1), lk) + tk - 1, tk)
    n_blk = lk // tk
    heads = range(nh)
    ones = jnp.ones((2 * SUBLANES, tk), BF16)

    def key_start(j):
        return pl.multiple_of(jnp.minimum(j, n_blk - 1) * tk, tk) if lk > tk else 0

    def scores(j, dst):
        k0 = key_start(j)
        for h in heads:
            dst[h] = _dg(k_ref[0, pl.ds(k0, tk), h * HEAD_PAD:(h + 1) * HEAD_PAD],
                         q_ref[0, :, h * HEAD_PAD:(h + 1) * HEAD_PAD], NT)

    def step(j, src, dst, masked):
        if dst is not None:
            scores(j + 1, dst)
        k0 = key_start(j)
        st = [src[h] for h in heads]
        if masked:
            kc = lax.shift_right_logical(k0 + lax.broadcasted_iota(jnp.int32, (tk, tq), 0), CHUNK_SHIFT)
            qc = lax.shift_right_logical(q0 + lax.broadcasted_iota(jnp.int32, (tk, tq), 1), CHUNK_SHIFT)
            visible = kc <= qc
            st = [jnp.where(visible, s, NEG_BIG) for s in st]
        m_old = [m_ref[h] for h in heads]
        m_new = [jnp.maximum(m_old[h], jnp.max(st[h], axis=0, keepdims=True)) for h in heads]
        alpha = [jnp.exp2(m_old[h] - m_new[h]) for h in heads]
        pt = [jnp.exp2(st[h] - m_new[h]).astype(BF16) for h in heads]
        pv = [_dg(jnp.concatenate([vt_ref[0, h * MLA_V:(h + 1) * MLA_V, pl.ds(k0, tk)], ones], axis=0), pt[h])
              for h in heads]
        for h in heads:
            acc_ref[h] = alpha[h] * acc_ref[h] + pv[h][:MLA_V]
            m_ref[h] = m_new[h]
            l_ref[h] = alpha[h] * l_ref[h] + pv[h][MLA_V:MLA_V + 1]

    if n_blk == 1:
        scores(0, sa_ref)

        @pl.when(n_full == 1)
        def _():
            step(0, sa_ref, None, False)

        @pl.when(n_full == 0)
        def _():
            step(0, sa_ref, None, True)
    else:
        odd = lax.rem(n_full, 2)

        @pl.when(odd == 1)
        def _():
            scores(0, sb_ref)
            step(0, sb_ref, sa_ref, False)

        @pl.when(odd == 0)
        def _():
            scores(0, sa_ref)

        def pair(p, c):
            j = odd + 2 * p
            step(j, sa_ref, sb_ref, False)
            step(j + 1, sb_ref, sa_ref, False)
            return c

        lax.fori_loop(0, lax.div(n_full, 2), pair, 0)

        @pl.when(n_end > n_full)
        def _():
            step(n_full, sa_ref, None, True)

        def rest(j, c):
            scores(j, sa_ref)
            step(j, sa_ref, None, True)
            return c

        lax.fori_loop(n_full + 1, n_end, rest, 0)
    for h in heads:
        o_ref[0, :, h * MLA_V:(h + 1) * MLA_V] = (acc_ref[h] / l_ref[h]).T.astype(BF16)


def _attention(q, k, vt, past):
    b, lq, _ = q.shape
    lk = k.shape[1]
    lq_pad = -(-lq // LANES) * LANES
    if lq_pad != lq:
        q = jnp.pad(q, ((0, 0), (0, lq_pad - lq), (0, 0)))
    tq = _row_tile(lq_pad, ATTN_TQ)
    tk = ATTN_TK if lk % ATTN_TK == 0 else lk
    nh = ATTN_HEADS_PER_STEP
    kv_bytes = nh * lk * (HEAD_PAD + MLA_V) * jnp.dtype(BF16).itemsize
    kv_mode = pl.Buffered(2 if 2 * kv_bytes <= VMEM_LIMIT // 4 else 1)
    out = pl.pallas_call(
        functools.partial(_attn_kernel, tq=tq, tk=tk, past=past, lk=lk, nh=nh),
        grid=(b, MLA_HEADS // nh, lq_pad // tq),
        in_specs=[pl.BlockSpec((1, tq, nh * HEAD_PAD), lambda bi, h, i: (bi, i, h)),
                  pl.BlockSpec((1, lk, nh * HEAD_PAD), lambda bi, h, i: (bi, 0, h), pipeline_mode=kv_mode),
                  pl.BlockSpec((1, nh * MLA_V, lk), lambda bi, h, i: (bi, h, 0), pipeline_mode=kv_mode)],
        out_specs=pl.BlockSpec((1, tq, nh * MLA_V), lambda bi, h, i: (bi, i, h)),
        out_shape=jax.ShapeDtypeStruct((b, lq_pad, MLA_HEADS * MLA_V), BF16),
        scratch_shapes=[pltpu.VMEM((nh, MLA_V, tq), F32), pltpu.VMEM((nh, 1, tq), F32),
                        pltpu.VMEM((nh, 1, tq), F32), pltpu.VMEM((nh, tk, tq), F32),
                        pltpu.VMEM((nh, tk, tq), F32)],
        compiler_params=_params(("arbitrary", "arbitrary", "arbitrary")),
        name="chunk_causal_attention",
    )(q, k, vt)
    return out[:, :lq]


def _pad_hist(hist):
    return jnp.pad(hist, ((0, 0), (SUBLANES - hist.shape[1], 0), (0, 0)))


def _rot_half_cols(w):
    w1, w2 = jnp.split(w, 2, axis=-1)
    return jnp.concatenate([-w2, w1], axis=-1)


def _pack_weights(p):
    d = D_MODEL
    pk = {}
    w_in = p['gdn_w_in'][0]
    pk['gdn_w_main'] = w_in[:, :GDN_CONV_CH + GDN_V].astype(BF16)
    pk['gdn_w_ba'] = jnp.pad(w_in[:, GDN_CONV_CH + GDN_V:], ((0, 0), (0, LANES - 2 * GDN_HEADS))).astype(BF16)
    pad_gate = lambda a: jnp.pad(a.reshape(1, GDN_HEADS), ((0, 0), (GDN_HEADS, LANES - 2 * GDN_HEADS)))
    pk['gdn_alog'] = pad_gate(p['gdn_a_log'][0])
    pk['gdn_dtb'] = pad_gate(p['gdn_dt_bias'][0])
    pk['gdn_conv_w'] = p['gdn_conv_w'][0]
    pk['gdn_norm_g'] = p['gdn_norm_g'][0].reshape(1, GDN_DV)
    pk['gdn_w_out'] = p['gdn_w_out'][0].astype(BF16)
    wkv = p['kv_w_down']
    pk['kv_w_down'] = jnp.concatenate([wkv, _rot_half_cols(wkv[:, MLA_KV_LORA:])], axis=-1).astype(BF16)
    pk['kv_norm_g'] = p['kv_norm_g'].reshape(1, MLA_KV_LORA)
    w_up = p['kv_w_up'].reshape(MLA_KV_LORA, MLA_HEADS, MLA_NOPE + MLA_V)
    pk['kv_w_k'] = w_up[..., :MLA_NOPE].reshape(MLA_KV_LORA, MLA_HEADS * MLA_NOPE).astype(BF16)
    pk['kv_w_vt'] = jnp.transpose(w_up[..., MLA_NOPE:], (1, 2, 0)).astype(BF16)
    pk['mla_w_dq'] = p['mla_w_dq'][0].astype(BF16)
    pk['mla_q_norm_g'] = p['mla_q_norm_g'][0].reshape(1, MLA_Q_LORA)
    wuq = p['mla_w_uq'][0].reshape(MLA_Q_LORA, MLA_HEADS, MLA_NOPE + MLA_ROPE)
    wuq = jnp.concatenate([wuq, _rot_half_cols(wuq[..., MLA_NOPE:])], axis=-1)
    pk['mla_w_uq'] = wuq.reshape(MLA_Q_LORA, MLA_HEADS * HEAD_PAD).astype(BF16)
    pk['mla_w_out'] = p['mla_w_out'][0].astype(BF16)
    pk['ffn_w_in'] = p['ffn_w_in'].astype(BF16)
    pk['ffn_b_in'] = p['ffn_b_in'].reshape(DEPTH, 1, 2 * D_FF)
    pk['ffn_conv_w'] = p['ffn_conv_w']
    pk['ffn_conv_b'] = p['ffn_conv_b'].reshape(DEPTH, 1, 2 * D_FF)
    pk['ffn_w_down'] = p['ffn_w_down'].astype(BF16)
    pk['ln_g'] = p['ln_g'].reshape(2 * DEPTH, 1, d)
    pk['ln_b'] = p['ln_b'].reshape(2 * DEPTH, 1, d)
    return pk


def _rope_table(past, l):
    inv = 1.0 / (ROPE_THETA ** (jnp.arange(0, MLA_ROPE, 2, dtype=F32) / MLA_ROPE))
    ang = (past + jnp.arange(l, dtype=jnp.int32)).astype(F32)[:, None] * inv[None, :]
    ang = jnp.concatenate([ang, ang], axis=-1)
    return jnp.concatenate([jnp.cos(ang), jnp.sin(ang)], axis=-1)


def _block_tail(o, z, norm_g, w_out, x, mods, hist, pk, layer):
    y, hist8 = _mix_ffn(o, z, norm_g, x, mods[2 * layer], w_out, pk['ln_g'][2 * layer], pk['ln_b'][2 * layer],
                        mods[2 * layer + 1], pk['ffn_w_in'], pk['ffn_b_in'][layer],
                        pk['ffn_conv_w'][layer], pk['ffn_conv_b'][layer], _pad_hist(hist),
                        pk['ffn_w_down'], pk['ln_g'][2 * layer + 1], pk['ln_b'][2 * layer + 1], layer)
    return y, hist8[:, SUBLANES - (FFN_CONV - 1):]


def _trunk(x, mods, gdn_state, gdn_conv, ffn_conv, ckv_past, kpe_past, pk):
    b, l, _ = x.shape
    past = kpe_past.shape[1]
    c_len = CHUNK if l % CHUNK == 0 else l
    n = l // c_len

    q, k, v, z, gb, hist8 = _gdn_in(x, mods[0], pk['gdn_w_main'], pk['gdn_w_ba'], _pad_hist(gdn_conv[:, 0]),
                                    pk['gdn_conv_w'], pk['gdn_alog'], pk['gdn_dtb'], c_len)
    col = gb.reshape(b, n, c_len, 2 * GDN_HEADS)
    row = jnp.swapaxes(col, 2, 3)
    o, s_end = _gdn_chunk(q, k, v, col, row, gdn_state[:, 0], c_len)
    gdn_conv_out = hist8[:, None, SUBLANES - (GDN_CONV - 1):]
    x, fh0 = _block_tail(o, z, pk['gdn_norm_g'], pk['gdn_w_out'], x, mods, ffn_conv[:, 0], pk, 0)

    proj = _mla_proj(x, mods[2], pk['kv_w_down'], pk['kv_norm_g'], _rope_table(past, l),
                     pk['mla_w_dq'], pk['mla_q_norm_g'], pk['mla_w_uq'], pk['kv_w_k'], pk['kv_w_vt'],
                     with_kv=(past == 0))
    ckv_new, kpe_new, qm = proj[:3]
    if past == 0:
        km, vm = proj[3:]
    else:
        ckv_all = jnp.concatenate([ckv_past, ckv_new], axis=1)
        kpe_all = jnp.concatenate([kpe_past, kpe_new], axis=1)
        kpe_pad = jnp.pad(kpe_all, ((0, 0), (0, 0), (0, LANES - MLA_ROPE)))
        km, vm = _kv_up(ckv_all, kpe_pad, pk['kv_w_k'], pk['kv_w_vt'])
    o = _attention(qm, km, vm, past)
    x, fh1 = _block_tail(o, None, None, pk['mla_w_out'], x, mods, ffn_conv[:, 1], pk, 1)

    return (x, s_end[:, None], gdn_conv_out, jnp.stack([fh0, fh1], axis=1), ckv_new, kpe_new)


def kernel(x_prompt, x_sample, c_prompt, c_sample, state_gdn, state_gdn_conv, state_ffn_conv, cache_ckv, cache_kpe, ada_w, ada_b, ln_g, ln_b, gdn_w_in, gdn_conv_w, gdn_a_log, gdn_dt_bias, gdn_norm_g, gdn_w_out, kv_w_down, kv_norm_g, kv_w_up, mla_w_dq, mla_q_norm_g, mla_w_uq, mla_w_out, ffn_w_in, ffn_b_in, ffn_conv_w, ffn_conv_b, ffn_w_down):
    p = {'ln_g': ln_g, 'ln_b': ln_b, 'gdn_w_in': gdn_w_in, 'gdn_conv_w': gdn_conv_w, 'gdn_a_log': gdn_a_log,
         'gdn_dt_bias': gdn_dt_bias, 'gdn_norm_g': gdn_norm_g, 'gdn_w_out': gdn_w_out,
         'kv_w_down': kv_w_down, 'kv_norm_g': kv_norm_g, 'kv_w_up': kv_w_up,
         'mla_w_dq': mla_w_dq, 'mla_q_norm_g': mla_q_norm_g, 'mla_w_uq': mla_w_uq, 'mla_w_out': mla_w_out,
         'ffn_w_in': ffn_w_in, 'ffn_b_in': ffn_b_in, 'ffn_conv_w': ffn_conv_w,
         'ffn_conv_b': ffn_conv_b, 'ffn_w_down': ffn_w_down}
    pk = _pack_weights(p)
    bp = x_prompt.shape[0]
    mods = _ada_terms(jnp.concatenate([c_prompt, c_sample], axis=0), ada_w, ada_b)[:, :, None, :]
    zeros_like_b = lambda a: jnp.zeros((bp,) + a.shape[1:], a.dtype)
    out_p = _trunk(x_prompt, mods[:, :bp], zeros_like_b(state_gdn), zeros_like_b(state_gdn_conv),
                   zeros_like_b(state_ffn_conv), jnp.zeros((bp, 0, MLA_KV_LORA), cache_ckv.dtype),
                   jnp.zeros((bp, 0, MLA_ROPE), cache_kpe.dtype), pk)
    out_s = _trunk(x_sample, mods[:, bp:], state_gdn, state_gdn_conv, state_ffn_conv, cache_ckv, cache_kpe, pk)
    return (out_p[0], out_s[0]) + out_p[1:] + out_s[1:]
```

```python
import functools

import jax
import jax.numpy as jnp
from jax import lax
from jax.experimental import pallas as pl
from jax.experimental.pallas import tpu as pltpu

F32 = jnp.float32
BF16 = jnp.bfloat16

D_MODEL = 1024
DEPTH = 2
CHUNK = 64
ALPHA = (2.0 * DEPTH) ** 0.25
LN_EPS = 1e-5
RMS_EPS = 1e-6
GDN_HEADS = 8
GDN_DK = 128
GDN_DV = 128
GDN_CONV = 4
GDN_QK = GDN_HEADS * GDN_DK
GDN_V = GDN_HEADS * GDN_DV
GDN_CONV_CH = 2 * GDN_QK + GDN_V
MLA_HEADS = 8
MLA_NOPE = 128
MLA_ROPE = 64
MLA_V = 128
MLA_KV_LORA = 256
MLA_Q_LORA = 384
ROPE_THETA = 10000.0
MLA_SCALE = (MLA_NOPE + MLA_ROPE) ** -0.5
Q_PRESCALE = MLA_SCALE * 1.4426950408889634
D_FF = 2816
FFN_CONV = 3

LANES = 128
SUBLANES = 8
MXU_DIM = 256
HEAD_PAD = 256
VMEM_LIMIT = 56 * 1024 * 1024
NEG_BIG = -1e30

NN = ((1,), (0,))
NT = ((1,), (1,))
TN = ((0,), (0,))


def _row_tile(n, cap):
    if n <= cap:
        return n
    for t in range(cap, 15, -1):
        if n % t == 0 and t % 16 == 0:
            return t
    raise ValueError(f"no row tile for {n}")


def _split3(a):
    hi = a.astype(BF16)
    r = a - hi.astype(F32)
    mid = r.astype(BF16)
    lo = (r - mid.astype(F32)).astype(BF16)
    return hi, mid, lo


def _dg(a, b, dims=NN):
    return lax.dot_general(a, b, (dims, ((), ())), preferred_element_type=F32)


def _dot_hp(a, b, dims=NN):
    a0, a1 = a.astype(BF16), (a - a.astype(BF16).astype(F32)).astype(BF16)
    b0, b1 = b.astype(BF16), (b - b.astype(BF16).astype(F32)).astype(BF16)
    return _dg(a0, b0, dims) + (_dg(a0, b1, dims) + _dg(a1, b0, dims))


def _sigmoid(x):
    return 1.0 / (1.0 + jnp.exp(-x))


def _silu(x):
    return x * _sigmoid(x)


def _layer_norm(y, g, b):
    mu = jnp.mean(y, axis=-1, keepdims=True)
    yc = y - mu
    var = jnp.mean(yc * yc, axis=-1, keepdims=True)
    return yc * lax.rsqrt(var + LN_EPS) * g + b


def _const_spec(shape):
    nd = len(shape)
    return pl.BlockSpec(shape, lambda *_: (0,) * nd, pipeline_mode=pl.Buffered(1))


def _shift_rows(u, carry8, j):
    rows, width = u.shape
    groups = rows // SUBLANES
    rot = pltpu.roll(u.reshape(groups, SUBLANES, width), j, axis=1)
    above = jnp.concatenate([pltpu.roll(carry8, j, axis=0)[None], rot[:groups - 1]], axis=0)
    sub = lax.broadcasted_iota(jnp.int32, rot.shape, 1)
    return jnp.where(sub < j, above, rot).reshape(rows, width)


def _params(sem):
    return pltpu.CompilerParams(dimension_semantics=sem, vmem_limit_bytes=VMEM_LIMIT)


def _ada_kernel(c_ref, w_ref, b_ref, o_ref):
    s = _silu(c_ref[...])
    o_ref[0] = _dot_hp(s, w_ref[0]) + b_ref[0]


def _ada_terms(c_all, ada_w, ada_b):
    n_sub, d, n3 = ada_w.shape
    bc = c_all.shape[0]
    tn = 768
    return pl.pallas_call(
        _ada_kernel,
        grid=(n_sub, n3 // tn),
        in_specs=[
            pl.BlockSpec((bc, d), lambda i, j: (0, 0)),
            pl.BlockSpec((1, d, tn), lambda i, j: (i, 0, j)),
            pl.BlockSpec((1, 1, tn), lambda i, j: (i, 0, j)),
        ],
        out_specs=pl.BlockSpec((1, bc, tn), lambda i, j: (i, 0, j)),
        out_shape=jax.ShapeDtypeStruct((n_sub, bc, n3), F32),
        compiler_params=_params(("arbitrary", "arbitrary")),
        name="ada_terms",
    )(c_all, ada_w, ada_b.reshape(n_sub, 1, n3))


GDN_IN_TILE = 256


def _gdn_in_kernel(x_ref, mod_ref, w_ref, wba_ref, hist_ref, cw_ref, alog_ref, dtb_ref, tri_ref,
                   q_ref, k_ref, v_ref, z_ref, gb_ref, hist_out_ref, carry, *, tm):
    t = pl.program_id(1)

    @pl.when(t == 0)
    def _():
        carry[...] = hist_ref[0]

    shift = mod_ref[0, :, 0:D_MODEL]
    scale = mod_ref[0, :, D_MODEL:2 * D_MODEL]
    hin = (x_ref[0] * (1.0 + scale) + shift).astype(BF16)

    outs = (q_ref, k_ref, v_ref)

    def conv_tile(sec, c):
        lo = sec * GDN_QK + c * GDN_IN_TILE
        cols = slice(lo, lo + GDN_IN_TILE)
        ocols = slice(c * GDN_IN_TILE, (c + 1) * GDN_IN_TILE)
        u = jnp.dot(hin, w_ref[:, cols], preferred_element_type=F32)
        c8 = carry[:, cols]
        acc = u * cw_ref[GDN_CONV - 1:GDN_CONV, cols]
        for j in range(1, GDN_CONV):
            acc = acc + _shift_rows(u, c8, j) * cw_ref[GDN_CONV - 1 - j:GDN_CONV - j, cols]
        carry[:, cols] = u[tm - SUBLANES:tm]
        s = _silu(acc)
        if sec == 2:
            v_ref[0, :, ocols] = s.astype(BF16)
        else:
            post = GDN_DK ** -0.5 if sec == 0 else 1.0
            parts = []
            for h in range(GDN_IN_TILE // GDN_DK):
                sh = s[:, h * GDN_DK:(h + 1) * GDN_DK]
                inv = lax.rsqrt(jnp.sum(sh * sh, axis=-1, keepdims=True) + RMS_EPS)
                parts.append(sh * (inv * post))
            outs[sec][0, :, ocols] = jnp.concatenate(parts, axis=-1).astype(BF16)

    def z_tile(c):
        ocols = slice(c * GDN_IN_TILE, (c + 1) * GDN_IN_TILE)
        cols = slice(GDN_CONV_CH + c * GDN_IN_TILE, GDN_CONV_CH + (c + 1) * GDN_IN_TILE)
        z_ref[0, :, ocols] = jnp.dot(hin, w_ref[:, cols], preferred_element_type=F32).astype(BF16)

    def gates():
        ba = jnp.dot(hin, wba_ref[...], preferred_element_type=F32)
        beta = _sigmoid(ba)
        xs = ba + dtb_ref[...]
        softplus = jnp.maximum(xs, 0.0) + jnp.log(1.0 + jnp.exp(-jnp.abs(xs)))
        g = -jnp.exp(alog_ref[...]) * softplus
        tri = tri_ref[...]
        g0, g1, g2 = _split3(g)
        gsum = (jnp.dot(tri, g0, preferred_element_type=F32)
                + jnp.dot(tri, g1, preferred_element_type=F32)
                + jnp.dot(tri, g2, preferred_element_type=F32))
        lane = lax.broadcasted_iota(jnp.int32, (tm, LANES), 1)
        gb_ref[0] = jnp.where(lane < GDN_HEADS, beta, gsum)[:, :2 * GDN_HEADS]

    for c in range(GDN_QK // GDN_IN_TILE):
        conv_tile(0, c)
        z_tile(c)
        conv_tile(1, c)
        if c == 0:
            gates()
        conv_tile(2, c)
    hist_out_ref[0] = carry[...]


def _gdn_in(x, mod, w_main, w_ba, hist8, conv_w, alog_row, dtb_row, c_len):
    b, l, d = x.shape
    tm = _row_tile(l, 512)
    assert tm % c_len == 0
    idx = jnp.arange(tm, dtype=jnp.int32)
    tri = ((idx[:, None] >= idx[None, :]) & ((idx[:, None] // c_len) == (idx[None, :] // c_len))).astype(BF16)
    tok = lambda w: pl.BlockSpec((1, tm, w), lambda i, t: (i, t, 0))
    return pl.pallas_call(
        functools.partial(_gdn_in_kernel, tm=tm),
        grid=(b, l // tm),
        in_specs=[
            tok(d),
            pl.BlockSpec((1, 1, 3 * d), lambda i, t: (i, 0, 0)),
            _const_spec(w_main.shape),
            _const_spec(w_ba.shape),
            pl.BlockSpec((1, SUBLANES, GDN_CONV_CH), lambda i, t: (i, 0, 0)),
            _const_spec(conv_w.shape),
            _const_spec(alog_row.shape),
            _const_spec(dtb_row.shape),
            _const_spec(tri.shape),
        ],
        out_specs=[tok(GDN_QK), tok(GDN_QK), tok(GDN_V), tok(GDN_V), tok(2 * GDN_HEADS),
                   pl.BlockSpec((1, SUBLANES, GDN_CONV_CH), lambda i, t: (i, 0, 0))],
        out_shape=[
            jax.ShapeDtypeStruct((b, l, GDN_QK), BF16),
            jax.ShapeDtypeStruct((b, l, GDN_QK), BF16),
            jax.ShapeDtypeStruct((b, l, GDN_V), BF16),
            jax.ShapeDtypeStruct((b, l, GDN_V), BF16),
            jax.ShapeDtypeStruct((b, l, 2 * GDN_HEADS), F32),
            jax.ShapeDtypeStruct((b, SUBLANES, GDN_CONV_CH), F32),
        ],
        scratch_shapes=[pltpu.VMEM((SUBLANES, GDN_CONV_CH), F32)],
        compiler_params=_params(("arbitrary", "arbitrary")),
        name="gdn_in",
    )(x, mod, w_main, w_ba, hist8, conv_w, alog_row, dtb_row, tri)


GDN_GROUP = 4


def _bf(xs):
    return [x.astype(BF16) for x in xs]


def _gdn_chunk_kernel(q_ref, k_ref, v_ref, col_ref, row_ref, s0_ref, o_ref, s_ref, *, c_len, group):
    @pl.when(pl.program_id(1) == 0)
    def _():
        s_ref[...] = s0_ref[...]

    ri = lax.broadcasted_iota(jnp.int32, (c_len, c_len), 0)
    ci = lax.broadcasted_iota(jnp.int32, (c_len, c_len), 1)
    incl = ri >= ci
    strict = ri > ci
    n_sq = c_len.bit_length() - 2
    heads = range(GDN_HEADS)
    items = [(g, h) for g in range(group) for h in heads]
    each = range(len(items))
    rows = lambda g: slice(g * c_len, (g + 1) * c_len)
    lanes = lambda h: slice(h * GDN_DK, (h + 1) * GDN_DK)

    col = [col_ref[0, g] for g in range(group)]
    row = [row_ref[0, g] for g in range(group)]
    qb = [q_ref[0, rows(g), lanes(h)] for g, h in items]
    kb = [k_ref[0, rows(g), lanes(h)] for g, h in items]
    k = [x.astype(F32) for x in kb]
    v = [v_ref[0, rows(g), lanes(h)].astype(F32) for g, h in items]
    beta = [col[g][:, h:h + 1] for g, h in items]
    gc = [col[g][:, GDN_HEADS + h:GDN_HEADS + h + 1] for g, h in items]
    gr = [row[g][GDN_HEADS + h:GDN_HEADS + h + 1, :] for g, h in items]
    gam = [jnp.where(incl, jnp.exp(jnp.where(incl, gc[i] - gr[i], 0.0)), 0.0) for i in each]
    eg = [jnp.exp(gc[i]) for i in each]
    qkk = [_dg(jnp.concatenate([qb[i], kb[i]], axis=0), kb[i], NT) for i in each]
    p = _bf([jnp.where(strict, -(beta[i] * qkk[i][c_len:] * gam[i]), 0.0) for i in each])
    sol = [jnp.concatenate([beta[i] * v[i], beta[i] * k[i] * eg[i]], axis=-1) for i in each]
    solb = _bf(sol)
    sol = [sol[i] + _dg(p[i], solb[i]) for i in each]
    for _ in range(n_sq):
        p = _bf([_dg(p[i], p[i]) for i in each])
        solb = _bf(sol)
        sol = [sol[i] + _dg(p[i], solb[i]) for i in each]
    u = [sol[i][:, :GDN_DV] for i in each]
    wq = [jnp.concatenate([sol[i][:, GDN_DV:].astype(BF16), (qb[i].astype(F32) * eg[i]).astype(BF16)], axis=0)
          for i in each]
    att = _bf([qkk[i][:c_len] * gam[i] for i in each])
    g_last = [gc[i][c_len - 1:c_len, :] for i in each]
    k_dec = _bf([k[i] * jnp.exp(g_last[i] - gc[i]) for i in each])
    decay = [jnp.exp(g_last[i]) for i in each]

    s_cur = [s_ref[0, h] for h in heads]
    for g in range(group):
        it = [g * GDN_HEADS + h for h in heads]
        sb = _bf(s_cur)
        ws = [_dg(wq[it[h]], sb[h]) for h in heads]
        vb = _bf([u[it[h]] - ws[h][:c_len] for h in heads])
        o = [ws[h][c_len:] + _dg(att[it[h]], vb[h]) for h in heads]
        s_cur = [decay[it[h]] * s_cur[h] + _dg(k_dec[it[h]], vb[h], TN) for h in heads]
        o_ref[0, rows(g), :] = jnp.concatenate(o, axis=-1).astype(BF16)
    for h in heads:
        s_ref[0, h] = s_cur[h]


def _gdn_chunk(q, k, v, col, row, s0, c_len):
    b, l, _ = q.shape
    n = l // c_len
    group = GDN_GROUP if n % GDN_GROUP == 0 else 1
    rows = c_len * group
    tok = pl.BlockSpec((1, rows, GDN_QK), lambda i, t: (i, t, 0))
    st = pl.BlockSpec((1, GDN_HEADS, GDN_DK, GDN_DV), lambda i, t: (i, 0, 0, 0))
    return pl.pallas_call(
        functools.partial(_gdn_chunk_kernel, c_len=c_len, group=group),
        grid=(b, n // group),
        in_specs=[tok, tok, tok,
                  pl.BlockSpec((1, group, c_len, 2 * GDN_HEADS), lambda i, t: (i, t, 0, 0)),
                  pl.BlockSpec((1, group, 2 * GDN_HEADS, c_len), lambda i, t: (i, t, 0, 0)),
                  st],
        out_specs=[tok, st],
        out_shape=[jax.ShapeDtypeStruct((b, l, GDN_V), BF16),
                   jax.ShapeDtypeStruct(s0.shape, F32)],
        compiler_params=_params(("arbitrary", "arbitrary")),
        name="gdn_chunk",
    )(q, k, v, col, row, s0)


FFN_TILE = MXU_DIM
FFN_ROWS = 512


def _mix_ffn_kernel(*refs, gated, tm):
    if gated:
        o_ref, z_ref, ng_ref = refs[:3]
        refs = refs[3:]
    else:
        o_ref = refs[0]
        refs = refs[1:]
    (x_ref, mmod_ref, wo_ref, mlg_ref, mlb_ref, fmod_ref, win_ref, bin_ref, cw_ref, cb_ref, hist_ref, wdn_ref,
     lg_ref, lb_ref, y_ref, hist_out_ref, carry, x1_ref, hbuf, fbuf) = refs
    t = pl.program_id(1)

    @pl.when(t == 0)
    def _():
        carry[...] = hist_ref[0]

    if gated:
        for h in range(GDN_HEADS):
            hs = slice(h * GDN_DV, (h + 1) * GDN_DV)
            oh = o_ref[0, :, hs].astype(F32)
            r = oh * lax.rsqrt(jnp.mean(oh * oh, axis=-1, keepdims=True) + RMS_EPS) * ng_ref[...]
            hbuf[:, hs] = (r * _silu(z_ref[0, :, hs].astype(F32))).astype(BF16)
        a = hbuf[...]
    else:
        a = o_ref[0]
    mix = jnp.dot(a, wo_ref[...], preferred_element_type=F32)
    mgate = mmod_ref[0, :, 2 * D_MODEL:3 * D_MODEL]
    x1_ref[...] = _layer_norm(ALPHA * x_ref[0] + (1.0 + mgate) * mix, mlg_ref[...], mlb_ref[...])

    shift = fmod_ref[0, :, 0:D_MODEL]
    scale = fmod_ref[0, :, D_MODEL:2 * D_MODEL]
    gate = fmod_ref[0, :, 2 * D_MODEL:3 * D_MODEL]
    hbuf[...] = (x1_ref[...] * (1.0 + scale) + shift).astype(BF16)

    def conv_half(lo):
        cols = slice(lo, lo + FFN_TILE)
        u = jnp.dot(hbuf[...], win_ref[:, cols], preferred_element_type=F32) + bin_ref[:, cols]
        c8 = carry[:, cols]
        out = u * cw_ref[FFN_CONV - 1:FFN_CONV, cols] + cb_ref[:, cols]
        for j in range(1, FFN_CONV):
            out = out + _shift_rows(u, c8, j) * cw_ref[FFN_CONV - 1 - j:FFN_CONV - j, cols]
        carry[:, cols] = u[tm - SUBLANES:tm]
        return out

    for i in range(D_FF // FFN_TILE):
        lo = i * FFN_TILE
        ua = conv_half(lo)
        ub = conv_half(lo + D_FF)
        fbuf[:, lo:lo + FFN_TILE] = (_silu(ua) * ub).astype(BF16)

    hist_out_ref[0] = carry[...]
    f = jnp.dot(fbuf[...], wdn_ref[...], preferred_element_type=F32)
    y = ALPHA * x1_ref[...] + (1.0 + gate) * f
    y_ref[0] = _layer_norm(y, lg_ref[...], lb_ref[...])


def _layer_spec(stacked, layer):
    nd = stacked.ndim - 1
    return pl.BlockSpec((None,) + stacked.shape[1:], lambda *_: (layer,) + (0,) * nd, pipeline_mode=pl.Buffered(1))


def _mix_ffn(o, z, norm_g, x, mix_mod, w_out, mix_ln_g, mix_ln_b,
             ffn_mod, w_in_all, b_in, conv_w, conv_b, hist8, w_down_all, ln_g, ln_b, layer):
    b, l, d = x.shape
    tm = _row_tile(l, FFN_ROWS)
    gated = z is not None
    tok = pl.BlockSpec((1, tm, d), lambda i, t: (i, t, 0))
    modspec = pl.BlockSpec((1, 1, 3 * d), lambda i, t: (i, 0, 0))
    hspec = pl.BlockSpec((1, SUBLANES, 2 * D_FF), lambda i, t: (i, 0, 0))
    row = _const_spec((1, d))
    head_args, head_specs = ((o, z, norm_g), [tok, tok, _const_spec(norm_g.shape)]) if gated else ((o,), [tok])
    return pl.pallas_call(
        functools.partial(_mix_ffn_kernel, gated=gated, tm=tm),
        grid=(b, l // tm),
        in_specs=head_specs + [tok, modspec, _const_spec(w_out.shape), row, row,
                               modspec, _layer_spec(w_in_all, layer), _const_spec(b_in.shape),
                               _const_spec(conv_w.shape), _const_spec(conv_b.shape), hspec,
                               _layer_spec(w_down_all, layer), row, row],
        out_specs=[tok, hspec],
        out_shape=[jax.ShapeDtypeStruct((b, l, d), F32),
                   jax.ShapeDtypeStruct((b, SUBLANES, 2 * D_FF), F32)],
        scratch_shapes=[pltpu.VMEM((SUBLANES, 2 * D_FF), F32), pltpu.VMEM((tm, d), F32),
                        pltpu.VMEM((tm, d), BF16), pltpu.VMEM((tm, D_FF), BF16)],
        compiler_params=_params(("arbitrary", "arbitrary")),
        name="mix_ffn_gated" if gated else "mix_ffn",
    )(*head_args, x, mix_mod, w_out, mix_ln_g, mix_ln_b,
      ffn_mod, w_in_all, b_in, conv_w, conv_b, hist8, w_down_all, ln_g, ln_b)


def _rope_pair(pair, cs):
    prod = pair * cs
    return prod + pltpu.roll(prod, MLA_ROPE, axis=1)


def _mla_proj_kernel(x_ref, mod_ref, wkv_ref, kvg_ref, cs_ref, wdq_ref, qg_ref, wuq_ref, *rest, with_kv):
    if with_kv:
        wk_ref, wvt_ref, ckv_ref, kpe_ref, q_ref, k_ref, vt_ref = rest
    else:
        ckv_ref, kpe_ref, q_ref = rest
    x = x_ref[0]
    cs = cs_ref[...]
    lane = lax.broadcasted_iota(jnp.int32, (x.shape[0], LANES), 1)
    kv = jnp.dot(x.astype(BF16), wkv_ref[...], preferred_element_type=F32)
    lat = kv[:, :MLA_KV_LORA]
    ckv = lat * lax.rsqrt(jnp.mean(lat * lat, axis=-1, keepdims=True) + RMS_EPS) * kvg_ref[...]
    kpe = _rope_pair(kv[:, MLA_KV_LORA:], cs)
    ckv_ref[0] = ckv
    kpe_ref[0] = kpe[:, :MLA_ROPE]
    if with_kv:
        _kv_heads(ckv.astype(BF16), jnp.where(lane < MLA_ROPE, kpe, 0.0).astype(BF16), wk_ref, wvt_ref, k_ref, vt_ref)

    shift = mod_ref[0, :, 0:D_MODEL]
    scale = mod_ref[0, :, D_MODEL:2 * D_MODEL]
    hin = (x * (1.0 + scale) + shift).astype(BF16)
    qd = jnp.dot(hin, wdq_ref[...], preferred_element_type=F32)
    qd = qd * lax.rsqrt(jnp.mean(qd * qd, axis=-1, keepdims=True) + RMS_EPS) * qg_ref[...]
    qd = qd.astype(BF16)
    for h in range(MLA_HEADS):
        qh = jnp.dot(qd, wuq_ref[:, h * HEAD_PAD:(h + 1) * HEAD_PAD], preferred_element_type=F32)
        q_ref[0, :, h * HEAD_PAD:h * HEAD_PAD + MLA_NOPE] = (qh[:, :MLA_NOPE] * Q_PRESCALE).astype(BF16)
        pe = jnp.where(lane < MLA_ROPE, _rope_pair(qh[:, MLA_NOPE:], cs) * Q_PRESCALE, 0.0)
        q_ref[0, :, h * HEAD_PAD + MLA_NOPE:(h + 1) * HEAD_PAD] = pe.astype(BF16)


def _mla_proj(x, mod, w_kv, kv_g, cossin, w_dq, q_g, w_uq, w_k, w_vt, with_kv):
    b, l, d = x.shape
    tm = _row_tile(l, 512)
    tok = lambda w: pl.BlockSpec((1, tm, w), lambda i, t: (i, t, 0))
    in_specs = [tok(d), pl.BlockSpec((1, 1, 3 * d), lambda i, t: (i, 0, 0)),
                _const_spec(w_kv.shape), _const_spec(kv_g.shape),
                pl.BlockSpec((tm, LANES), lambda i, t: (t, 0)),
                _const_spec(w_dq.shape), _const_spec(q_g.shape), _const_spec(w_uq.shape)]
    out_specs = [tok(MLA_KV_LORA), tok(MLA_ROPE), tok(MLA_HEADS * HEAD_PAD)]
    out_shape = [jax.ShapeDtypeStruct((b, l, MLA_KV_LORA), F32),
                 jax.ShapeDtypeStruct((b, l, MLA_ROPE), F32),
                 jax.ShapeDtypeStruct((b, l, MLA_HEADS * HEAD_PAD), BF16)]
    args = (x, mod, w_kv, kv_g, cossin, w_dq, q_g, w_uq)
    if with_kv:
        in_specs += [_const_spec(w_k.shape), _const_spec(w_vt.shape)]
        out_specs += [tok(MLA_HEADS * HEAD_PAD), pl.BlockSpec((1, MLA_HEADS * MLA_V, tm), lambda i, t: (i, 0, t))]
        out_shape += [jax.ShapeDtypeStruct((b, l, MLA_HEADS * HEAD_PAD), BF16),
                      jax.ShapeDtypeStruct((b, MLA_HEADS * MLA_V, l), BF16)]
        args += (w_k, w_vt)
    return pl.pallas_call(
        functools.partial(_mla_proj_kernel, with_kv=with_kv),
        grid=(b, l // tm),
        in_specs=in_specs,
        out_specs=out_specs,
        out_shape=out_shape,
        compiler_params=_params(("arbitrary", "arbitrary")),
        name="mla_proj_kv" if with_kv else "mla_proj",
    )(*args)


def _kv_heads(lat, kpe, wk_ref, wvt_ref, k_ref, vt_ref):
    for h in range(MLA_HEADS):
        kn = jnp.dot(lat, wk_ref[:, h * MLA_NOPE:(h + 1) * MLA_NOPE], preferred_element_type=F32)
        k_ref[0, :, h * HEAD_PAD:h * HEAD_PAD + MLA_NOPE] = kn.astype(BF16)
        k_ref[0, :, h * HEAD_PAD + MLA_NOPE:(h + 1) * HEAD_PAD] = kpe
        vt_ref[0, h * MLA_V:(h + 1) * MLA_V, :] = _dg(wvt_ref[h], lat, NT).astype(BF16)


def _kv_up_kernel(ckv_ref, kpe_ref, wk_ref, wvt_ref, k_ref, vt_ref):
    _kv_heads(ckv_ref[0].astype(BF16), kpe_ref[0].astype(BF16), wk_ref, wvt_ref, k_ref, vt_ref)


def _kv_up(ckv_all, kpe_pad, w_k, w_vt):
    b, lk, _ = ckv_all.shape
    tm = ATTN_TK if lk % ATTN_TK == 0 else lk
    tok = lambda w: pl.BlockSpec((1, tm, w), lambda i, t: (i, t, 0))
    return pl.pallas_call(
        _kv_up_kernel,
        grid=(b, lk // tm),
        in_specs=[tok(MLA_KV_LORA), tok(LANES), _const_spec(w_k.shape), _const_spec(w_vt.shape)],
        out_specs=[tok(MLA_HEADS * HEAD_PAD),
                   pl.BlockSpec((1, MLA_HEADS * MLA_V, tm), lambda i, t: (i, 0, t))],
        out_shape=[jax.ShapeDtypeStruct((b, lk, MLA_HEADS * HEAD_PAD), BF16),
                   jax.ShapeDtypeStruct((b, MLA_HEADS * MLA_V, lk), BF16)],
        compiler_params=_params(("arbitrary", "arbitrary")),
        name="kv_up",
    )(ckv_all, kpe_pad, w_k, w_vt)


ATTN_TQ = 512
ATTN_TK = 512
ATTN_HEADS_PER_STEP = 4
CHUNK_SHIFT = CHUNK.bit_length() - 1


def _chunk_end(pos):
    return (lax.shift_right_logical(pos, CHUNK_SHIFT) + 1) * CHUNK


def _attn_kernel(q_ref, k_ref, vt_ref, o_ref, acc_ref, m_ref, l_ref, sa_ref, sb_ref, *, tq, tk, past, lk, nh):
    i = pl.program_id(2)
    acc_ref[...] = jnp.zeros_like(acc_ref)
    m_ref[...] = jnp.full_like(m_ref, NEG_BIG)
    l_ref[...] = jnp.zeros_like(l_ref)
    q0 = past + i * tq
    n_full = lax.div(jnp.minimum(_chunk_end(q0), lk), tk)
    n_end = lax.div(jnp.minimum(_chunk_end(q0 + tq - 1), lk) + tk - 1, tk)
    n_blk = lk // tk
    heads = range(nh)
    ones = jnp.ones((2 * SUBLANES, tk), BF16)

    def key_start(j):
        return pl.multiple_of(jnp.minimum(j, n_blk - 1) * tk, tk) if lk > tk else 0

    def scores(j, dst):
        k0 = key_start(j)
        for h in heads:
            dst[h] = _dg(k_ref[0, pl.ds(k0, tk), h * HEAD_PAD:(h + 1) * HEAD_PAD],
                         q_ref[0, :, h * HEAD_PAD:(h + 1) * HEAD_PAD], NT)

    def step(j, src, dst, masked):
        if dst is not None:
            scores(j + 1, dst)
        k0 = key_start(j)
        st = [src[h] for h in heads]
        if masked:
            kc = lax.shift_right_logical(k0 + lax.broadcasted_iota(jnp.int32, (tk, tq), 0), CHUNK_SHIFT)
            qc = lax.shift_right_logical(q0 + lax.broadcasted_iota(jnp.int32, (tk, tq), 1), CHUNK_SHIFT)
            visible = kc <= qc
            st = [jnp.where(visible, s, NEG_BIG) for s in st]
        m_old = [m_ref[h] for h in heads]
        m_new = [jnp.maximum(m_old[h], jnp.max(st[h], axis=0, keepdims=True)) for h in heads]
        alpha = [jnp.exp2(m_old[h] - m_new[h]) for h in heads]
        pt = [jnp.exp2(st[h] - m_new[h]).astype(BF16) for h in heads]
        pv = [_dg(jnp.concatenate([vt_ref[0, h * MLA_V:(h + 1) * MLA_V, pl.ds(k0, tk)], ones], axis=0), pt[h])
              for h in heads]
        for h in heads:
            acc_ref[h] = alpha[h] * acc_ref[h] + pv[h][:MLA_V]
            m_ref[h] = m_new[h]
            l_ref[h] = alpha[h] * l_ref[h] + pv[h][MLA_V:MLA_V + 1]

    if n_blk == 1:
        scores(0, sa_ref)

        @pl.when(n_full == 1)
        def _():
            step(0, sa_ref, None, False)

        @pl.when(n_full == 0)
        def _():
            step(0, sa_ref, None, True)
    else:
        odd = lax.rem(n_full, 2)

        @pl.when(odd == 1)
        def _():
            scores(0, sb_ref)
            step(0, sb_ref, sa_ref, False)

        @pl.when(odd == 0)
        def _():
            scores(0, sa_ref)

        def pair(p, c):
            j = odd + 2 * p
            step(j, sa_ref, sb_ref, False)
            step(j + 1, sb_ref, sa_ref, False)
            return c

        lax.fori_loop(0, lax.div(n_full, 2), pair, 0)

        @pl.when(n_end > n_full)
        def _():
            step(n_full, sa_ref, None, True)

        def rest(j, c):
            scores(j, sa_ref)
            step(j, sa_ref, None, True)
            return c

        lax.fori_loop(n_full + 1, n_end, rest, 0)
    for h in heads:
        o_ref[0, :, h * MLA_V:(h + 1) * MLA_V] = (acc_ref[h] / l_ref[h]).T.astype(BF16)


def _attention(q, k, vt, past):
    b, lq, _ = q.shape
    lk = k.shape[1]
    lq_pad = -(-lq // LANES) * LANES
    if lq_pad != lq:
        q = jnp.pad(q, ((0, 0), (0, lq_pad - lq), (0, 0)))
    tq = _row_tile(lq_pad, ATTN_TQ)
    tk = ATTN_TK if lk % ATTN_TK == 0 else lk
    nh = ATTN_HEADS_PER_STEP
    kv_bytes = nh * lk * (HEAD_PAD + MLA_V) * jnp.dtype(BF16).itemsize
    kv_mode = pl.Buffered(2 if 2 * kv_bytes <= VMEM_LIMIT // 4 else 1)
    out = pl.pallas_call(
        functools.partial(_attn_kernel, tq=tq, tk=tk, past=past, lk=lk, nh=nh),
        grid=(b, MLA_HEADS // nh, lq_pad // tq),
        in_specs=[pl.BlockSpec((1, tq, nh * HEAD_PAD), lambda bi, h, i: (bi, i, h)),
                  pl.BlockSpec((1, lk, nh * HEAD_PAD), lambda bi, h, i: (bi, 0, h), pipeline_mode=kv_mode),
                  pl.BlockSpec((1, nh * MLA_V, lk), lambda bi, h, i: (bi, h, 0), pipeline_mode=kv_mode)],
        out_specs=pl.BlockSpec((1, tq, nh * MLA_V), lambda bi, h, i: (bi, i, h)),
        out_shape=jax.ShapeDtypeStruct((b, lq_pad, MLA_HEADS * MLA_V), BF16),
        scratch_shapes=[pltpu.VMEM((nh, MLA_V, tq), F32), pltpu.VMEM((nh, 1, tq), F32),
                        pltpu.VMEM((nh, 1, tq), F32), pltpu.VMEM((nh, tk, tq), F32),
                        pltpu.VMEM((nh, tk, tq), F32)],
        compiler_params=_params(("arbitrary", "arbitrary", "arbitrary")),
        name="chunk_causal_attention",
    )(q, k, vt)
    return out[:, :lq]


def _pad_hist(hist):
    return jnp.pad(hist, ((0, 0), (SUBLANES - hist.shape[1], 0), (0, 0)))


def _rot_half_cols(w):
    w1, w2 = jnp.split(w, 2, axis=-1)
    return jnp.concatenate([-w2, w1], axis=-1)


def _pack_weights(p):
    d = D_MODEL
    pk = {}
    w_in = p['gdn_w_in'][0]
    pk['gdn_w_main'] = w_in[:, :GDN_CONV_CH + GDN_V].astype(BF16)
    pk['gdn_w_ba'] = jnp.pad(w_in[:, GDN_CONV_CH + GDN_V:], ((0, 0), (0, LANES - 2 * GDN_HEADS))).astype(BF16)
    pad_gate = lambda a: jnp.pad(a.reshape(1, GDN_HEADS), ((0, 0), (GDN_HEADS, LANES - 2 * GDN_HEADS)))
    pk['gdn_alog'] = pad_gate(p['gdn_a_log'][0])
    pk['gdn_dtb'] = pad_gate(p['gdn_dt_bias'][0])
    pk['gdn_conv_w'] = p['gdn_conv_w'][0]
    pk['gdn_norm_g'] = p['gdn_norm_g'][0].reshape(1, GDN_DV)
    pk['gdn_w_out'] = p['gdn_w_out'][0].astype(BF16)
    wkv = p['kv_w_down']
    pk['kv_w_down'] = jnp.concatenate([wkv, _rot_half_cols(wkv[:, MLA_KV_LORA:])], axis=-1).astype(BF16)
    pk['kv_norm_g'] = p['kv_norm_g'].reshape(1, MLA_KV_LORA)
    w_up = p['kv_w_up'].reshape(MLA_KV_LORA, MLA_HEADS, MLA_NOPE + MLA_V)
    pk['kv_w_k'] = w_up[..., :MLA_NOPE].reshape(MLA_KV_LORA, MLA_HEADS * MLA_NOPE).astype(BF16)
    pk['kv_w_vt'] = jnp.transpose(w_up[..., MLA_NOPE:], (1, 2, 0)).astype(BF16)
    pk['mla_w_dq'] = p['mla_w_dq'][0].astype(BF16)
    pk['mla_q_norm_g'] = p['mla_q_norm_g'][0].reshape(1, MLA_Q_LORA)
    wuq = p['mla_w_uq'][0].reshape(MLA_Q_LORA, MLA_HEADS, MLA_NOPE + MLA_ROPE)
    wuq = jnp.concatenate([wuq, _rot_half_cols(wuq[..., MLA_NOPE:])], axis=-1)
    pk['mla_w_uq'] = wuq.reshape(MLA_Q_LORA, MLA_HEADS * HEAD_PAD).astype(BF16)
    pk['mla_w_out'] = p['mla_w_out'][0].astype(BF16)
    pk['ffn_w_in'] = p['ffn_w_in'].astype(BF16)
    pk['ffn_b_in'] = p['ffn_b_in'].reshape(DEPTH, 1, 2 * D_FF)
    pk['ffn_conv_w'] = p['ffn_conv_w']
    pk['ffn_conv_b'] = p['ffn_conv_b'].reshape(DEPTH, 1, 2 * D_FF)
    pk['ffn_w_down'] = p['ffn_w_down'].astype(BF16)
    pk['ln_g'] = p['ln_g'].reshape(2 * DEPTH, 1, d)
    pk['ln_b'] = p['ln_b'].reshape(2 * DEPTH, 1, d)
    return pk


def _rope_table(past, l):
    inv = 1.0 / (ROPE_THETA ** (jnp.arange(0, MLA_ROPE, 2, dtype=F32) / MLA_ROPE))
    ang = (past + jnp.arange(l, dtype=jnp.int32)).astype(F32)[:, None] * inv[None, :]
    ang = jnp.concatenate([ang, ang], axis=-1)
    return jnp.concatenate([jnp.cos(ang), jnp.sin(ang)], axis=-1)


def _block_tail(o, z, norm_g, w_out, x, mods, hist, pk, layer):
    y, hist8 = _mix_ffn(o, z, norm_g, x, mods[2 * layer], w_out, pk['ln_g'][2 * layer], pk['ln_b'][2 * layer],
                        mods[2 * layer + 1], pk['ffn_w_in'], pk['ffn_b_in'][layer],
                        pk['ffn_conv_w'][layer], pk['ffn_conv_b'][layer], _pad_hist(hist),
                        pk['ffn_w_down'], pk['ln_g'][2 * layer + 1], pk['ln_b'][2 * layer + 1], layer)
    return y, hist8[:, SUBLANES - (FFN_CONV - 1):]


def _trunk(x, mods, gdn_state, gdn_conv, ffn_conv, ckv_past, kpe_past, pk):
    b, l, _ = x.shape
    past = kpe_past.shape[1]
    c_len = CHUNK if l % CHUNK == 0 else l
    n = l // c_len

    q, k, v, z, gb, hist8 = _gdn_in(x, mods[0], pk['gdn_w_main'], pk['gdn_w_ba'], _pad_hist(gdn_conv[:, 0]),
                                    pk['gdn_conv_w'], pk['gdn_alog'], pk['gdn_dtb'], c_len)
    col = gb.reshape(b, n, c_len, 2 * GDN_HEADS)
    row = jnp.swapaxes(col, 2, 3)
    o, s_end = _gdn_chunk(q, k, v, col, row, gdn_state[:, 0], c_len)
    gdn_conv_out = hist8[:, None, SUBLANES - (GDN_CONV - 1):]
    x, fh0 = _block_tail(o, z, pk['gdn_norm_g'], pk['gdn_w_out'], x, mods, ffn_conv[:, 0], pk, 0)

    proj = _mla_proj(x, mods[2], pk['kv_w_down'], pk['kv_norm_g'], _rope_table(past, l),
                     pk['mla_w_dq'], pk['mla_q_norm_g'], pk['mla_w_uq'], pk['kv_w_k'], pk['kv_w_vt'],
                     with_kv=(past == 0))
    ckv_new, kpe_new, qm = proj[:3]
    if past == 0:
        km, vm = proj[3:]
    else:
        ckv_all = jnp.concatenate([ckv_past, ckv_new], axis=1)
        kpe_all = jnp.concatenate([kpe_past, kpe_new], axis=1)
        kpe_pad = jnp.pad(kpe_all, ((0, 0), (0, 0), (0, LANES - MLA_ROPE)))
        km, vm = _kv_up(ckv_all, kpe_pad, pk['kv_w_k'], pk['kv_w_vt'])
    o = _attention(qm, km, vm, past)
    x, fh1 = _block_tail(o, None, None, pk['mla_w_out'], x, mods, ffn_conv[:, 1], pk, 1)

    return (x, s_end[:, None], gdn_conv_out, jnp.stack([fh0, fh1], axis=1), ckv_new, kpe_new)


def kernel(x_prompt, x_sample, c_prompt, c_sample, state_gdn, state_gdn_conv, state_ffn_conv, cache_ckv, cache_kpe, ada_w, ada_b, ln_g, ln_b, gdn_w_in, gdn_conv_w, gdn_a_log, gdn_dt_bias, gdn_norm_g, gdn_w_out, kv_w_down, kv_norm_g, kv_w_up, mla_w_dq, mla_q_norm_g, mla_w_uq, mla_w_out, ffn_w_in, ffn_b_in, ffn_conv_w, ffn_conv_b, ffn_w_down):
    p = {'ln_g': ln_g, 'ln_b': ln_b, 'gdn_w_in': gdn_w_in, 'gdn_conv_w': gdn_conv_w, 'gdn_a_log': gdn_a_log,
         'gdn_dt_bias': gdn_dt_bias, 'gdn_norm_g': gdn_norm_g, 'gdn_w_out': gdn_w_out,
         'kv_w_down': kv_w_down, 'kv_norm_g': kv_norm_g, 'kv_w_up': kv_w_up,
         'mla_w_dq': mla_w_dq, 'mla_q_norm_g': mla_q_norm_g, 'mla_w_uq': mla_w_uq, 'mla_w_out': mla_w_out,
         'ffn_w_in': ffn_w_in, 'ffn_b_in': ffn_b_in, 'ffn_conv_w': ffn_conv_w,
         'ffn_conv_b': ffn_conv_b, 'ffn_w_down': ffn_w_down}
    pk = _pack_weights(p)
    bp = x_prompt.shape[0]
    mods = _ada_terms(jnp.concatenate([c_prompt, c_sample], axis=0), ada_w, ada_b)[:, :, None, :]
    zeros_like_b = lambda a: jnp.zeros((bp,) + a.shape[1:], a.dtype)
    out_p = _trunk(x_prompt, mods[:, :bp], zeros_like_b(state_gdn), zeros_like_b(state_gdn_conv),
                   zeros_like_b(state_ffn_conv), jnp.zeros((bp, 0, MLA_KV_LORA), cache_ckv.dtype),
                   jnp.zeros((bp, 0, MLA_ROPE), cache_kpe.dtype), pk)
    out_s = _trunk(x_sample, mods[:, bp:], state_gdn, state_gdn_conv, state_ffn_conv, cache_ckv, cache_kpe, pk)
    return (out_p[0], out_s[0]) + out_p[1:] + out_s[1:]
```

```python
import functools

import jax
import jax.numpy as jnp
from jax import lax
from jax.experimental import pallas as pl
from jax.experimental.pallas import tpu as pltpu

F32 = jnp.float32
BF16 = jnp.bfloat16

D_MODEL = 1024
DEPTH = 2
CHUNK = 64
ALPHA = (2.0 * DEPTH) ** 0.25
LN_EPS = 1e-5
RMS_EPS = 1e-6
GDN_HEADS = 8
GDN_DK = 128
GDN_DV = 128
GDN_CONV = 4
GDN_QK = GDN_HEADS * GDN_DK
GDN_V = GDN_HEADS * GDN_DV
GDN_CONV_CH = 2 * GDN_QK + GDN_V
MLA_HEADS = 8
MLA_NOPE = 128
MLA_ROPE = 64
MLA_V = 128
MLA_KV_LORA = 256
MLA_Q_LORA = 384
ROPE_THETA = 10000.0
MLA_SCALE = (MLA_NOPE + MLA_ROPE) ** -0.5
Q_PRESCALE = MLA_SCALE * 1.4426950408889634
D_FF = 2816
FFN_CONV = 3

LANES = 128
SUBLANES = 8
MXU_DIM = 256
HEAD_PAD = 256
VMEM_LIMIT = 56 * 1024 * 1024
NEG_BIG = -1e30

NN = ((1,), (0,))
NT = ((1,), (1,))
TN = ((0,), (0,))


def _row_tile(n, cap):
    if n <= cap:
        return n
    for t in range(cap, 15, -1):
        if n % t == 0 and t % 16 == 0:
            return t
    raise ValueError(f"no row tile for {n}")


def _split3(a):
    hi = a.astype(BF16)
    r = a - hi.astype(F32)
    mid = r.astype(BF16)
    lo = (r - mid.astype(F32)).astype(BF16)
    return hi, mid, lo


def _dg(a, b, dims=NN):
    return lax.dot_general(a, b, (dims, ((), ())), preferred_element_type=F32)


def _dot_hp(a, b, dims=NN):
    a0, a1 = a.astype(BF16), (a - a.astype(BF16).astype(F32)).astype(BF16)
    b0, b1 = b.astype(BF16), (b - b.astype(BF16).astype(F32)).astype(BF16)
    return _dg(a0, b0, dims) + (_dg(a0, b1, dims) + _dg(a1, b0, dims))


def _sigmoid(x):
    return 1.0 / (1.0 + jnp.exp(-x))


def _silu(x):
    return x * _sigmoid(x)


def _layer_norm(y, g, b):
    mu = jnp.mean(y, axis=-1, keepdims=True)
    yc = y - mu
    var = jnp.mean(yc * yc, axis=-1, keepdims=True)
    return yc * lax.rsqrt(var + LN_EPS) * g + b


def _const_spec(shape):
    nd = len(shape)
    return pl.BlockSpec(shape, lambda *_: (0,) * nd, pipeline_mode=pl.Buffered(1))


def _shift_rows(u, carry8, j):
    rows, width = u.shape
    groups = rows // SUBLANES
    rot = pltpu.roll(u.reshape(groups, SUBLANES, width), j, axis=1)
    above = jnp.concatenate([pltpu.roll(carry8, j, axis=0)[None], rot[:groups - 1]], axis=0)
    sub = lax.broadcasted_iota(jnp.int32, rot.shape, 1)
    return jnp.where(sub < j, above, rot).reshape(rows, width)


def _params(sem):
    return pltpu.CompilerParams(dimension_semantics=sem, vmem_limit_bytes=VMEM_LIMIT)


def _ada_kernel(c_ref, w_ref, b_ref, o_ref):
    s = _silu(c_ref[...])
    o_ref[0] = _dot_hp(s, w_ref[0]) + b_ref[0]


def _ada_terms(c_all, ada_w, ada_b):
    n_sub, d, n3 = ada_w.shape
    bc = c_all.shape[0]
    tn = 768
    return pl.pallas_call(
        _ada_kernel,
        grid=(n_sub, n3 // tn),
        in_specs=[
            pl.BlockSpec((bc, d), lambda i, j: (0, 0)),
            pl.BlockSpec((1, d, tn), lambda i, j: (i, 0, j)),
            pl.BlockSpec((1, 1, tn), lambda i, j: (i, 0, j)),
        ],
        out_specs=pl.BlockSpec((1, bc, tn), lambda i, j: (i, 0, j)),
        out_shape=jax.ShapeDtypeStruct((n_sub, bc, n3), F32),
        compiler_params=_params(("arbitrary", "arbitrary")),
        name="ada_terms",
    )(c_all, ada_w, ada_b.reshape(n_sub, 1, n3))


GDN_IN_TILE = 256


def _gdn_in_kernel(x_ref, mod_ref, w_ref, wba_ref, hist_ref, cw_ref, alog_ref, dtb_ref, tri_ref,
                   q_ref, k_ref, v_ref, z_ref, gb_ref, hist_out_ref, carry, *, tm):
    t = pl.program_id(1)

    @pl.when(t == 0)
    def _():
        carry[...] = hist_ref[0]

    shift = mod_ref[0, :, 0:D_MODEL]
    scale = mod_ref[0, :, D_MODEL:2 * D_MODEL]
    hin = (x_ref[0] * (1.0 + scale) + shift).astype(BF16)

    outs = (q_ref, k_ref, v_ref)

    def conv_tile(sec, c):
        lo = sec * GDN_QK + c * GDN_IN_TILE
        cols = slice(lo, lo + GDN_IN_TILE)
        ocols = slice(c * GDN_IN_TILE, (c + 1) * GDN_IN_TILE)
        u = jnp.dot(hin, w_ref[:, cols], preferred_element_type=F32)
        c8 = carry[:, cols]
        acc = u * cw_ref[GDN_CONV - 1:GDN_CONV, cols]
        for j in range(1, GDN_CONV):
            acc = acc + _shift_rows(u, c8, j) * cw_ref[GDN_CONV - 1 - j:GDN_CONV - j, cols]
        carry[:, cols] = u[tm - SUBLANES:tm]
        s = _silu(acc)
        if sec == 2:
            v_ref[0, :, ocols] = s.astype(BF16)
        else:
            post = GDN_DK ** -0.5 if sec == 0 else 1.0
            parts = []
            for h in range(GDN_IN_TILE // GDN_DK):
                sh = s[:, h * GDN_DK:(h + 1) * GDN_DK]
                inv = lax.rsqrt(jnp.sum(sh * sh, axis=-1, keepdims=True) + RMS_EPS)
                parts.append(sh * (inv * post))
            outs[sec][0, :, ocols] = jnp.concatenate(parts, axis=-1).astype(BF16)

    def z_tile(c):
        ocols = slice(c * GDN_IN_TILE, (c + 1) * GDN_IN_TILE)
        cols = slice(GDN_CONV_CH + c * GDN_IN_TILE, GDN_CONV_CH + (c + 1) * GDN_IN_TILE)
        z_ref[0, :, ocols] = jnp.dot(hin, w_ref[:, cols], preferred_element_type=F32).astype(BF16)

    def gates():
        ba = jnp.dot(hin, wba_ref[...], preferred_element_type=F32)
        beta = _sigmoid(ba)
        xs = ba + dtb_ref[...]
        softplus = jnp.maximum(xs, 0.0) + jnp.log(1.0 + jnp.exp(-jnp.abs(xs)))
        g = -jnp.exp(alog_ref[...]) * softplus
        tri = tri_ref[...]
        g0, g1, g2 = _split3(g)
        gsum = (jnp.dot(tri, g0, preferred_element_type=F32)
                + jnp.dot(tri, g1, preferred_element_type=F32)
                + jnp.dot(tri, g2, preferred_element_type=F32))
        lane = lax.broadcasted_iota(jnp.int32, (tm, LANES), 1)
        gb_ref[0] = jnp.where(lane < GDN_HEADS, beta, gsum)[:, :2 * GDN_HEADS]

    for c in range(GDN_QK // GDN_IN_TILE):
        conv_tile(0, c)
        z_tile(c)
        conv_tile(1, c)
        if c == 0:
            gates()
        conv_tile(2, c)
    hist_out_ref[0] = carry[...]


def _gdn_in(x, mod, w_main, w_ba, hist8, conv_w, alog_row, dtb_row, c_len):
    b, l, d = x.shape
    tm = _row_tile(l, 512)
    assert tm % c_len == 0
    idx = jnp.arange(tm, dtype=jnp.int32)
    tri = ((idx[:, None] >= idx[None, :]) & ((idx[:, None] // c_len) == (idx[None, :] // c_len))).astype(BF16)
    tok = lambda w: pl.BlockSpec((1, tm, w), lambda i, t: (i, t, 0))
    return pl.pallas_call(
        functools.partial(_gdn_in_kernel, tm=tm),
        grid=(b, l // tm),
        in_specs=[
            tok(d),
            pl.BlockSpec((1, 1, 3 * d), lambda i, t: (i, 0, 0)),
            _const_spec(w_main.shape),
            _const_spec(w_ba.shape),
            pl.BlockSpec((1, SUBLANES, GDN_CONV_CH), lambda i, t: (i, 0, 0)),
            _const_spec(conv_w.shape),
            _const_spec(alog_row.shape),
            _const_spec(dtb_row.shape),
            _const_spec(tri.shape),
        ],
        out_specs=[tok(GDN_QK), tok(GDN_QK), tok(GDN_V), tok(GDN_V), tok(2 * GDN_HEADS),
                   pl.BlockSpec((1, SUBLANES, GDN_CONV_CH), lambda i, t: (i, 0, 0))],
        out_shape=[
            jax.ShapeDtypeStruct((b, l, GDN_QK), BF16),
            jax.ShapeDtypeStruct((b, l, GDN_QK), BF16),
            jax.ShapeDtypeStruct((b, l, GDN_V), BF16),
            jax.ShapeDtypeStruct((b, l, GDN_V), BF16),
            jax.ShapeDtypeStruct((b, l, 2 * GDN_HEADS), F32),
            jax.ShapeDtypeStruct((b, SUBLANES, GDN_CONV_CH), F32),
        ],
        scratch_shapes=[pltpu.VMEM((SUBLANES, GDN_CONV_CH), F32)],
        compiler_params=_params(("arbitrary", "arbitrary")),
        name="gdn_in",
    )(x, mod, w_main, w_ba, hist8, conv_w, alog_row, dtb_row, tri)


GDN_GROUP = 4


def _bf(xs):
    return [x.astype(BF16) for x in xs]


def _drain(gen):
    for _ in gen:
        pass


def _interleave(major, n_major, minor, n_minor):
    done = 0
    for i in range(n_major):
        next(major, None)
        while done * n_major < (i + 1) * n_minor:
            next(minor, None)
            done += 1
    _drain(major)
    _drain(minor)


def _gdn_chunk_kernel(q_ref, k_ref, v_ref, col_ref, row_ref, s0_ref, o_ref, s_ref, *, c_len, group):
    @pl.when(pl.program_id(1) == 0)
    def _():
        s_ref[...] = s0_ref[...]

    ri = lax.broadcasted_iota(jnp.int32, (c_len, c_len), 0)
    ci = lax.broadcasted_iota(jnp.int32, (c_len, c_len), 1)
    incl = ri >= ci
    strict = ri > ci
    n_sq = c_len.bit_length() - 2
    heads = range(GDN_HEADS)
    rows = lambda g: slice(g * c_len, (g + 1) * c_len)
    lanes = lambda h: slice(h * GDN_DK, (h + 1) * GDN_DK)

    def prepare(chunks, w):
        items = [(g, h) for g in chunks for h in heads]
        each = range(len(items))
        col = {g: col_ref[0, g] for g in chunks}
        row = {g: row_ref[0, g] for g in chunks}
        w['qb'] = [q_ref[0, rows(g), lanes(h)] for g, h in items]
        w['kb'] = [k_ref[0, rows(g), lanes(h)] for g, h in items]
        w['k'] = [x.astype(F32) for x in w['kb']]
        w['v'] = [v_ref[0, rows(g), lanes(h)].astype(F32) for g, h in items]
        w['beta'] = [col[g][:, h:h + 1] for g, h in items]
        w['gc'] = [col[g][:, GDN_HEADS + h:GDN_HEADS + h + 1] for g, h in items]
        gr = [row[g][GDN_HEADS + h:GDN_HEADS + h + 1, :] for g, h in items]
        yield
        w['gam'] = [jnp.where(incl, jnp.exp(jnp.where(incl, w['gc'][i] - gr[i], 0.0)), 0.0) for i in each]
        w['eg'] = [jnp.exp(w['gc'][i]) for i in each]
        yield
        w['sol'] = [jnp.concatenate([w['beta'][i] * w['v'][i], w['beta'][i] * w['k'][i] * w['eg'][i]], axis=-1)
                    for i in each]
        g_last = [w['gc'][i][c_len - 1:c_len, :] for i in each]
        w['k_dec'] = _bf([w['k'][i] * jnp.exp(g_last[i] - w['gc'][i]) for i in each])
        w['decay'] = [jnp.exp(g_last[i]) for i in each]
        yield

    n_prepare = 3

    def solve(w):
        each = range(len(w['qb']))
        qkk = [_dg(jnp.concatenate([w['qb'][i], w['kb'][i]], axis=0), w['kb'][i], NT) for i in each]
        yield
        p = _bf([jnp.where(strict, -(w['beta'][i] * qkk[i][c_len:] * w['gam'][i]), 0.0) for i in each])
        w['att'] = _bf([qkk[i][:c_len] * w['gam'][i] for i in each])
        sol = w['sol']
        solb = _bf(sol)
        sol = [sol[i] + _dg(p[i], solb[i]) for i in each]
        yield
        for _ in range(n_sq):
            p = _bf([_dg(p[i], p[i]) for i in each])
            yield
            solb = _bf(sol)
            sol = [sol[i] + _dg(p[i], solb[i]) for i in each]
            yield
        w['u'] = [sol[i][:, :GDN_DV] for i in each]
        w['wq'] = [jnp.concatenate([sol[i][:, GDN_DV:].astype(BF16),
                                    (w['qb'][i].astype(F32) * w['eg'][i]).astype(BF16)], axis=0)
                   for i in each]

    n_solve = 2 + 2 * n_sq

    state = {'s': [s_ref[0, h] for h in heads]}

    def recur(chunks, w):
        for n, g in enumerate(chunks):
            it = [n * GDN_HEADS + h for h in heads]
            sb = _bf(state['s'])
            ws = [_dg(w['wq'][it[h]], sb[h]) for h in heads]
            yield
            vb = _bf([w['u'][it[h]] - ws[h][:c_len] for h in heads])
            o = [ws[h][c_len:] + _dg(w['att'][it[h]], vb[h]) for h in heads]
            state['s'] = [w['decay'][it[h]] * state['s'][h] + _dg(w['k_dec'][it[h]], vb[h], TN) for h in heads]
            yield
            o_ref[0, rows(g), :] = jnp.concatenate(o, axis=-1).astype(BF16)

    n_waves = 2 if group % 2 == 0 else 1
    per = group // n_waves
    waves = [list(range(n * per, (n + 1) * per)) for n in range(n_waves)]
    data = [{} for _ in waves]
    _drain(prepare(waves[0], data[0]))
    for n in range(n_waves):
        side = []
        n_side = 0
        if n + 1 < n_waves:
            side.append(prepare(waves[n + 1], data[n + 1]))
            n_side += n_prepare
        if n >= 1:
            side.append(recur(waves[n - 1], data[n - 1]))
            n_side += 2 * per

        def chain(gens=side):
            for gen in gens:
                yield from gen

        _interleave(solve(data[n]), n_solve, chain(), n_side)
    _drain(recur(waves[-1], data[-1]))
    for h in heads:
        s_ref[0, h] = state['s'][h]


def _gdn_chunk(q, k, v, col, row, s0, c_len):
    b, l, _ = q.shape
    n = l // c_len
    group = GDN_GROUP if n % GDN_GROUP == 0 else 1
    rows = c_len * group
    tok = pl.BlockSpec((1, rows, GDN_QK), lambda i, t: (i, t, 0))
    st = pl.BlockSpec((1, GDN_HEADS, GDN_DK, GDN_DV), lambda i, t: (i, 0, 0, 0))
    return pl.pallas_call(
        functools.partial(_gdn_chunk_kernel, c_len=c_len, group=group),
        grid=(b, n // group),
        in_specs=[tok, tok, tok,
                  pl.BlockSpec((1, group, c_len, 2 * GDN_HEADS), lambda i, t: (i, t, 0, 0)),
                  pl.BlockSpec((1, group, 2 * GDN_HEADS, c_len), lambda i, t: (i, t, 0, 0)),
                  st],
        out_specs=[tok, st],
        out_shape=[jax.ShapeDtypeStruct((b, l, GDN_V), BF16),
                   jax.ShapeDtypeStruct(s0.shape, F32)],
        compiler_params=_params(("arbitrary", "arbitrary")),
        name="gdn_chunk",
    )(q, k, v, col, row, s0)


FFN_TILE = MXU_DIM
FFN_ROWS = 512
FFN_SUB_TILES = 2


def _mix_ffn_kernel(*refs, gated, tm, n_sub):
    if gated:
        o_ref, z_ref, ng_ref = refs[:3]
        refs = refs[3:]
    else:
        o_ref = refs[0]
        refs = refs[1:]
    (x_ref, mmod_ref, wo_ref, mlg_ref, mlb_ref, fmod_ref, win_ref, bin_ref, cw_ref, cb_ref, hist_ref, wdn_ref,
     lg_ref, lb_ref, y_ref, hist_out_ref, carry, x1_ref, hbuf, fbuf) = refs
    t = pl.program_id(1)

    @pl.when(t == 0)
    def _():
        carry[...] = hist_ref[0]

    mgate = mmod_ref[0, :, 2 * D_MODEL:3 * D_MODEL]
    shift = fmod_ref[0, :, 0:D_MODEL]
    scale = fmod_ref[0, :, D_MODEL:2 * D_MODEL]
    gate = fmod_ref[0, :, 2 * D_MODEL:3 * D_MODEL]
    sub = tm // n_sub
    subs = range(n_sub)
    rows = lambda s: slice(s * sub, (s + 1) * sub)

    def mixer_input(s):
        if not gated:
            return
        for h in range(GDN_HEADS):
            hs = slice(h * GDN_DV, (h + 1) * GDN_DV)
            oh = o_ref[0, rows(s), hs].astype(F32)
            r = oh * lax.rsqrt(jnp.mean(oh * oh, axis=-1, keepdims=True) + RMS_EPS) * ng_ref[...]
            hbuf[s, :, hs] = (r * _silu(z_ref[0, rows(s), hs].astype(F32))).astype(BF16)

    def mixer_proj(s):
        a = hbuf[s] if gated else o_ref[0, rows(s), :]
        return jnp.dot(a, wo_ref[...], preferred_element_type=F32)

    def mixer_norm(s, mix):
        x1_ref[s] = _layer_norm(ALPHA * x_ref[0, rows(s), :] + (1.0 + mgate) * mix, mlg_ref[...], mlb_ref[...])
        hbuf[s] = (x1_ref[s] * (1.0 + scale) + shift).astype(BF16)

    def conv_half(s, lo):
        cols = slice(lo, lo + FFN_TILE)
        u = jnp.dot(hbuf[s], win_ref[:, cols], preferred_element_type=F32) + bin_ref[:, cols]
        c8 = carry[:, cols]
        out = u * cw_ref[FFN_CONV - 1:FFN_CONV, cols] + cb_ref[:, cols]
        for j in range(1, FFN_CONV):
            out = out + _shift_rows(u, c8, j) * cw_ref[FFN_CONV - 1 - j:FFN_CONV - j, cols]
        carry[:, cols] = u[sub - SUBLANES:sub]
        return out

    def ffn_norm(s, f):
        y_ref[0, rows(s), :] = _layer_norm(ALPHA * x1_ref[s] + (1.0 + gate) * f, lg_ref[...], lb_ref[...])

    mixer_input(0)
    mix = {}
    for s in subs:
        mix[s] = mixer_proj(s)
        if s + 1 < n_sub:
            mixer_input(s + 1)
        if s >= 1:
            mixer_norm(s - 1, mix.pop(s - 1))
    mixer_norm(n_sub - 1, mix.pop(n_sub - 1))

    for i in range(D_FF // FFN_TILE):
        lo = i * FFN_TILE
        for s in subs:
            ua = conv_half(s, lo)
            ub = conv_half(s, lo + D_FF)
            fbuf[s, :, lo:lo + FFN_TILE] = (_silu(ua) * ub).astype(BF16)
    hist_out_ref[0] = carry[...]

    down = {}
    for s in subs:
        down[s] = jnp.dot(fbuf[s], wdn_ref[...], preferred_element_type=F32)
        if s >= 1:
            ffn_norm(s - 1, down.pop(s - 1))
    ffn_norm(n_sub - 1, down.pop(n_sub - 1))


def _layer_spec(stacked, layer):
    nd = stacked.ndim - 1
    return pl.BlockSpec((None,) + stacked.shape[1:], lambda *_: (layer,) + (0,) * nd, pipeline_mode=pl.Buffered(1))


def _mix_ffn(o, z, norm_g, x, mix_mod, w_out, mix_ln_g, mix_ln_b,
             ffn_mod, w_in_all, b_in, conv_w, conv_b, hist8, w_down_all, ln_g, ln_b, layer):
    b, l, d = x.shape
    tm = _row_tile(l, FFN_ROWS)
    n_sub = FFN_SUB_TILES if tm == FFN_ROWS else 1
    gated = z is not None
    tok = pl.BlockSpec((1, tm, d), lambda i, t: (i, t, 0))
    modspec = pl.BlockSpec((1, 1, 3 * d), lambda i, t: (i, 0, 0))
    hspec = pl.BlockSpec((1, SUBLANES, 2 * D_FF), lambda i, t: (i, 0, 0))
    row = _const_spec((1, d))
    head_args, head_specs = ((o, z, norm_g), [tok, tok, _const_spec(norm_g.shape)]) if gated else ((o,), [tok])
    return pl.pallas_call(
        functools.partial(_mix_ffn_kernel, gated=gated, tm=tm, n_sub=n_sub),
        grid=(b, l // tm),
        in_specs=head_specs + [tok, modspec, _const_spec(w_out.shape), row, row,
                               modspec, _layer_spec(w_in_all, layer), _const_spec(b_in.shape),
                               _const_spec(conv_w.shape), _const_spec(conv_b.shape), hspec,
                               _layer_spec(w_down_all, layer), row, row],
        out_specs=[tok, hspec],
        out_shape=[jax.ShapeDtypeStruct((b, l, d), F32),
                   jax.ShapeDtypeStruct((b, SUBLANES, 2 * D_FF), F32)],
        scratch_shapes=[pltpu.VMEM((SUBLANES, 2 * D_FF), F32), pltpu.VMEM((n_sub, tm // n_sub, d), F32),
                        pltpu.VMEM((n_sub, tm // n_sub, d), BF16), pltpu.VMEM((n_sub, tm // n_sub, D_FF), BF16)],
        compiler_params=_params(("arbitrary", "arbitrary")),
        name="mix_ffn_gated" if gated else "mix_ffn",
    )(*head_args, x, mix_mod, w_out, mix_ln_g, mix_ln_b,
      ffn_mod, w_in_all, b_in, conv_w, conv_b, hist8, w_down_all, ln_g, ln_b)


def _rope_pair(pair, cs):
    prod = pair * cs
    return prod + pltpu.roll(prod, MLA_ROPE, axis=1)


def _mla_proj_kernel(x_ref, mod_ref, wkv_ref, kvg_ref, cs_ref, wdq_ref, qg_ref, wuq_ref, *rest, with_kv):
    if with_kv:
        wk_ref, wvt_ref, ckv_ref, kpe_ref, q_ref, k_ref, vt_ref = rest
    else:
        ckv_ref, kpe_ref, q_ref = rest
    x = x_ref[0]
    cs = cs_ref[...]
    lane = lax.broadcasted_iota(jnp.int32, (x.shape[0], LANES), 1)
    kv = jnp.dot(x.astype(BF16), wkv_ref[...], preferred_element_type=F32)
    lat = kv[:, :MLA_KV_LORA]
    ckv = lat * lax.rsqrt(jnp.mean(lat * lat, axis=-1, keepdims=True) + RMS_EPS) * kvg_ref[...]
    kpe = _rope_pair(kv[:, MLA_KV_LORA:], cs)
    ckv_ref[0] = ckv
    kpe_ref[0] = kpe[:, :MLA_ROPE]
    if with_kv:
        lat_b = ckv.astype(BF16)
        kpe_b = jnp.where(lane < MLA_ROPE, kpe, 0.0).astype(BF16)

    shift = mod_ref[0, :, 0:D_MODEL]
    scale = mod_ref[0, :, D_MODEL:2 * D_MODEL]
    hin = (x * (1.0 + scale) + shift).astype(BF16)
    qd = jnp.dot(hin, wdq_ref[...], preferred_element_type=F32)
    qd = qd * lax.rsqrt(jnp.mean(qd * qd, axis=-1, keepdims=True) + RMS_EPS) * qg_ref[...]
    qd = qd.astype(BF16)
    for h in range(MLA_HEADS):
        qh = jnp.dot(qd, wuq_ref[:, h * HEAD_PAD:(h + 1) * HEAD_PAD], preferred_element_type=F32)
        q_ref[0, :, h * HEAD_PAD:h * HEAD_PAD + MLA_NOPE] = (qh[:, :MLA_NOPE] * Q_PRESCALE).astype(BF16)
        pe = jnp.where(lane < MLA_ROPE, _rope_pair(qh[:, MLA_NOPE:], cs) * Q_PRESCALE, 0.0)
        q_ref[0, :, h * HEAD_PAD + MLA_NOPE:(h + 1) * HEAD_PAD] = pe.astype(BF16)
        if with_kv:
            _kv_head(h, lat_b, kpe_b, wk_ref, wvt_ref, k_ref, vt_ref)


def _mla_proj(x, mod, w_kv, kv_g, cossin, w_dq, q_g, w_uq, w_k, w_vt, with_kv):
    b, l, d = x.shape
    tm = _row_tile(l, 512)
    tok = lambda w: pl.BlockSpec((1, tm, w), lambda i, t: (i, t, 0))
    in_specs = [tok(d), pl.BlockSpec((1, 1, 3 * d), lambda i, t: (i, 0, 0)),
                _const_spec(w_kv.shape), _const_spec(kv_g.shape),
                pl.BlockSpec((tm, LANES), lambda i, t: (t, 0)),
                _const_spec(w_dq.shape), _const_spec(q_g.shape), _const_spec(w_uq.shape)]
    out_specs = [tok(MLA_KV_LORA), tok(MLA_ROPE), tok(MLA_HEADS * HEAD_PAD)]
    out_shape = [jax.ShapeDtypeStruct((b, l, MLA_KV_LORA), F32),
                 jax.ShapeDtypeStruct((b, l, MLA_ROPE), F32),
                 jax.ShapeDtypeStruct((b, l, MLA_HEADS * HEAD_PAD), BF16)]
    args = (x, mod, w_kv, kv_g, cossin, w_dq, q_g, w_uq)
    if with_kv:
        in_specs += [_const_spec(w_k.shape), _const_spec(w_vt.shape)]
        out_specs += [tok(MLA_HEADS * HEAD_PAD), pl.BlockSpec((1, MLA_HEADS * MLA_V, tm), lambda i, t: (i, 0, t))]
        out_shape += [jax.ShapeDtypeStruct((b, l, MLA_HEADS * HEAD_PAD), BF16),
                      jax.ShapeDtypeStruct((b, MLA_HEADS * MLA_V, l), BF16)]
        args += (w_k, w_vt)
    return pl.pallas_call(
        functools.partial(_mla_proj_kernel, with_kv=with_kv),
        grid=(b, l // tm),
        in_specs=in_specs,
        out_specs=out_specs,
        out_shape=out_shape,
        compiler_params=_params(("arbitrary", "arbitrary")),
        name="mla_proj_kv" if with_kv else "mla_proj",
    )(*args)


def _kv_head(h, lat, kpe, wk_ref, wvt_ref, k_ref, vt_ref):
    kn = jnp.dot(lat, wk_ref[:, h * MLA_NOPE:(h + 1) * MLA_NOPE], preferred_element_type=F32)
    k_ref[0, :, h * HEAD_PAD:h * HEAD_PAD + MLA_NOPE] = kn.astype(BF16)
    k_ref[0, :, h * HEAD_PAD + MLA_NOPE:(h + 1) * HEAD_PAD] = kpe
    vt_ref[0, h * MLA_V:(h + 1) * MLA_V, :] = _dg(wvt_ref[h], lat, NT).astype(BF16)


def _kv_up_kernel(ckv_ref, kpe_ref, wk_ref, wvt_ref, k_ref, vt_ref):
    lat = ckv_ref[0].astype(BF16)
    kpe = kpe_ref[0].astype(BF16)
    for h in range(MLA_HEADS):
        _kv_head(h, lat, kpe, wk_ref, wvt_ref, k_ref, vt_ref)


def _kv_up(ckv_all, kpe_pad, w_k, w_vt):
    b, lk, _ = ckv_all.shape
    tm = ATTN_TK if lk % ATTN_TK == 0 else lk
    tok = lambda w: pl.BlockSpec((1, tm, w), lambda i, t: (i, t, 0))
    return pl.pallas_call(
        _kv_up_kernel,
        grid=(b, lk // tm),
        in_specs=[tok(MLA_KV_LORA), tok(LANES), _const_spec(w_k.shape), _const_spec(w_vt.shape)],
        out_specs=[tok(MLA_HEADS * HEAD_PAD),
                   pl.BlockSpec((1, MLA_HEADS * MLA_V, tm), lambda i, t: (i, 0, t))],
        out_shape=[jax.ShapeDtypeStruct((b, lk, MLA_HEADS * HEAD_PAD), BF16),
                   jax.ShapeDtypeStruct((b, MLA_HEADS * MLA_V, lk), BF16)],
        compiler_params=_params(("arbitrary", "arbitrary")),
        name="kv_up",
    )(ckv_all, kpe_pad, w_k, w_vt)


ATTN_TQ = 512
ATTN_TK = 512
ATTN_HEADS_PER_STEP = 4
CHUNK_SHIFT = CHUNK.bit_length() - 1


def _chunk_end(pos):
    return (lax.shift_right_logical(pos, CHUNK_SHIFT) + 1) * CHUNK


def _attn_kernel(q_ref, k_ref, vt_ref, o_ref, acc_ref, m_ref, l_ref, sa_ref, sb_ref, *, tq, tk, past, lk, nh):
    i = pl.program_id(2)
    acc_ref[...] = jnp.zeros_like(acc_ref)
    m_ref[...] = jnp.full_like(m_ref, NEG_BIG)
    l_ref[...] = jnp.zeros_like(l_ref)
    q0 = past + i * tq
    n_full = lax.div(jnp.minimum(_chunk_end(q0), lk), tk)
    n_end = lax.div(jnp.minimum(_chunk_end(q0 + tq - 1), lk) + tk - 1, tk)
    n_blk = lk // tk
    heads = range(nh)
    ones = jnp.ones((2 * SUBLANES, tk), BF16)

    def key_start(j):
        return pl.multiple_of(jnp.minimum(j, n_blk - 1) * tk, tk) if lk > tk else 0

    def scores(j, dst):
        k0 = key_start(j)
        for h in heads:
            dst[h] = _dg(k_ref[0, pl.ds(k0, tk), h * HEAD_PAD:(h + 1) * HEAD_PAD],
                         q_ref[0, :, h * HEAD_PAD:(h + 1) * HEAD_PAD], NT)

    def step(j, src, dst, masked):
        if dst is not None:
            scores(j + 1, dst)
        k0 = key_start(j)
        st = [src[h] for h in heads]
        if masked:
            kc = lax.shift_right_logical(k0 + lax.broadcasted_iota(jnp.int32, (tk, tq), 0), CHUNK_SHIFT)
            qc = lax.shift_right_logical(q0 + lax.broadcasted_iota(jnp.int32, (tk, tq), 1), CHUNK_SHIFT)
            visible = kc <= qc
            st = [jnp.where(visible, s, NEG_BIG) for s in st]
        m_old = [m_ref[h] for h in heads]
        m_new = [jnp.maximum(m_old[h], jnp.max(st[h], axis=0, keepdims=True)) for h in heads]
        alpha = [jnp.exp2(m_old[h] - m_new[h]) for h in heads]
        pt = [jnp.exp2(st[h] - m_new[h]).astype(BF16) for h in heads]
        pv = [_dg(jnp.concatenate([vt_ref[0, h * MLA_V:(h + 1) * MLA_V, pl.ds(k0, tk)], ones], axis=0), pt[h])
              for h in heads]
        for h in heads:
            acc_ref[h] = alpha[h] * acc_ref[h] + pv[h][:MLA_V]
            m_ref[h] = m_new[h]
            l_ref[h] = alpha[h] * l_ref[h] + pv[h][MLA_V:MLA_V + 1]

    if n_blk == 1:
        scores(0, sa_ref)

        @pl.when(n_full == 1)
        def _():
            step(0, sa_ref, None, False)

        @pl.when(n_full == 0)
        def _():
            step(0, sa_ref, None, True)
    else:
        odd = lax.rem(n_full, 2)

        @pl.when(odd == 1)
        def _():
            scores(0, sb_ref)
            step(0, sb_ref, sa_ref, False)

        @pl.when(odd == 0)
        def _():
            scores(0, sa_ref)

        def pair(p, c):
            j = odd + 2 * p
            step(j, sa_ref, sb_ref, False)
            step(j + 1, sb_ref, sa_ref, False)
            return c

        lax.fori_loop(0, lax.div(n_full, 2), pair, 0)

        @pl.when(n_end > n_full)
        def _():
            step(n_full, sa_ref, None, True)

        def rest(j, c):
            scores(j, sa_ref)
            step(j, sa_ref, None, True)
            return c

        lax.fori_loop(n_full + 1, n_end, rest, 0)
    for h in heads:
        o_ref[0, :, h * MLA_V:(h + 1) * MLA_V] = (acc_ref[h] / l_ref[h]).T.astype(BF16)


def _attention(q, k, vt, past):
    b, lq, _ = q.shape
    lk = k.shape[1]
    lq_pad = -(-lq // LANES) * LANES
    if lq_pad != lq:
        q = jnp.pad(q, ((0, 0), (0, lq_pad - lq), (0, 0)))
    tq = _row_tile(lq_pad, ATTN_TQ)
    tk = ATTN_TK if lk % ATTN_TK == 0 else lk
    nh = ATTN_HEADS_PER_STEP
    kv_bytes = nh * lk * (HEAD_PAD + MLA_V) * jnp.dtype(BF16).itemsize
    kv_mode = pl.Buffered(2 if 2 * kv_bytes <= VMEM_LIMIT // 4 else 1)
    out = pl.pallas_call(
        functools.partial(_attn_kernel, tq=tq, tk=tk, past=past, lk=lk, nh=nh),
        grid=(b, MLA_HEADS // nh, lq_pad // tq),
        in_specs=[pl.BlockSpec((1, tq, nh * HEAD_PAD), lambda bi, h, i: (bi, i, h)),
                  pl.BlockSpec((1, lk, nh * HEAD_PAD), lambda bi, h, i: (bi, 0, h), pipeline_mode=kv_mode),
                  pl.BlockSpec((1, nh * MLA_V, lk), lambda bi, h, i: (bi, h, 0), pipeline_mode=kv_mode)],
        out_specs=pl.BlockSpec((1, tq, nh * MLA_V), lambda bi, h, i: (bi, i, h)),
        out_shape=jax.ShapeDtypeStruct((b, lq_pad, MLA_HEADS * MLA_V), BF16),
        scratch_shapes=[pltpu.VMEM((nh, MLA_V, tq), F32), pltpu.VMEM((nh, 1, tq), F32),
                        pltpu.VMEM((nh, 1, tq), F32), pltpu.VMEM((nh, tk, tq), F32),
                        pltpu.VMEM((nh, tk, tq), F32)],
        compiler_params=_params(("arbitrary", "arbitrary", "arbitrary")),
        name="chunk_causal_attention",
    )(q, k, vt)
    return out[:, :lq]


def _pad_hist(hist):
    return jnp.pad(hist, ((0, 0), (SUBLANES - hist.shape[1], 0), (0, 0)))


def _rot_half_cols(w):
    w1, w2 = jnp.split(w, 2, axis=-1)
    return jnp.concatenate([-w2, w1], axis=-1)


def _pack_weights(p):
    d = D_MODEL
    pk = {}
    w_in = p['gdn_w_in'][0]
    pk['gdn_w_main'] = w_in[:, :GDN_CONV_CH + GDN_V].astype(BF16)
    pk['gdn_w_ba'] = jnp.pad(w_in[:, GDN_CONV_CH + GDN_V:], ((0, 0), (0, LANES - 2 * GDN_HEADS))).astype(BF16)
    pad_gate = lambda a: jnp.pad(a.reshape(1, GDN_HEADS), ((0, 0), (GDN_HEADS, LANES - 2 * GDN_HEADS)))
    pk['gdn_alog'] = pad_gate(p['gdn_a_log'][0])
    pk['gdn_dtb'] = pad_gate(p['gdn_dt_bias'][0])
    pk['gdn_conv_w'] = p['gdn_conv_w'][0]
    pk['gdn_norm_g'] = p['gdn_norm_g'][0].reshape(1, GDN_DV)
    pk['gdn_w_out'] = p['gdn_w_out'][0].astype(BF16)
    wkv = p['kv_w_down']
    pk['kv_w_down'] = jnp.concatenate([wkv, _rot_half_cols(wkv[:, MLA_KV_LORA:])], axis=-1).astype(BF16)
    pk['kv_norm_g'] = p['kv_norm_g'].reshape(1, MLA_KV_LORA)
    w_up = p['kv_w_up'].reshape(MLA_KV_LORA, MLA_HEADS, MLA_NOPE + MLA_V)
    pk['kv_w_k'] = w_up[..., :MLA_NOPE].reshape(MLA_KV_LORA, MLA_HEADS * MLA_NOPE).astype(BF16)
    pk['kv_w_vt'] = jnp.transpose(w_up[..., MLA_NOPE:], (1, 2, 0)).astype(BF16)
    pk['mla_w_dq'] = p['mla_w_dq'][0].astype(BF16)
    pk['mla_q_norm_g'] = p['mla_q_norm_g'][0].reshape(1, MLA_Q_LORA)
    wuq = p['mla_w_uq'][0].reshape(MLA_Q_LORA, MLA_HEADS, MLA_NOPE + MLA_ROPE)
    wuq = jnp.concatenate([wuq, _rot_half_cols(wuq[..., MLA_NOPE:])], axis=-1)
    pk['mla_w_uq'] = wuq.reshape(MLA_Q_LORA, MLA_HEADS * HEAD_PAD).astype(BF16)
    pk['mla_w_out'] = p['mla_w_out'][0].astype(BF16)
    pk['ffn_w_in'] = p['ffn_w_in'].astype(BF16)
    pk['ffn_b_in'] = p['ffn_b_in'].reshape(DEPTH, 1, 2 * D_FF)
    pk['ffn_conv_w'] = p['ffn_conv_w']
    pk['ffn_conv_b'] = p['ffn_conv_b'].reshape(DEPTH, 1, 2 * D_FF)
    pk['ffn_w_down'] = p['ffn_w_down'].astype(BF16)
    pk['ln_g'] = p['ln_g'].reshape(2 * DEPTH, 1, d)
    pk['ln_b'] = p['ln_b'].reshape(2 * DEPTH, 1, d)
    return pk


def _rope_table(past, l):
    inv = 1.0 / (ROPE_THETA ** (jnp.arange(0, MLA_ROPE, 2, dtype=F32) / MLA_ROPE))
    ang = (past + jnp.arange(l, dtype=jnp.int32)).astype(F32)[:, None] * inv[None, :]
    ang = jnp.concatenate([ang, ang], axis=-1)
    return jnp.concatenate([jnp.cos(ang), jnp.sin(ang)], axis=-1)


def _block_tail(o, z, norm_g, w_out, x, mods, hist, pk, layer):
    y, hist8 = _mix_ffn(o, z, norm_g, x, mods[2 * layer], w_out, pk['ln_g'][2 * layer], pk['ln_b'][2 * layer],
                        mods[2 * layer + 1], pk['ffn_w_in'], pk['ffn_b_in'][layer],
                        pk['ffn_conv_w'][layer], pk['ffn_conv_b'][layer], _pad_hist(hist),
                        pk['ffn_w_down'], pk['ln_g'][2 * layer + 1], pk['ln_b'][2 * layer + 1], layer)
    return y, hist8[:, SUBLANES - (FFN_CONV - 1):]


def _trunk(x, mods, gdn_state, gdn_conv, ffn_conv, ckv_past, kpe_past, pk):
    b, l, _ = x.shape
    past = kpe_past.shape[1]
    c_len = CHUNK if l % CHUNK == 0 else l
    n = l // c_len

    q, k, v, z, gb, hist8 = _gdn_in(x, mods[0], pk['gdn_w_main'], pk['gdn_w_ba'], _pad_hist(gdn_conv[:, 0]),
                                    pk['gdn_conv_w'], pk['gdn_alog'], pk['gdn_dtb'], c_len)
    col = gb.reshape(b, n, c_len, 2 * GDN_HEADS)
    row = jnp.swapaxes(col, 2, 3)
    o, s_end = _gdn_chunk(q, k, v, col, row, gdn_state[:, 0], c_len)
    gdn_conv_out = hist8[:, None, SUBLANES - (GDN_CONV - 1):]
    x, fh0 = _block_tail(o, z, pk['gdn_norm_g'], pk['gdn_w_out'], x, mods, ffn_conv[:, 0], pk, 0)

    proj = _mla_proj(x, mods[2], pk['kv_w_down'], pk['kv_norm_g'], _rope_table(past, l),
                     pk['mla_w_dq'], pk['mla_q_norm_g'], pk['mla_w_uq'], pk['kv_w_k'], pk['kv_w_vt'],
                     with_kv=(past == 0))
    ckv_new, kpe_new, qm = proj[:3]
    if past == 0:
        km, vm = proj[3:]
    else:
        ckv_all = jnp.concatenate([ckv_past, ckv_new], axis=1)
        kpe_all = jnp.concatenate([kpe_past, kpe_new], axis=1)
        kpe_pad = jnp.pad(kpe_all, ((0, 0), (0, 0), (0, LANES - MLA_ROPE)))
        km, vm = _kv_up(ckv_all, kpe_pad, pk['kv_w_k'], pk['kv_w_vt'])
    o = _attention(qm, km, vm, past)
    x, fh1 = _block_tail(o, None, None, pk['mla_w_out'], x, mods, ffn_conv[:, 1], pk, 1)

    return (x, s_end[:, None], gdn_conv_out, jnp.stack([fh0, fh1], axis=1), ckv_new, kpe_new)


def kernel(x_prompt, x_sample, c_prompt, c_sample, state_gdn, state_gdn_conv, state_ffn_conv, cache_ckv, cache_kpe, ada_w, ada_b, ln_g, ln_b, gdn_w_in, gdn_conv_w, gdn_a_log, gdn_dt_bias, gdn_norm_g, gdn_w_out, kv_w_down, kv_norm_g, kv_w_up, mla_w_dq, mla_q_norm_g, mla_w_uq, mla_w_out, ffn_w_in, ffn_b_in, ffn_conv_w, ffn_conv_b, ffn_w_down):
    p = {'ln_g': ln_g, 'ln_b': ln_b, 'gdn_w_in': gdn_w_in, 'gdn_conv_w': gdn_conv_w, 'gdn_a_log': gdn_a_log,
         'gdn_dt_bias': gdn_dt_bias, 'gdn_norm_g': gdn_norm_g, 'gdn_w_out': gdn_w_out,
         'kv_w_down': kv_w_down, 'kv_norm_g': kv_norm_g, 'kv_w_up': kv_w_up,
         'mla_w_dq': mla_w_dq, 'mla_q_norm_g': mla_q_norm_g, 'mla_w_uq': mla_w_uq, 'mla_w_out': mla_w_out,
         'ffn_w_in': ffn_w_in, 'ffn_b_in': ffn_b_in, 'ffn_conv_w': ffn_conv_w,
         'ffn_conv_b': ffn_conv_b, 'ffn_w_down': ffn_w_down}
    pk = _pack_weights(p)
    bp = x_prompt.shape[0]
    mods = _ada_terms(jnp.concatenate([c_prompt, c_sample], axis=0), ada_w, ada_b)[:, :, None, :]
    zeros_like_b = lambda a: jnp.zeros((bp,) + a.shape[1:], a.dtype)
    out_p = _trunk(x_prompt, mods[:, :bp], zeros_like_b(state_gdn), zeros_like_b(state_gdn_conv),
                   zeros_like_b(state_ffn_conv), jnp.zeros((bp, 0, MLA_KV_LORA), cache_ckv.dtype),
                   jnp.zeros((bp, 0, MLA_ROPE), cache_kpe.dtype), pk)
    out_s = _trunk(x_sample, mods[:, bp:], state_gdn, state_gdn_conv, state_ffn_conv, cache_ckv, cache_kpe, pk)
    return (out_p[0], out_s[0]) + out_p[1:] + out_s[1:]
```

```python
import functools

import jax
import jax.numpy as jnp
from jax import lax
from jax.experimental import pallas as pl
from jax.experimental.pallas import tpu as pltpu

F32 = jnp.float32
BF16 = jnp.bfloat16

D_MODEL = 1024
DEPTH = 2
CHUNK = 64
ALPHA = (2.0 * DEPTH) ** 0.25
LN_EPS = 1e-5
RMS_EPS = 1e-6
GDN_HEADS = 8
GDN_DK = 128
GDN_DV = 128
GDN_CONV = 4
GDN_QK = GDN_HEADS * GDN_DK
GDN_V = GDN_HEADS * GDN_DV
GDN_CONV_CH = 2 * GDN_QK + GDN_V
MLA_HEADS = 8
MLA_NOPE = 128
MLA_ROPE = 64
MLA_V = 128
MLA_KV_LORA = 256
MLA_Q_LORA = 384
ROPE_THETA = 10000.0
MLA_SCALE = (MLA_NOPE + MLA_ROPE) ** -0.5
Q_PRESCALE = MLA_SCALE * 1.4426950408889634
D_FF = 2816
FFN_CONV = 3

LANES = 128
SUBLANES = 8
MXU_DIM = 256
HEAD_PAD = 256
VMEM_LIMIT = 56 * 1024 * 1024
NEG_BIG = -1e30

NN = ((1,), (0,))
NT = ((1,), (1,))
TN = ((0,), (0,))


def _row_tile(n, cap):
    if n <= cap:
        return n
    for t in range(cap, 15, -1):
        if n % t == 0 and t % 16 == 0:
            return t
    raise ValueError(f"no row tile for {n}")


def _split3(a):
    hi = a.astype(BF16)
    r = a - hi.astype(F32)
    mid = r.astype(BF16)
    lo = (r - mid.astype(F32)).astype(BF16)
    return hi, mid, lo


def _dg(a, b, dims=NN):
    return lax.dot_general(a, b, (dims, ((), ())), preferred_element_type=F32)


def _dot_hp(a, b, dims=NN):
    a0, a1 = a.astype(BF16), (a - a.astype(BF16).astype(F32)).astype(BF16)
    b0, b1 = b.astype(BF16), (b - b.astype(BF16).astype(F32)).astype(BF16)
    return _dg(a0, b0, dims) + (_dg(a0, b1, dims) + _dg(a1, b0, dims))


def _sigmoid(x):
    return 1.0 / (1.0 + jnp.exp(-x))


def _silu(x):
    return x * _sigmoid(x)


def _layer_norm(y, g, b):
    mu = jnp.mean(y, axis=-1, keepdims=True)
    yc = y - mu
    var = jnp.mean(yc * yc, axis=-1, keepdims=True)
    return yc * lax.rsqrt(var + LN_EPS) * g + b


def _const_spec(shape):
    nd = len(shape)
    return pl.BlockSpec(shape, lambda *_: (0,) * nd, pipeline_mode=pl.Buffered(1))


def _shift_rows(u, carry8, j):
    rows, width = u.shape
    groups = rows // SUBLANES
    rot = pltpu.roll(u.reshape(groups, SUBLANES, width), j, axis=1)
    above = jnp.concatenate([pltpu.roll(carry8, j, axis=0)[None], rot[:groups - 1]], axis=0)
    sub = lax.broadcasted_iota(jnp.int32, rot.shape, 1)
    return jnp.where(sub < j, above, rot).reshape(rows, width)


def _params(sem):
    return pltpu.CompilerParams(dimension_semantics=sem, vmem_limit_bytes=VMEM_LIMIT)


def _ada_kernel(c_ref, w_ref, b_ref, o_ref):
    s = _silu(c_ref[...])
    o_ref[0] = _dot_hp(s, w_ref[0]) + b_ref[0]


def _ada_terms(c_all, ada_w, ada_b):
    n_sub, d, n3 = ada_w.shape
    bc = c_all.shape[0]
    tn = 768
    return pl.pallas_call(
        _ada_kernel,
        grid=(n_sub, n3 // tn),
        in_specs=[
            pl.BlockSpec((bc, d), lambda i, j: (0, 0)),
            pl.BlockSpec((1, d, tn), lambda i, j: (i, 0, j)),
            pl.BlockSpec((1, 1, tn), lambda i, j: (i, 0, j)),
        ],
        out_specs=pl.BlockSpec((1, bc, tn), lambda i, j: (i, 0, j)),
        out_shape=jax.ShapeDtypeStruct((n_sub, bc, n3), F32),
        compiler_params=_params(("arbitrary", "arbitrary")),
        name="ada_terms",
    )(c_all, ada_w, ada_b.reshape(n_sub, 1, n3))


GDN_IN_TILE = 256


def _gdn_in_kernel(x_ref, mod_ref, w_ref, wba_ref, hist_ref, cw_ref, alog_ref, dtb_ref, tri_ref,
                   q_ref, k_ref, v_ref, z_ref, gb_ref, hist_out_ref, carry, *, tm):
    t = pl.program_id(1)

    @pl.when(t == 0)
    def _():
        carry[...] = hist_ref[0]

    shift = mod_ref[0, :, 0:D_MODEL]
    scale = mod_ref[0, :, D_MODEL:2 * D_MODEL]
    hin = (x_ref[0] * (1.0 + scale) + shift).astype(BF16)

    outs = (q_ref, k_ref, v_ref)

    def conv_tile(sec, c):
        lo = sec * GDN_QK + c * GDN_IN_TILE
        cols = slice(lo, lo + GDN_IN_TILE)
        ocols = slice(c * GDN_IN_TILE, (c + 1) * GDN_IN_TILE)
        u = jnp.dot(hin, w_ref[:, cols], preferred_element_type=F32)
        c8 = carry[:, cols]
        acc = u * cw_ref[GDN_CONV - 1:GDN_CONV, cols]
        for j in range(1, GDN_CONV):
            acc = acc + _shift_rows(u, c8, j) * cw_ref[GDN_CONV - 1 - j:GDN_CONV - j, cols]
        carry[:, cols] = u[tm - SUBLANES:tm]
        s = _silu(acc)
        if sec == 2:
            v_ref[0, :, ocols] = s.astype(BF16)
        else:
            post = GDN_DK ** -0.5 if sec == 0 else 1.0
            parts = []
            for h in range(GDN_IN_TILE // GDN_DK):
                sh = s[:, h * GDN_DK:(h + 1) * GDN_DK]
                inv = lax.rsqrt(jnp.sum(sh * sh, axis=-1, keepdims=True) + RMS_EPS)
                parts.append(sh * (inv * post))
            outs[sec][0, :, ocols] = jnp.concatenate(parts, axis=-1).astype(BF16)

    def z_tile(c):
        ocols = slice(c * GDN_IN_TILE, (c + 1) * GDN_IN_TILE)
        cols = slice(GDN_CONV_CH + c * GDN_IN_TILE, GDN_CONV_CH + (c + 1) * GDN_IN_TILE)
        z_ref[0, :, ocols] = jnp.dot(hin, w_ref[:, cols], preferred_element_type=F32).astype(BF16)

    def gates():
        ba = jnp.dot(hin, wba_ref[...], preferred_element_type=F32)
        beta = _sigmoid(ba)
        xs = ba + dtb_ref[...]
        softplus = jnp.maximum(xs, 0.0) + jnp.log(1.0 + jnp.exp(-jnp.abs(xs)))
        g = -jnp.exp(alog_ref[...]) * softplus
        tri = tri_ref[...]
        g0, g1, g2 = _split3(g)
        gsum = (jnp.dot(tri, g0, preferred_element_type=F32)
                + jnp.dot(tri, g1, preferred_element_type=F32)
                + jnp.dot(tri, g2, preferred_element_type=F32))
        lane = lax.broadcasted_iota(jnp.int32, (tm, LANES), 1)
        gb_ref[0] = jnp.where(lane < GDN_HEADS, beta, gsum)[:, :2 * GDN_HEADS]

    for c in range(GDN_QK // GDN_IN_TILE):
        conv_tile(0, c)
        z_tile(c)
        conv_tile(1, c)
        if c == 0:
            gates()
        conv_tile(2, c)
    hist_out_ref[0] = carry[...]


def _gdn_in(x, mod, w_main, w_ba, hist8, conv_w, alog_row, dtb_row, c_len):
    b, l, d = x.shape
    tm = _row_tile(l, 512)
    assert tm % c_len == 0
    idx = jnp.arange(tm, dtype=jnp.int32)
    tri = ((idx[:, None] >= idx[None, :]) & ((idx[:, None] // c_len) == (idx[None, :] // c_len))).astype(BF16)
    tok = lambda w: pl.BlockSpec((1, tm, w), lambda i, t: (i, t, 0))
    return pl.pallas_call(
        functools.partial(_gdn_in_kernel, tm=tm),
        grid=(b, l // tm),
        in_specs=[
            tok(d),
            pl.BlockSpec((1, 1, 3 * d), lambda i, t: (i, 0, 0)),
            _const_spec(w_main.shape),
            _const_spec(w_ba.shape),
            pl.BlockSpec((1, SUBLANES, GDN_CONV_CH), lambda i, t: (i, 0, 0)),
            _const_spec(conv_w.shape),
            _const_spec(alog_row.shape),
            _const_spec(dtb_row.shape),
            _const_spec(tri.shape),
        ],
        out_specs=[tok(GDN_QK), tok(GDN_QK), tok(GDN_V), tok(GDN_V), tok(2 * GDN_HEADS),
                   pl.BlockSpec((1, SUBLANES, GDN_CONV_CH), lambda i, t: (i, 0, 0))],
        out_shape=[
            jax.ShapeDtypeStruct((b, l, GDN_QK), BF16),
            jax.ShapeDtypeStruct((b, l, GDN_QK), BF16),
            jax.ShapeDtypeStruct((b, l, GDN_V), BF16),
            jax.ShapeDtypeStruct((b, l, GDN_V), BF16),
            jax.ShapeDtypeStruct((b, l, 2 * GDN_HEADS), F32),
            jax.ShapeDtypeStruct((b, SUBLANES, GDN_CONV_CH), F32),
        ],
        scratch_shapes=[pltpu.VMEM((SUBLANES, GDN_CONV_CH), F32)],
        compiler_params=_params(("arbitrary", "arbitrary")),
        name="gdn_in",
    )(x, mod, w_main, w_ba, hist8, conv_w, alog_row, dtb_row, tri)


GDN_GROUP = 4


def _bf(xs):
    return [x.astype(BF16) for x in xs]


def _drain(gen):
    for _ in gen:
        pass


def _interleave(major, n_major, minor, n_minor):
    done = 0
    for i in range(n_major):
        next(major, None)
        while done * n_major < (i + 1) * n_minor:
            next(minor, None)
            done += 1
    _drain(major)
    _drain(minor)


def _gdn_chunk_kernel(q_ref, k_ref, v_ref, col_ref, row_ref, s0_ref, o_ref, s_ref, *, c_len, group):
    @pl.when(pl.program_id(1) == 0)
    def _():
        s_ref[...] = s0_ref[...]

    ri = lax.broadcasted_iota(jnp.int32, (c_len, c_len), 0)
    ci = lax.broadcasted_iota(jnp.int32, (c_len, c_len), 1)
    incl = ri >= ci
    strict = ri > ci
    n_sq = c_len.bit_length() - 2
    heads = range(GDN_HEADS)
    rows = lambda g: slice(g * c_len, (g + 1) * c_len)
    lanes = lambda h: slice(h * GDN_DK, (h + 1) * GDN_DK)

    def prepare(chunks, w):
        items = [(g, h) for g in chunks for h in heads]
        each = range(len(items))
        col = {g: col_ref[0, g] for g in chunks}
        row = {g: row_ref[0, g] for g in chunks}
        w['qb'] = [q_ref[0, rows(g), lanes(h)] for g, h in items]
        w['kb'] = [k_ref[0, rows(g), lanes(h)] for g, h in items]
        w['k'] = [x.astype(F32) for x in w['kb']]
        w['v'] = [v_ref[0, rows(g), lanes(h)].astype(F32) for g, h in items]
        w['beta'] = [col[g][:, h:h + 1] for g, h in items]
        w['gc'] = [col[g][:, GDN_HEADS + h:GDN_HEADS + h + 1] for g, h in items]
        gr = [row[g][GDN_HEADS + h:GDN_HEADS + h + 1, :] for g, h in items]
        yield
        w['gam'] = [jnp.where(incl, jnp.exp(jnp.where(incl, w['gc'][i] - gr[i], 0.0)), 0.0) for i in each]
        w['eg'] = [jnp.exp(w['gc'][i]) for i in each]
        yield
        w['sol'] = [jnp.concatenate([w['beta'][i] * w['v'][i], w['beta'][i] * w['k'][i] * w['eg'][i]], axis=-1)
                    for i in each]
        g_last = [w['gc'][i][c_len - 1:c_len, :] for i in each]
        w['k_dec'] = _bf([w['k'][i] * jnp.exp(g_last[i] - w['gc'][i]) for i in each])
        w['decay'] = [jnp.exp(g_last[i]) for i in each]
        yield

    n_prepare = 3

    def solve(w):
        each = range(len(w['qb']))
        qkk = [_dg(jnp.concatenate([w['qb'][i], w['kb'][i]], axis=0), w['kb'][i], NT) for i in each]
        yield
        p = _bf([jnp.where(strict, -(w['beta'][i] * qkk[i][c_len:] * w['gam'][i]), 0.0) for i in each])
        w['att'] = _bf([qkk[i][:c_len] * w['gam'][i] for i in each])
        sol = w['sol']
        solb = _bf(sol)
        sol = [sol[i] + _dg(p[i], solb[i]) for i in each]
        yield
        for _ in range(n_sq):
            p = _bf([_dg(p[i], p[i]) for i in each])
            yield
            solb = _bf(sol)
            sol = [sol[i] + _dg(p[i], solb[i]) for i in each]
            yield
        w['u'] = [sol[i][:, :GDN_DV] for i in each]
        w['wq'] = [jnp.concatenate([sol[i][:, GDN_DV:].astype(BF16),
                                    (w['qb'][i].astype(F32) * w['eg'][i]).astype(BF16)], axis=0)
                   for i in each]

    n_solve = 2 + 2 * n_sq

    state = {'s': [s_ref[0, h] for h in heads]}

    def recur(chunks, w):
        for n, g in enumerate(chunks):
            it = [n * GDN_HEADS + h for h in heads]
            sb = _bf(state['s'])
            ws = [_dg(w['wq'][it[h]], sb[h]) for h in heads]
            yield
            vb = _bf([w['u'][it[h]] - ws[h][:c_len] for h in heads])
            o = [ws[h][c_len:] + _dg(w['att'][it[h]], vb[h]) for h in heads]
            state['s'] = [w['decay'][it[h]] * state['s'][h] + _dg(w['k_dec'][it[h]], vb[h], TN) for h in heads]
            yield
            o_ref[0, rows(g), :] = jnp.concatenate(o, axis=-1).astype(BF16)

    n_waves = 2 if group % 2 == 0 else 1
    per = group // n_waves
    waves = [list(range(n * per, (n + 1) * per)) for n in range(n_waves)]
    data = [{} for _ in waves]
    _drain(prepare(waves[0], data[0]))
    for n in range(n_waves):
        side = []
        n_side = 0
        if n + 1 < n_waves:
            side.append(prepare(waves[n + 1], data[n + 1]))
            n_side += n_prepare
        if n >= 1:
            side.append(recur(waves[n - 1], data[n - 1]))
            n_side += 2 * per

        def chain(gens=side):
            for gen in gens:
                yield from gen

        _interleave(solve(data[n]), n_solve, chain(), n_side)
    _drain(recur(waves[-1], data[-1]))
    for h in heads:
        s_ref[0, h] = state['s'][h]


def _gdn_chunk(q, k, v, col, row, s0, c_len):
    b, l, _ = q.shape
    n = l // c_len
    group = GDN_GROUP if n % GDN_GROUP == 0 else 1
    rows = c_len * group
    tok = pl.BlockSpec((1, rows, GDN_QK), lambda i, t: (i, t, 0))
    st = pl.BlockSpec((1, GDN_HEADS, GDN_DK, GDN_DV), lambda i, t: (i, 0, 0, 0))
    return pl.pallas_call(
        functools.partial(_gdn_chunk_kernel, c_len=c_len, group=group),
        grid=(b, n // group),
        in_specs=[tok, tok, tok,
                  pl.BlockSpec((1, group, c_len, 2 * GDN_HEADS), lambda i, t: (i, t, 0, 0)),
                  pl.BlockSpec((1, group, 2 * GDN_HEADS, c_len), lambda i, t: (i, t, 0, 0)),
                  st],
        out_specs=[tok, st],
        out_shape=[jax.ShapeDtypeStruct((b, l, GDN_V), BF16),
                   jax.ShapeDtypeStruct(s0.shape, F32)],
        compiler_params=_params(("arbitrary", "arbitrary")),
        name="gdn_chunk",
    )(q, k, v, col, row, s0)


FFN_TILE = MXU_DIM
FFN_ROWS = 512
FFN_SUB_TILES = 2


def _mix_ffn_kernel(*refs, gated, tm, n_sub):
    if gated:
        o_ref, z_ref, ng_ref = refs[:3]
        refs = refs[3:]
    else:
        o_ref = refs[0]
        refs = refs[1:]
    (x_ref, mmod_ref, wo_ref, mlg_ref, mlb_ref, fmod_ref, win_ref, bin_ref, cw_ref, cb_ref, hist_ref, wdn_ref,
     lg_ref, lb_ref, y_ref, hist_out_ref, carry, x1_ref, hbuf, fbuf) = refs
    t = pl.program_id(1)

    @pl.when(t == 0)
    def _():
        carry[...] = hist_ref[0]

    mgate = mmod_ref[0, :, 2 * D_MODEL:3 * D_MODEL]
    shift = fmod_ref[0, :, 0:D_MODEL]
    scale = fmod_ref[0, :, D_MODEL:2 * D_MODEL]
    gate = fmod_ref[0, :, 2 * D_MODEL:3 * D_MODEL]
    sub = tm // n_sub
    subs = range(n_sub)
    rows = lambda s: slice(s * sub, (s + 1) * sub)

    def mixer_input(s):
        if not gated:
            return
        for h in range(GDN_HEADS):
            hs = slice(h * GDN_DV, (h + 1) * GDN_DV)
            oh = o_ref[0, rows(s), hs].astype(F32)
            r = oh * lax.rsqrt(jnp.mean(oh * oh, axis=-1, keepdims=True) + RMS_EPS) * ng_ref[...]
            hbuf[s, :, hs] = (r * _silu(z_ref[0, rows(s), hs].astype(F32))).astype(BF16)

    def mixer_proj(s):
        a = hbuf[s] if gated else o_ref[0, rows(s), :]
        return jnp.dot(a, wo_ref[...], preferred_element_type=F32)

    def mixer_norm(s, mix):
        x1_ref[s] = _layer_norm(ALPHA * x_ref[0, rows(s), :] + (1.0 + mgate) * mix, mlg_ref[...], mlb_ref[...])
        hbuf[s] = (x1_ref[s] * (1.0 + scale) + shift).astype(BF16)

    def conv_half(s, lo):
        cols = slice(lo, lo + FFN_TILE)
        u = jnp.dot(hbuf[s], win_ref[:, cols], preferred_element_type=F32) + bin_ref[:, cols]
        c8 = carry[:, cols]
        out = u * cw_ref[FFN_CONV - 1:FFN_CONV, cols] + cb_ref[:, cols]
        for j in range(1, FFN_CONV):
            out = out + _shift_rows(u, c8, j) * cw_ref[FFN_CONV - 1 - j:FFN_CONV - j, cols]
        carry[:, cols] = u[sub - SUBLANES:sub]
        return out

    def ffn_norm(s, f):
        y_ref[0, rows(s), :] = _layer_norm(ALPHA * x1_ref[s] + (1.0 + gate) * f, lg_ref[...], lb_ref[...])

    mixer_input(0)
    mix = {}
    for s in subs:
        mix[s] = mixer_proj(s)
        if s + 1 < n_sub:
            mixer_input(s + 1)
        if s >= 1:
            mixer_norm(s - 1, mix.pop(s - 1))
    mixer_norm(n_sub - 1, mix.pop(n_sub - 1))

    for i in range(D_FF // FFN_TILE):
        lo = i * FFN_TILE
        for s in subs:
            ua = conv_half(s, lo)
            ub = conv_half(s, lo + D_FF)
            fbuf[s, :, lo:lo + FFN_TILE] = (_silu(ua) * ub).astype(BF16)
    hist_out_ref[0] = carry[...]

    down = {}
    for s in subs:
        down[s] = jnp.dot(fbuf[s], wdn_ref[...], preferred_element_type=F32)
        if s >= 1:
            ffn_norm(s - 1, down.pop(s - 1))
    ffn_norm(n_sub - 1, down.pop(n_sub - 1))


def _layer_spec(stacked, layer):
    nd = stacked.ndim - 1
    return pl.BlockSpec((None,) + stacked.shape[1:], lambda *_: (layer,) + (0,) * nd, pipeline_mode=pl.Buffered(1))


def _mix_ffn(o, z, norm_g, x, mix_mod, w_out, mix_ln_g, mix_ln_b,
             ffn_mod, w_in_all, b_in, conv_w, conv_b, hist8, w_down_all, ln_g, ln_b, layer):
    b, l, d = x.shape
    tm = _row_tile(l, FFN_ROWS)
    n_sub = FFN_SUB_TILES if tm == FFN_ROWS else 1
    gated = z is not None
    tok = pl.BlockSpec((1, tm, d), lambda i, t: (i, t, 0))
    modspec = pl.BlockSpec((1, 1, 3 * d), lambda i, t: (i, 0, 0))
    hspec = pl.BlockSpec((1, SUBLANES, 2 * D_FF), lambda i, t: (i, 0, 0))
    row = _const_spec((1, d))
    head_args, head_specs = ((o, z, norm_g), [tok, tok, _const_spec(norm_g.shape)]) if gated else ((o,), [tok])
    return pl.pallas_call(
        functools.partial(_mix_ffn_kernel, gated=gated, tm=tm, n_sub=n_sub),
        grid=(b, l // tm),
        in_specs=head_specs + [tok, modspec, _const_spec(w_out.shape), row, row,
                               modspec, _layer_spec(w_in_all, layer), _const_spec(b_in.shape),
                               _const_spec(conv_w.shape), _const_spec(conv_b.shape), hspec,
                               _layer_spec(w_down_all, layer), row, row],
        out_specs=[tok, hspec],
        out_shape=[jax.ShapeDtypeStruct((b, l, d), F32),
                   jax.ShapeDtypeStruct((b, SUBLANES, 2 * D_FF), F32)],
        scratch_shapes=[pltpu.VMEM((SUBLANES, 2 * D_FF), F32), pltpu.VMEM((n_sub, tm // n_sub, d), F32),
                        pltpu.VMEM((n_sub, tm // n_sub, d), BF16), pltpu.VMEM((n_sub, tm // n_sub, D_FF), BF16)],
        compiler_params=_params(("arbitrary", "arbitrary")),
        name="mix_ffn_gated" if gated else "mix_ffn",
    )(*head_args, x, mix_mod, w_out, mix_ln_g, mix_ln_b,
      ffn_mod, w_in_all, b_in, conv_w, conv_b, hist8, w_down_all, ln_g, ln_b)


def _rope_pair(pair, cs):
    prod = pair * cs
    return prod + pltpu.roll(prod, MLA_ROPE, axis=1)


def _mla_proj_kernel(x_ref, mod_ref, wkv_ref, kvg_ref, cs_ref, wdq_ref, qg_ref, wuq_ref, *rest, with_kv):
    if with_kv:
        wk_ref, wvt_ref, ckv_ref, kpe_ref, q_ref, k_ref, vt_ref = rest
    else:
        ckv_ref, kpe_ref, q_ref = rest
    x = x_ref[0]
    cs = cs_ref[...]
    lane = lax.broadcasted_iota(jnp.int32, (x.shape[0], LANES), 1)
    kv = jnp.dot(x.astype(BF16), wkv_ref[...], preferred_element_type=F32)
    lat = kv[:, :MLA_KV_LORA]
    ckv = lat * lax.rsqrt(jnp.mean(lat * lat, axis=-1, keepdims=True) + RMS_EPS) * kvg_ref[...]
    kpe = _rope_pair(kv[:, MLA_KV_LORA:], cs)
    ckv_ref[0] = ckv
    kpe_ref[0] = kpe[:, :MLA_ROPE]
    if with_kv:
        lat_b = ckv.astype(BF16)
        kpe_b = jnp.where(lane < MLA_ROPE, kpe, 0.0).astype(BF16)

    shift = mod_ref[0, :, 0:D_MODEL]
    scale = mod_ref[0, :, D_MODEL:2 * D_MODEL]
    hin = (x * (1.0 + scale) + shift).astype(BF16)
    qd = jnp.dot(hin, wdq_ref[...], preferred_element_type=F32)
    qd = qd * lax.rsqrt(jnp.mean(qd * qd, axis=-1, keepdims=True) + RMS_EPS) * qg_ref[...]
    qd = qd.astype(BF16)
    for h in range(MLA_HEADS):
        qh = jnp.dot(qd, wuq_ref[:, h * HEAD_PAD:(h + 1) * HEAD_PAD], preferred_element_type=F32)
        q_ref[0, :, h * HEAD_PAD:h * HEAD_PAD + MLA_NOPE] = (qh[:, :MLA_NOPE] * Q_PRESCALE).astype(BF16)
        pe = jnp.where(lane < MLA_ROPE, _rope_pair(qh[:, MLA_NOPE:], cs) * Q_PRESCALE, 0.0)
        q_ref[0, :, h * HEAD_PAD + MLA_NOPE:(h + 1) * HEAD_PAD] = pe.astype(BF16)
        if with_kv:
            _kv_head(h, lat_b, kpe_b, wk_ref, wvt_ref, k_ref, vt_ref)


def _mla_proj(x, mod, w_kv, kv_g, cossin, w_dq, q_g, w_uq, w_k, w_vt, with_kv):
    b, l, d = x.shape
    tm = _row_tile(l, 512)
    tok = lambda w: pl.BlockSpec((1, tm, w), lambda i, t: (i, t, 0))
    in_specs = [tok(d), pl.BlockSpec((1, 1, 3 * d), lambda i, t: (i, 0, 0)),
                _const_spec(w_kv.shape), _const_spec(kv_g.shape),
                pl.BlockSpec((tm, LANES), lambda i, t: (t, 0)),
                _const_spec(w_dq.shape), _const_spec(q_g.shape), _const_spec(w_uq.shape)]
    out_specs = [tok(MLA_KV_LORA), tok(MLA_ROPE), tok(MLA_HEADS * HEAD_PAD)]
    out_shape = [jax.ShapeDtypeStruct((b, l, MLA_KV_LORA), F32),
                 jax.ShapeDtypeStruct((b, l, MLA_ROPE), F32),
                 jax.ShapeDtypeStruct((b, l, MLA_HEADS * HEAD_PAD), BF16)]
    args = (x, mod, w_kv, kv_g, cossin, w_dq, q_g, w_uq)
    if with_kv:
        in_specs += [_const_spec(w_k.shape), _const_spec(w_vt.shape)]
        out_specs += [tok(MLA_HEADS * HEAD_PAD), pl.BlockSpec((1, MLA_HEADS * MLA_V, tm), lambda i, t: (i, 0, t))]
        out_shape += [jax.ShapeDtypeStruct((b, l, MLA_HEADS * HEAD_PAD), BF16),
                      jax.ShapeDtypeStruct((b, MLA_HEADS * MLA_V, l), BF16)]
        args += (w_k, w_vt)
    return pl.pallas_call(
        functools.partial(_mla_proj_kernel, with_kv=with_kv),
        grid=(b, l // tm),
        in_specs=in_specs,
        out_specs=out_specs,
        out_shape=out_shape,
        compiler_params=_params(("arbitrary", "arbitrary")),
        name="mla_proj_kv" if with_kv else "mla_proj",
    )(*args)


def _kv_head(h, lat, kpe, wk_ref, wvt_ref, k_ref, vt_ref):
    kn = jnp.dot(lat, wk_ref[:, h * MLA_NOPE:(h + 1) * MLA_NOPE], preferred_element_type=F32)
    k_ref[0, :, h * HEAD_PAD:h * HEAD_PAD + MLA_NOPE] = kn.astype(BF16)
    k_ref[0, :, h * HEAD_PAD + MLA_NOPE:(h + 1) * HEAD_PAD] = kpe
    vt_ref[0, h * MLA_V:(h + 1) * MLA_V, :] = _dg(wvt_ref[h], lat, NT).astype(BF16)


def _kv_up_kernel(ckv_ref, kpe_ref, wk_ref, wvt_ref, k_ref, vt_ref):
    lat = ckv_ref[0].astype(BF16)
    kpe = kpe_ref[0].astype(BF16)
    for h in range(MLA_HEADS):
        _kv_head(h, lat, kpe, wk_ref, wvt_ref, k_ref, vt_ref)


def _kv_up(ckv_all, kpe_pad, w_k, w_vt):
    b, lk, _ = ckv_all.shape
    tm = ATTN_TK if lk % ATTN_TK == 0 else lk
    tok = lambda w: pl.BlockSpec((1, tm, w), lambda i, t: (i, t, 0))
    return pl.pallas_call(
        _kv_up_kernel,
        grid=(b, lk // tm),
        in_specs=[tok(MLA_KV_LORA), tok(LANES), _const_spec(w_k.shape), _const_spec(w_vt.shape)],
        out_specs=[tok(MLA_HEADS * HEAD_PAD),
                   pl.BlockSpec((1, MLA_HEADS * MLA_V, tm), lambda i, t: (i, 0, t))],
        out_shape=[jax.ShapeDtypeStruct((b, lk, MLA_HEADS * HEAD_PAD), BF16),
                   jax.ShapeDtypeStruct((b, MLA_HEADS * MLA_V, lk), BF16)],
        compiler_params=_params(("arbitrary", "arbitrary")),
        name="kv_up",
    )(ckv_all, kpe_pad, w_k, w_vt)


ATTN_TQ = 512
ATTN_TK = 512
ATTN_HEADS_PER_STEP = 4
CHUNK_SHIFT = CHUNK.bit_length() - 1


def _chunk_end(pos):
    return (lax.shift_right_logical(pos, CHUNK_SHIFT) + 1) * CHUNK


def _attn_kernel(q_ref, k_ref, vt_ref, o_ref, acc_ref, m_ref, l_ref, sa_ref, sb_ref, *, tq, tk, past, lk, nh):
    i = pl.program_id(2)
    acc_ref[...] = jnp.zeros_like(acc_ref)
    m_ref[...] = jnp.full_like(m_ref, NEG_BIG)
    l_ref[...] = jnp.zeros_like(l_ref)
    q0 = past + i * tq
    n_full = lax.div(jnp.minimum(_chunk_end(q0), lk), tk)
    n_end = lax.div(jnp.minimum(_chunk_end(q0 + tq - 1), lk) + tk - 1, tk)
    n_blk = lk // tk
    heads = range(nh)
    ones = jnp.ones((2 * SUBLANES, tk), BF16)

    def key_start(j):
        return pl.multiple_of(jnp.minimum(j, n_blk - 1) * tk, tk) if lk > tk else 0

    def scores(j, dst):
        k0 = key_start(j)
        for h in heads:
            dst[h] = _dg(k_ref[0, pl.ds(k0, tk), h * HEAD_PAD:(h + 1) * HEAD_PAD],
                         q_ref[0, :, h * HEAD_PAD:(h + 1) * HEAD_PAD], NT)

    def step(j, src, dst, masked, nxt=None):
        if dst is not None:
            scores(j + 1 if nxt is None else nxt, dst)
        k0 = key_start(j)
        st = [src[h] for h in heads]
        if masked:
            kc = lax.shift_right_logical(k0 + lax.broadcasted_iota(jnp.int32, (tk, tq), 0), CHUNK_SHIFT)
            qc = lax.shift_right_logical(q0 + lax.broadcasted_iota(jnp.int32, (tk, tq), 1), CHUNK_SHIFT)
            visible = kc <= qc
            st = [jnp.where(visible, s, NEG_BIG) for s in st]
        m_old = [m_ref[h] for h in heads]
        m_new = [jnp.maximum(m_old[h], jnp.max(st[h], axis=0, keepdims=True)) for h in heads]
        alpha = [jnp.exp2(m_old[h] - m_new[h]) for h in heads]
        pt = [jnp.exp2(st[h] - m_new[h]).astype(BF16) for h in heads]
        pv = [_dg(jnp.concatenate([vt_ref[0, h * MLA_V:(h + 1) * MLA_V, pl.ds(k0, tk)], ones], axis=0), pt[h])
              for h in heads]
        for h in heads:
            acc_ref[h] = alpha[h] * acc_ref[h] + pv[h][:MLA_V]
            m_ref[h] = m_new[h]
            l_ref[h] = alpha[h] * l_ref[h] + pv[h][MLA_V:MLA_V + 1]

    if n_blk == 1:
        scores(0, sa_ref)

        @pl.when(n_full == 1)
        def _():
            step(0, sa_ref, None, False)

        @pl.when(n_full == 0)
        def _():
            step(0, sa_ref, None, True)
    else:
        def rest(j, c):
            scores(j, sa_ref)
            step(j, sa_ref, None, True)
            return c

        lax.fori_loop(n_full + 1, n_end, rest, 0)

        odd = lax.rem(n_full, 2) == 1
        has_mask = n_end > n_full

        @pl.when(has_mask & odd)
        def _():
            scores(n_full, sa_ref)
            step(n_full, sa_ref, sb_ref, True, nxt=0)

        @pl.when(has_mask & jnp.logical_not(odd))
        def _():
            scores(n_full, sb_ref)
            step(n_full, sb_ref, sa_ref, True, nxt=0)

        @pl.when(jnp.logical_not(has_mask) & odd)
        def _():
            scores(0, sb_ref)

        @pl.when(jnp.logical_not(has_mask) & jnp.logical_not(odd))
        def _():
            scores(0, sa_ref)

        @pl.when(odd)
        def _():
            step(0, sb_ref, sa_ref, False)

        def pair(p, c):
            j = odd.astype(jnp.int32) + 2 * p
            step(j, sa_ref, sb_ref, False)
            step(j + 1, sb_ref, sa_ref, False)
            return c

        lax.fori_loop(0, lax.div(n_full, 2), pair, 0)
    for h in heads:
        o_ref[0, :, h * MLA_V:(h + 1) * MLA_V] = (acc_ref[h] / l_ref[h]).T.astype(BF16)


def _attention(q, k, vt, past):
    b, lq, _ = q.shape
    lk = k.shape[1]
    lq_pad = -(-lq // LANES) * LANES
    if lq_pad != lq:
        q = jnp.pad(q, ((0, 0), (0, lq_pad - lq), (0, 0)))
    tq = _row_tile(lq_pad, ATTN_TQ)
    tk = ATTN_TK if lk % ATTN_TK == 0 else lk
    nh = ATTN_HEADS_PER_STEP
    kv_bytes = nh * lk * (HEAD_PAD + MLA_V) * jnp.dtype(BF16).itemsize
    kv_mode = pl.Buffered(2 if 2 * kv_bytes <= VMEM_LIMIT // 4 else 1)
    out = pl.pallas_call(
        functools.partial(_attn_kernel, tq=tq, tk=tk, past=past, lk=lk, nh=nh),
        grid=(b, MLA_HEADS // nh, lq_pad // tq),
        in_specs=[pl.BlockSpec((1, tq, nh * HEAD_PAD), lambda bi, h, i: (bi, i, h)),
                  pl.BlockSpec((1, lk, nh * HEAD_PAD), lambda bi, h, i: (bi, 0, h), pipeline_mode=kv_mode),
                  pl.BlockSpec((1, nh * MLA_V, lk), lambda bi, h, i: (bi, h, 0), pipeline_mode=kv_mode)],
        out_specs=pl.BlockSpec((1, tq, nh * MLA_V), lambda bi, h, i: (bi, i, h)),
        out_shape=jax.ShapeDtypeStruct((b, lq_pad, MLA_HEADS * MLA_V), BF16),
        scratch_shapes=[pltpu.VMEM((nh, MLA_V, tq), F32), pltpu.VMEM((nh, 1, tq), F32),
                        pltpu.VMEM((nh, 1, tq), F32), pltpu.VMEM((nh, tk, tq), F32),
                        pltpu.VMEM((nh, tk, tq), F32)],
        compiler_params=_params(("arbitrary", "arbitrary", "arbitrary")),
        name="chunk_causal_attention",
    )(q, k, vt)
    return out[:, :lq]


def _pad_hist(hist):
    return jnp.pad(hist, ((0, 0), (SUBLANES - hist.shape[1], 0), (0, 0)))


def _rot_half_cols(w):
    w1, w2 = jnp.split(w, 2, axis=-1)
    return jnp.concatenate([-w2, w1], axis=-1)


def _pack_weights(p):
    d = D_MODEL
    pk = {}
    w_in = p['gdn_w_in'][0]
    pk['gdn_w_main'] = w_in.astype(BF16)
    pk['gdn_w_ba'] = jnp.pad(w_in[:, GDN_CONV_CH + GDN_V:], ((0, 0), (0, LANES - 2 * GDN_HEADS))).astype(BF16)
    pad_gate = lambda a: jnp.pad(a.reshape(1, GDN_HEADS), ((0, 0), (GDN_HEADS, LANES - 2 * GDN_HEADS)))
    pk['gdn_alog'] = pad_gate(p['gdn_a_log'][0])
    pk['gdn_dtb'] = pad_gate(p['gdn_dt_bias'][0])
    pk['gdn_conv_w'] = p['gdn_conv_w'][0]
    pk['gdn_norm_g'] = p['gdn_norm_g'][0].reshape(1, GDN_DV)
    pk['gdn_w_out'] = p['gdn_w_out'][0].astype(BF16)
    wkv = p['kv_w_down']
    pk['kv_w_down'] = jnp.concatenate([wkv, _rot_half_cols(wkv[:, MLA_KV_LORA:])], axis=-1).astype(BF16)
    pk['kv_norm_g'] = p['kv_norm_g'].reshape(1, MLA_KV_LORA)
    w_up = p['kv_w_up'].reshape(MLA_KV_LORA, MLA_HEADS, MLA_NOPE + MLA_V)
    pk['kv_w_k'] = w_up[..., :MLA_NOPE].reshape(MLA_KV_LORA, MLA_HEADS * MLA_NOPE).astype(BF16)
    pk['kv_w_vt'] = jnp.transpose(w_up[..., MLA_NOPE:], (1, 2, 0)).astype(BF16)
    pk['mla_w_dq'] = p['mla_w_dq'][0].astype(BF16)
    pk['mla_q_norm_g'] = p['mla_q_norm_g'][0].reshape(1, MLA_Q_LORA)
    wuq = p['mla_w_uq'][0].reshape(MLA_Q_LORA, MLA_HEADS, MLA_NOPE + MLA_ROPE)
    wuq = jnp.concatenate([wuq, _rot_half_cols(wuq[..., MLA_NOPE:])], axis=-1)
    pk['mla_w_uq'] = wuq.reshape(MLA_Q_LORA, MLA_HEADS * HEAD_PAD).astype(BF16)
    pk['mla_w_out'] = p['mla_w_out'][0].astype(BF16)
    pk['ffn_w_in'] = p['ffn_w_in'].astype(BF16)
    pk['ffn_b_in'] = p['ffn_b_in'].reshape(DEPTH, 1, 2 * D_FF)
    pk['ffn_conv_w'] = p['ffn_conv_w']
    pk['ffn_conv_b'] = p['ffn_conv_b'].reshape(DEPTH, 1, 2 * D_FF)
    pk['ffn_w_down'] = p['ffn_w_down'].astype(BF16)
    pk['ln_g'] = p['ln_g'].reshape(2 * DEPTH, 1, d)
    pk['ln_b'] = p['ln_b'].reshape(2 * DEPTH, 1, d)
    return pk


def _rope_table(past, l):
    inv = 1.0 / (ROPE_THETA ** (jnp.arange(0, MLA_ROPE, 2, dtype=F32) / MLA_ROPE))
    ang = (past + jnp.arange(l, dtype=jnp.int32)).astype(F32)[:, None] * inv[None, :]
    ang = jnp.concatenate([ang, ang], axis=-1)
    return jnp.concatenate([jnp.cos(ang), jnp.sin(ang)], axis=-1)


def _block_tail(o, z, norm_g, w_out, x, mods, hist, pk, layer):
    y, hist8 = _mix_ffn(o, z, norm_g, x, mods[2 * layer], w_out, pk['ln_g'][2 * layer], pk['ln_b'][2 * layer],
                        mods[2 * layer + 1], pk['ffn_w_in'], pk['ffn_b_in'][layer],
                        pk['ffn_conv_w'][layer], pk['ffn_conv_b'][layer], _pad_hist(hist),
                        pk['ffn_w_down'], pk['ln_g'][2 * layer + 1], pk['ln_b'][2 * layer + 1], layer)
    return y, hist8[:, SUBLANES - (FFN_CONV - 1):]


def _trunk(x, mods, gdn_state, gdn_conv, ffn_conv, ckv_past, kpe_past, pk):
    b, l, _ = x.shape
    past = kpe_past.shape[1]
    c_len = CHUNK if l % CHUNK == 0 else l
    n = l // c_len

    q, k, v, z, gb, hist8 = _gdn_in(x, mods[0], pk['gdn_w_main'], pk['gdn_w_ba'], _pad_hist(gdn_conv[:, 0]),
                                    pk['gdn_conv_w'], pk['gdn_alog'], pk['gdn_dtb'], c_len)
    col = gb.reshape(b, n, c_len, 2 * GDN_HEADS)
    row = jnp.swapaxes(col, 2, 3)
    o, s_end = _gdn_chunk(q, k, v, col, row, gdn_state[:, 0], c_len)
    gdn_conv_out = hist8[:, None, SUBLANES - (GDN_CONV - 1):]
    x, fh0 = _block_tail(o, z, pk['gdn_norm_g'], pk['gdn_w_out'], x, mods, ffn_conv[:, 0], pk, 0)

    proj = _mla_proj(x, mods[2], pk['kv_w_down'], pk['kv_norm_g'], _rope_table(past, l),
                     pk['mla_w_dq'], pk['mla_q_norm_g'], pk['mla_w_uq'], pk['kv_w_k'], pk['kv_w_vt'],
                     with_kv=(past == 0))
    ckv_new, kpe_new, qm = proj[:3]
    if past == 0:
        km, vm = proj[3:]
    else:
        ckv_all = jnp.concatenate([ckv_past, ckv_new], axis=1)
        kpe_all = jnp.concatenate([kpe_past, kpe_new], axis=1)
        kpe_pad = jnp.pad(kpe_all, ((0, 0), (0, 0), (0, LANES - MLA_ROPE)))
        km, vm = _kv_up(ckv_all, kpe_pad, pk['kv_w_k'], pk['kv_w_vt'])
    o = _attention(qm, km, vm, past)
    x, fh1 = _block_tail(o, None, None, pk['mla_w_out'], x, mods, ffn_conv[:, 1], pk, 1)

    return (x, s_end[:, None], gdn_conv_out, jnp.stack([fh0, fh1], axis=1), ckv_new, kpe_new)


def kernel(x_prompt, x_sample, c_prompt, c_sample, state_gdn, state_gdn_conv, state_ffn_conv, cache_ckv, cache_kpe, ada_w, ada_b, ln_g, ln_b, gdn_w_in, gdn_conv_w, gdn_a_log, gdn_dt_bias, gdn_norm_g, gdn_w_out, kv_w_down, kv_norm_g, kv_w_up, mla_w_dq, mla_q_norm_g, mla_w_uq, mla_w_out, ffn_w_in, ffn_b_in, ffn_conv_w, ffn_conv_b, ffn_w_down):
    p = {'ln_g': ln_g, 'ln_b': ln_b, 'gdn_w_in': gdn_w_in, 'gdn_conv_w': gdn_conv_w, 'gdn_a_log': gdn_a_log,
         'gdn_dt_bias': gdn_dt_bias, 'gdn_norm_g': gdn_norm_g, 'gdn_w_out': gdn_w_out,
         'kv_w_down': kv_w_down, 'kv_norm_g': kv_norm_g, 'kv_w_up': kv_w_up,
         'mla_w_dq': mla_w_dq, 'mla_q_norm_g': mla_q_norm_g, 'mla_w_uq': mla_w_uq, 'mla_w_out': mla_w_out,
         'ffn_w_in': ffn_w_in, 'ffn_b_in': ffn_b_in, 'ffn_conv_w': ffn_conv_w,
         'ffn_conv_b': ffn_conv_b, 'ffn_w_down': ffn_w_down}
    pk = _pack_weights(p)
    bp = x_prompt.shape[0]
    mods = _ada_terms(jnp.concatenate([c_prompt, c_sample], axis=0), ada_w, ada_b)[:, :, None, :]
    zeros_like_b = lambda a: jnp.zeros((bp,) + a.shape[1:], a.dtype)
    out_p = _trunk(x_prompt, mods[:, :bp], zeros_like_b(state_gdn), zeros_like_b(state_gdn_conv),
                   zeros_like_b(state_ffn_conv), jnp.zeros((bp, 0, MLA_KV_LORA), cache_ckv.dtype),
                   jnp.zeros((bp, 0, MLA_ROPE), cache_kpe.dtype), pk)
    out_s = _trunk(x_sample, mods[:, bp:], state_gdn, state_gdn_conv, state_ffn_conv, cache_ckv, cache_kpe, pk)
    return (out_p[0], out_s[0]) + out_p[1:] + out_s[1:]
```

```python
import functools

import jax
import jax.numpy as jnp
from jax import lax
from jax.experimental import pallas as pl
from jax.experimental.pallas import tpu as pltpu

F32 = jnp.float32
BF16 = jnp.bfloat16

D_MODEL = 1024
DEPTH = 2
CHUNK = 64
ALPHA = (2.0 * DEPTH) ** 0.25
LN_EPS = 1e-5
RMS_EPS = 1e-6
GDN_HEADS = 8
GDN_DK = 128
GDN_DV = 128
GDN_CONV = 4
GDN_QK = GDN_HEADS * GDN_DK
GDN_V = GDN_HEADS * GDN_DV
GDN_CONV_CH = 2 * GDN_QK + GDN_V
MLA_HEADS = 8
MLA_NOPE = 128
MLA_ROPE = 64
MLA_V = 128
MLA_KV_LORA = 256
MLA_Q_LORA = 384
ROPE_THETA = 10000.0
MLA_SCALE = (MLA_NOPE + MLA_ROPE) ** -0.5
Q_PRESCALE = MLA_SCALE * 1.4426950408889634
D_FF = 2816
FFN_CONV = 3

LANES = 128
SUBLANES = 8
MXU_DIM = 256
HEAD_PAD = 256
VMEM_LIMIT = 56 * 1024 * 1024
NEG_BIG = -1e30

NN = ((1,), (0,))
NT = ((1,), (1,))
TN = ((0,), (0,))


def _row_tile(n, cap):
    if n <= cap:
        return n
    for t in range(cap, 15, -1):
        if n % t == 0 and t % 16 == 0:
            return t
    raise ValueError(f"no row tile for {n}")


def _split3(a):
    hi = a.astype(BF16)
    r = a - hi.astype(F32)
    mid = r.astype(BF16)
    lo = (r - mid.astype(F32)).astype(BF16)
    return hi, mid, lo


def _dg(a, b, dims=NN):
    return lax.dot_general(a, b, (dims, ((), ())), preferred_element_type=F32)


def _dot_hp(a, b, dims=NN):
    a0, a1 = a.astype(BF16), (a - a.astype(BF16).astype(F32)).astype(BF16)
    b0, b1 = b.astype(BF16), (b - b.astype(BF16).astype(F32)).astype(BF16)
    return _dg(a0, b0, dims) + (_dg(a0, b1, dims) + _dg(a1, b0, dims))


def _sigmoid(x):
    return 1.0 / (1.0 + jnp.exp(-x))


def _silu(x):
    return x * _sigmoid(x)


def _layer_norm(y, g, b):
    mu = jnp.mean(y, axis=-1, keepdims=True)
    yc = y - mu
    var = jnp.mean(yc * yc, axis=-1, keepdims=True)
    return yc * lax.rsqrt(var + LN_EPS) * g + b


def _const_spec(shape):
    nd = len(shape)
    return pl.BlockSpec(shape, lambda *_: (0,) * nd, pipeline_mode=pl.Buffered(1))


def _shift_rows(u, carry8, j):
    rows, width = u.shape
    groups = rows // SUBLANES
    rot = pltpu.roll(u.reshape(groups, SUBLANES, width), j, axis=1)
    above = jnp.concatenate([pltpu.roll(carry8, j, axis=0)[None], rot[:groups - 1]], axis=0)
    sub = lax.broadcasted_iota(jnp.int32, rot.shape, 1)
    return jnp.where(sub < j, above, rot).reshape(rows, width)


def _params(sem):
    return pltpu.CompilerParams(dimension_semantics=sem, vmem_limit_bytes=VMEM_LIMIT)


def _ada_kernel(c_ref, w_ref, b_ref, o_ref):
    s = _silu(c_ref[...])
    o_ref[0] = _dot_hp(s, w_ref[0]) + b_ref[0]


def _ada_terms(c_all, ada_w, ada_b):
    n_sub, d, n3 = ada_w.shape
    bc = c_all.shape[0]
    tn = 768
    return pl.pallas_call(
        _ada_kernel,
        grid=(n_sub, n3 // tn),
        in_specs=[
            pl.BlockSpec((bc, d), lambda i, j: (0, 0)),
            pl.BlockSpec((1, d, tn), lambda i, j: (i, 0, j)),
            pl.BlockSpec((1, 1, tn), lambda i, j: (i, 0, j)),
        ],
        out_specs=pl.BlockSpec((1, bc, tn), lambda i, j: (i, 0, j)),
        out_shape=jax.ShapeDtypeStruct((n_sub, bc, n3), F32),
        compiler_params=_params(("arbitrary", "arbitrary")),
        name="ada_terms",
    )(c_all, ada_w, ada_b.reshape(n_sub, 1, n3))


GDN_IN_TILE = 256


def _gdn_in_kernel(x_ref, mod_ref, w_ref, wba_ref, hist_ref, cw_ref, alog_ref, dtb_ref, tri_ref,
                   q_ref, k_ref, v_ref, z_ref, gb_ref, hist_out_ref, carry, *, tm):
    t = pl.program_id(1)

    @pl.when(t == 0)
    def _():
        carry[...] = hist_ref[0]

    shift = mod_ref[0, :, 0:D_MODEL]
    scale = mod_ref[0, :, D_MODEL:2 * D_MODEL]
    hin = (x_ref[0] * (1.0 + scale) + shift).astype(BF16)

    outs = (q_ref, k_ref, v_ref)

    def conv_tile(sec, c):
        lo = sec * GDN_QK + c * GDN_IN_TILE
        cols = slice(lo, lo + GDN_IN_TILE)
        ocols = slice(c * GDN_IN_TILE, (c + 1) * GDN_IN_TILE)
        u = jnp.dot(hin, w_ref[:, cols], preferred_element_type=F32)
        c8 = carry[:, cols]
        acc = u * cw_ref[GDN_CONV - 1:GDN_CONV, cols]
        for j in range(1, GDN_CONV):
            acc = acc + _shift_rows(u, c8, j) * cw_ref[GDN_CONV - 1 - j:GDN_CONV - j, cols]
        carry[:, cols] = u[tm - SUBLANES:tm]
        s = _silu(acc)
        if sec == 2:
            v_ref[0, :, ocols] = s.astype(BF16)
        else:
            post = GDN_DK ** -0.5 if sec == 0 else 1.0
            parts = []
            for h in range(GDN_IN_TILE // GDN_DK):
                sh = s[:, h * GDN_DK:(h + 1) * GDN_DK]
                inv = lax.rsqrt(jnp.sum(sh * sh, axis=-1, keepdims=True) + RMS_EPS)
                parts.append(sh * (inv * post))
            outs[sec][0, :, ocols] = jnp.concatenate(parts, axis=-1).astype(BF16)

    def z_tile(c):
        ocols = slice(c * GDN_IN_TILE, (c + 1) * GDN_IN_TILE)
        cols = slice(GDN_CONV_CH + c * GDN_IN_TILE, GDN_CONV_CH + (c + 1) * GDN_IN_TILE)
        z_ref[0, :, ocols] = jnp.dot(hin, w_ref[:, cols], preferred_element_type=F32).astype(BF16)

    def gates():
        ba = jnp.dot(hin, wba_ref[...], preferred_element_type=F32)
        beta = _sigmoid(ba)
        xs = ba + dtb_ref[...]
        softplus = jnp.maximum(xs, 0.0) + jnp.log(1.0 + jnp.exp(-jnp.abs(xs)))
        g = -jnp.exp(alog_ref[...]) * softplus
        tri = tri_ref[...]
        g0, g1, g2 = _split3(g)
        gsum = (jnp.dot(tri, g0, preferred_element_type=F32)
                + jnp.dot(tri, g1, preferred_element_type=F32)
                + jnp.dot(tri, g2, preferred_element_type=F32))
        lane = lax.broadcasted_iota(jnp.int32, (tm, LANES), 1)
        gb_ref[0] = jnp.where(lane < GDN_HEADS, beta, gsum)[:, :2 * GDN_HEADS]

    for c in range(GDN_QK // GDN_IN_TILE):
        conv_tile(0, c)
        z_tile(c)
        conv_tile(1, c)
        if c == 0:
            gates()
        conv_tile(2, c)
    hist_out_ref[0] = carry[...]


def _gdn_in(x, mod, w_main, w_ba, hist8, conv_w, alog_row, dtb_row, c_len):
    b, l, d = x.shape
    tm = _row_tile(l, 512)
    assert tm % c_len == 0
    idx = jnp.arange(tm, dtype=jnp.int32)
    tri = ((idx[:, None] >= idx[None, :]) & ((idx[:, None] // c_len) == (idx[None, :] // c_len))).astype(BF16)
    tok = lambda w: pl.BlockSpec((1, tm, w), lambda i, t: (i, t, 0))
    return pl.pallas_call(
        functools.partial(_gdn_in_kernel, tm=tm),
        grid=(b, l // tm),
        in_specs=[
            tok(d),
            pl.BlockSpec((1, 1, 3 * d), lambda i, t: (i, 0, 0)),
            _const_spec(w_main.shape),
            _const_spec(w_ba.shape),
            pl.BlockSpec((1, SUBLANES, GDN_CONV_CH), lambda i, t: (i, 0, 0)),
            _const_spec(conv_w.shape),
            _const_spec(alog_row.shape),
            _const_spec(dtb_row.shape),
            _const_spec(tri.shape),
        ],
        out_specs=[tok(GDN_QK), tok(GDN_QK), tok(GDN_V), tok(GDN_V), tok(2 * GDN_HEADS),
                   pl.BlockSpec((1, SUBLANES, GDN_CONV_CH), lambda i, t: (i, 0, 0))],
        out_shape=[
            jax.ShapeDtypeStruct((b, l, GDN_QK), BF16),
            jax.ShapeDtypeStruct((b, l, GDN_QK), BF16),
            jax.ShapeDtypeStruct((b, l, GDN_V), BF16),
            jax.ShapeDtypeStruct((b, l, GDN_V), BF16),
            jax.ShapeDtypeStruct((b, l, 2 * GDN_HEADS), F32),
            jax.ShapeDtypeStruct((b, SUBLANES, GDN_CONV_CH), F32),
        ],
        scratch_shapes=[pltpu.VMEM((SUBLANES, GDN_CONV_CH), F32)],
        compiler_params=_params(("arbitrary", "arbitrary")),
        name="gdn_in",
    )(x, mod, w_main, w_ba, hist8, conv_w, alog_row, dtb_row, tri)


GDN_GROUP = 4


def _bf(xs):
    return [x.astype(BF16) for x in xs]


def _drain(gen):
    for _ in gen:
        pass


def _interleave(major, n_major, minor, n_minor):
    done = 0
    for i in range(n_major):
        next(major, None)
        while done * n_major < (i + 1) * n_minor:
            next(minor, None)
            done += 1
    _drain(major)
    _drain(minor)


def _gdn_chunk_kernel(q_ref, k_ref, v_ref, col_ref, row_ref, s0_ref, o_ref, s_ref, *, c_len, group):
    @pl.when(pl.program_id(1) == 0)
    def _():
        s_ref[...] = s0_ref[...]

    ri = lax.broadcasted_iota(jnp.int32, (c_len, c_len), 0)
    ci = lax.broadcasted_iota(jnp.int32, (c_len, c_len), 1)
    incl = ri >= ci
    strict = ri > ci
    n_sq = c_len.bit_length() - 2
    heads = range(GDN_HEADS)
    rows = lambda g: slice(g * c_len, (g + 1) * c_len)
    lanes = lambda h: slice(h * GDN_DK, (h + 1) * GDN_DK)

    def prepare(chunks, w):
        items = [(g, h) for g in chunks for h in heads]
        each = range(len(items))
        col = {g: col_ref[0, g] for g in chunks}
        row = {g: row_ref[0, g] for g in chunks}
        w['qb'] = [q_ref[0, rows(g), lanes(h)] for g, h in items]
        w['kb'] = [k_ref[0, rows(g), lanes(h)] for g, h in items]
        w['k'] = [x.astype(F32) for x in w['kb']]
        w['v'] = [v_ref[0, rows(g), lanes(h)].astype(F32) for g, h in items]
        w['beta'] = [col[g][:, h:h + 1] for g, h in items]
        w['gc'] = [col[g][:, GDN_HEADS + h:GDN_HEADS + h + 1] for g, h in items]
        gr = [row[g][GDN_HEADS + h:GDN_HEADS + h + 1, :] for g, h in items]
        yield
        w['gam'] = [jnp.where(incl, jnp.exp(jnp.where(incl, w['gc'][i] - gr[i], 0.0)), 0.0) for i in each]
        w['eg'] = [jnp.exp(w['gc'][i]) for i in each]
        yield
        w['sol'] = [jnp.concatenate([w['beta'][i] * w['v'][i], w['beta'][i] * w['k'][i] * w['eg'][i]], axis=-1)
                    for i in each]
        g_last = [w['gc'][i][c_len - 1:c_len, :] for i in each]
        w['k_dec'] = _bf([w['k'][i] * jnp.exp(g_last[i] - w['gc'][i]) for i in each])
        w['decay'] = [jnp.exp(g_last[i]) for i in each]
        yield

    n_prepare = 3

    def solve(w):
        each = range(len(w['qb']))
        qkk = [_dg(jnp.concatenate([w['qb'][i], w['kb'][i]], axis=0), w['kb'][i], NT) for i in each]
        yield
        p = _bf([jnp.where(strict, -(w['beta'][i] * qkk[i][c_len:] * w['gam'][i]), 0.0) for i in each])
        w['att'] = _bf([qkk[i][:c_len] * w['gam'][i] for i in each])
        sol = w['sol']
        solb = _bf(sol)
        sol = [sol[i] + _dg(p[i], solb[i]) for i in each]
        yield
        for _ in range(n_sq):
            p = _bf([_dg(p[i], p[i]) for i in each])
            yield
            solb = _bf(sol)
            sol = [sol[i] + _dg(p[i], solb[i]) for i in each]
            yield
        w['u'] = [sol[i][:, :GDN_DV] for i in each]
        w['wq'] = [jnp.concatenate([sol[i][:, GDN_DV:].astype(BF16),
                                    (w['qb'][i].astype(F32) * w['eg'][i]).astype(BF16)], axis=0)
                   for i in each]

    n_solve = 2 + 2 * n_sq

    state = {'s': [s_ref[0, h] for h in heads]}

    def recur(chunks, w):
        for n, g in enumerate(chunks):
            it = [n * GDN_HEADS + h for h in heads]
            sb = _bf(state['s'])
            ws = [_dg(w['wq'][it[h]], sb[h]) for h in heads]
            yield
            vb = _bf([w['u'][it[h]] - ws[h][:c_len] for h in heads])
            o = [ws[h][c_len:] + _dg(w['att'][it[h]], vb[h]) for h in heads]
            state['s'] = [w['decay'][it[h]] * state['s'][h] + _dg(w['k_dec'][it[h]], vb[h], TN) for h in heads]
            yield
            o_ref[0, rows(g), :] = jnp.concatenate(o, axis=-1).astype(BF16)

    n_waves = 2 if group % 2 == 0 else 1
    per = group // n_waves
    waves = [list(range(n * per, (n + 1) * per)) for n in range(n_waves)]
    data = [{} for _ in waves]
    _drain(prepare(waves[0], data[0]))
    for n in range(n_waves):
        side = []
        n_side = 0
        if n + 1 < n_waves:
            side.append(prepare(waves[n + 1], data[n + 1]))
            n_side += n_prepare
        if n >= 1:
            side.append(recur(waves[n - 1], data[n - 1]))
            n_side += 2 * per

        def chain(gens=side):
            for gen in gens:
                yield from gen

        _interleave(solve(data[n]), n_solve, chain(), n_side)
    _drain(recur(waves[-1], data[-1]))
    for h in heads:
        s_ref[0, h] = state['s'][h]


def _gdn_chunk(q, k, v, col, row, s0, c_len):
    b, l, _ = q.shape
    n = l // c_len
    group = GDN_GROUP if n % GDN_GROUP == 0 else 1
    rows = c_len * group
    tok = pl.BlockSpec((1, rows, GDN_QK), lambda i, t: (i, t, 0))
    st = pl.BlockSpec((1, GDN_HEADS, GDN_DK, GDN_DV), lambda i, t: (i, 0, 0, 0))
    return pl.pallas_call(
        functools.partial(_gdn_chunk_kernel, c_len=c_len, group=group),
        grid=(b, n // group),
        in_specs=[tok, tok, tok,
                  pl.BlockSpec((1, group, c_len, 2 * GDN_HEADS), lambda i, t: (i, t, 0, 0)),
                  pl.BlockSpec((1, group, 2 * GDN_HEADS, c_len), lambda i, t: (i, t, 0, 0)),
                  st],
        out_specs=[tok, st],
        out_shape=[jax.ShapeDtypeStruct((b, l, GDN_V), BF16),
                   jax.ShapeDtypeStruct(s0.shape, F32)],
        compiler_params=_params(("arbitrary", "arbitrary")),
        name="gdn_chunk",
    )(q, k, v, col, row, s0)


FFN_TILE = MXU_DIM
FFN_ROWS = 512
FFN_SUB_TILES = 2


def _mix_ffn_kernel(*refs, gated, tm, n_sub):
    if gated:
        o_ref, z_ref, ng_ref = refs[:3]
        refs = refs[3:]
    else:
        o_ref = refs[0]
        refs = refs[1:]
    (x_ref, mmod_ref, wo_ref, mlg_ref, mlb_ref, fmod_ref, win_ref, bin_ref, cw_ref, cb_ref, hist_ref, wdn_ref,
     lg_ref, lb_ref, y_ref, hist_out_ref, carry, x1_ref, hbuf, fbuf) = refs
    t = pl.program_id(1)

    @pl.when(t == 0)
    def _():
        carry[...] = hist_ref[0]

    mgate = mmod_ref[0, :, 2 * D_MODEL:3 * D_MODEL]
    shift = fmod_ref[0, :, 0:D_MODEL]
    scale = fmod_ref[0, :, D_MODEL:2 * D_MODEL]
    gate = fmod_ref[0, :, 2 * D_MODEL:3 * D_MODEL]
    sub = tm // n_sub
    subs = range(n_sub)
    rows = lambda s: slice(s * sub, (s + 1) * sub)

    def mixer_input(s):
        if not gated:
            return
        for h in range(GDN_HEADS):
            hs = slice(h * GDN_DV, (h + 1) * GDN_DV)
            oh = o_ref[0, rows(s), hs].astype(F32)
            r = oh * lax.rsqrt(jnp.mean(oh * oh, axis=-1, keepdims=True) + RMS_EPS) * ng_ref[...]
            hbuf[s, :, hs] = (r * _silu(z_ref[0, rows(s), hs].astype(F32))).astype(BF16)

    def mixer_proj(s):
        a = hbuf[s] if gated else o_ref[0, rows(s), :]
        return jnp.dot(a, wo_ref[...], preferred_element_type=F32)

    def mixer_norm(s, mix):
        x1_ref[s] = _layer_norm(ALPHA * x_ref[0, rows(s), :] + (1.0 + mgate) * mix, mlg_ref[...], mlb_ref[...])
        hbuf[s] = (x1_ref[s] * (1.0 + scale) + shift).astype(BF16)

    def conv_half(s, lo):
        cols = slice(lo, lo + FFN_TILE)
        u = jnp.dot(hbuf[s], win_ref[:, cols], preferred_element_type=F32) + bin_ref[:, cols]
        c8 = carry[:, cols]
        out = u * cw_ref[FFN_CONV - 1:FFN_CONV, cols] + cb_ref[:, cols]
        for j in range(1, FFN_CONV):
            out = out + _shift_rows(u, c8, j) * cw_ref[FFN_CONV - 1 - j:FFN_CONV - j, cols]
        carry[:, cols] = u[sub - SUBLANES:sub]
        return out

    def ffn_norm(s, f):
        y_ref[0, rows(s), :] = _layer_norm(ALPHA * x1_ref[s] + (1.0 + gate) * f, lg_ref[...], lb_ref[...])

    mixer_input(0)
    mix = {}
    for s in subs:
        mix[s] = mixer_proj(s)
        if s + 1 < n_sub:
            mixer_input(s + 1)
        if s >= 1:
            mixer_norm(s - 1, mix.pop(s - 1))
    mixer_norm(n_sub - 1, mix.pop(n_sub - 1))

    for i in range(D_FF // FFN_TILE):
        lo = i * FFN_TILE
        for s in subs:
            ua = conv_half(s, lo)
            ub = conv_half(s, lo + D_FF)
            fbuf[s, :, lo:lo + FFN_TILE] = (_silu(ua) * ub).astype(BF16)
    hist_out_ref[0] = carry[...]

    down = {}
    for s in subs:
        down[s] = jnp.dot(fbuf[s], wdn_ref[...], preferred_element_type=F32)
        if s >= 1:
            ffn_norm(s - 1, down.pop(s - 1))
    ffn_norm(n_sub - 1, down.pop(n_sub - 1))


def _layer_spec(stacked, layer):
    nd = stacked.ndim - 1
    return pl.BlockSpec((None,) + stacked.shape[1:], lambda *_: (layer,) + (0,) * nd, pipeline_mode=pl.Buffered(1))


def _mix_ffn(o, z, norm_g, x, mix_mod, w_out, mix_ln_g, mix_ln_b,
             ffn_mod, w_in_all, b_in, conv_w, conv_b, hist8, w_down_all, ln_g, ln_b, layer):
    b, l, d = x.shape
    tm = _row_tile(l, FFN_ROWS)
    n_sub = FFN_SUB_TILES if tm == FFN_ROWS else 1
    gated = z is not None
    tok = pl.BlockSpec((1, tm, d), lambda i, t: (i, t, 0))
    modspec = pl.BlockSpec((1, 1, 3 * d), lambda i, t: (i, 0, 0))
    hspec = pl.BlockSpec((1, SUBLANES, 2 * D_FF), lambda i, t: (i, 0, 0))
    row = _const_spec((1, d))
    head_args, head_specs = ((o, z, norm_g), [tok, tok, _const_spec(norm_g.shape)]) if gated else ((o,), [tok])
    return pl.pallas_call(
        functools.partial(_mix_ffn_kernel, gated=gated, tm=tm, n_sub=n_sub),
        grid=(b, l // tm),
        in_specs=head_specs + [tok, modspec, _const_spec(w_out.shape), row, row,
                               modspec, _layer_spec(w_in_all, layer), _const_spec(b_in.shape),
                               _const_spec(conv_w.shape), _const_spec(conv_b.shape), hspec,
                               _layer_spec(w_down_all, layer), row, row],
        out_specs=[tok, hspec],
        out_shape=[jax.ShapeDtypeStruct((b, l, d), F32),
                   jax.ShapeDtypeStruct((b, SUBLANES, 2 * D_FF), F32)],
        scratch_shapes=[pltpu.VMEM((SUBLANES, 2 * D_FF), F32), pltpu.VMEM((n_sub, tm // n_sub, d), F32),
                        pltpu.VMEM((n_sub, tm // n_sub, d), BF16), pltpu.VMEM((n_sub, tm // n_sub, D_FF), BF16)],
        compiler_params=_params(("arbitrary", "arbitrary")),
        name="mix_ffn_gated" if gated else "mix_ffn",
    )(*head_args, x, mix_mod, w_out, mix_ln_g, mix_ln_b,
      ffn_mod, w_in_all, b_in, conv_w, conv_b, hist8, w_down_all, ln_g, ln_b)


def _rope_pair(pair, cs):
    prod = pair * cs
    return prod + pltpu.roll(prod, MLA_ROPE, axis=1)


def _mla_proj_kernel(x_ref, mod_ref, wkv_ref, kvg_ref, cs_ref, wdq_ref, qg_ref, wuq_ref, *rest, with_kv):
    if with_kv:
        wk_ref, wvt_ref, ckv_ref, kpe_ref, q_ref, k_ref, vt_ref = rest
    else:
        ckv_ref, kpe_ref, q_ref = rest
    x = x_ref[0]
    cs = cs_ref[...]
    lane = lax.broadcasted_iota(jnp.int32, (x.shape[0], LANES), 1)
    kv = jnp.dot(x.astype(BF16), wkv_ref[...], preferred_element_type=F32)
    lat = kv[:, :MLA_KV_LORA]
    ckv = lat * lax.rsqrt(jnp.mean(lat * lat, axis=-1, keepdims=True) + RMS_EPS) * kvg_ref[...]
    kpe = _rope_pair(kv[:, MLA_KV_LORA:], cs)
    ckv_ref[0] = ckv
    kpe_ref[0] = kpe[:, :MLA_ROPE]
    if with_kv:
        lat_b = ckv.astype(BF16)
        kpe_b = jnp.where(lane < MLA_ROPE, kpe, 0.0).astype(BF16)

    shift = mod_ref[0, :, 0:D_MODEL]
    scale = mod_ref[0, :, D_MODEL:2 * D_MODEL]
    hin = (x * (1.0 + scale) + shift).astype(BF16)
    qd = jnp.dot(hin, wdq_ref[...], preferred_element_type=F32)
    qd = qd * lax.rsqrt(jnp.mean(qd * qd, axis=-1, keepdims=True) + RMS_EPS) * qg_ref[...]
    qd = qd.astype(BF16)
    for h in range(MLA_HEADS):
        qh = jnp.dot(qd, wuq_ref[:, h * HEAD_PAD:(h + 1) * HEAD_PAD], preferred_element_type=F32)
        q_ref[0, :, h * HEAD_PAD:h * HEAD_PAD + MLA_NOPE] = (qh[:, :MLA_NOPE] * Q_PRESCALE).astype(BF16)
        pe = jnp.where(lane < MLA_ROPE, _rope_pair(qh[:, MLA_NOPE:], cs) * Q_PRESCALE, 0.0)
        q_ref[0, :, h * HEAD_PAD + MLA_NOPE:(h + 1) * HEAD_PAD] = pe.astype(BF16)
        if with_kv:
            _kv_head(h, lat_b, kpe_b, wk_ref, wvt_ref, k_ref, vt_ref)


def _mla_proj(x, mod, w_kv, kv_g, cossin, w_dq, q_g, w_uq, w_k, w_vt, with_kv):
    b, l, d = x.shape
    tm = _row_tile(l, 512)
    tok = lambda w: pl.BlockSpec((1, tm, w), lambda i, t: (i, t, 0))
    in_specs = [tok(d), pl.BlockSpec((1, 1, 3 * d), lambda i, t: (i, 0, 0)),
                _const_spec(w_kv.shape), _const_spec(kv_g.shape),
                pl.BlockSpec((tm, LANES), lambda i, t: (t, 0)),
                _const_spec(w_dq.shape), _const_spec(q_g.shape), _const_spec(w_uq.shape)]
    out_specs = [tok(MLA_KV_LORA), tok(MLA_ROPE), tok(MLA_HEADS * HEAD_PAD)]
    out_shape = [jax.ShapeDtypeStruct((b, l, MLA_KV_LORA), F32),
                 jax.ShapeDtypeStruct((b, l, MLA_ROPE), F32),
                 jax.ShapeDtypeStruct((b, l, MLA_HEADS * HEAD_PAD), BF16)]
    args = (x, mod, w_kv, kv_g, cossin, w_dq, q_g, w_uq)
    if with_kv:
        in_specs += [_const_spec(w_k.shape), _const_spec(w_vt.shape)]
        out_specs += [tok(MLA_HEADS * HEAD_PAD), pl.BlockSpec((1, MLA_HEADS * MLA_V, tm), lambda i, t: (i, 0, t))]
        out_shape += [jax.ShapeDtypeStruct((b, l, MLA_HEADS * HEAD_PAD), BF16),
                      jax.ShapeDtypeStruct((b, MLA_HEADS * MLA_V, l), BF16)]
        args += (w_k, w_vt)
    return pl.pallas_call(
        functools.partial(_mla_proj_kernel, with_kv=with_kv),
        grid=(b, l // tm),
        in_specs=in_specs,
        out_specs=out_specs,
        out_shape=out_shape,
        compiler_params=_params(("arbitrary", "arbitrary")),
        name="mla_proj_kv" if with_kv else "mla_proj",
    )(*args)


def _kv_head(h, lat, kpe, wk_ref, wvt_ref, k_ref, vt_ref):
    kn = jnp.dot(lat, wk_ref[:, h * MLA_NOPE:(h + 1) * MLA_NOPE], preferred_element_type=F32)
    k_ref[0, :, h * HEAD_PAD:h * HEAD_PAD + MLA_NOPE] = kn.astype(BF16)
    k_ref[0, :, h * HEAD_PAD + MLA_NOPE:(h + 1) * HEAD_PAD] = kpe
    vt_ref[0, h * MLA_V:(h + 1) * MLA_V, :] = _dg(wvt_ref[h], lat, NT).astype(BF16)


def _kv_up_kernel(ckv_ref, kpe_ref, wk_ref, wvt_ref, k_ref, vt_ref):
    lat = ckv_ref[0].astype(BF16)
    kpe = kpe_ref[0].astype(BF16)
    for h in range(MLA_HEADS):
        _kv_head(h, lat, kpe, wk_ref, wvt_ref, k_ref, vt_ref)


def _kv_up(ckv_all, kpe_pad, w_k, w_vt):
    b, lk, _ = ckv_all.shape
    tm = ATTN_TK if lk % ATTN_TK == 0 else lk
    tok = lambda w: pl.BlockSpec((1, tm, w), lambda i, t: (i, t, 0))
    return pl.pallas_call(
        _kv_up_kernel,
        grid=(b, lk // tm),
        in_specs=[tok(MLA_KV_LORA), tok(LANES), _const_spec(w_k.shape), _const_spec(w_vt.shape)],
        out_specs=[tok(MLA_HEADS * HEAD_PAD),
                   pl.BlockSpec((1, MLA_HEADS * MLA_V, tm), lambda i, t: (i, 0, t))],
        out_shape=[jax.ShapeDtypeStruct((b, lk, MLA_HEADS * HEAD_PAD), BF16),
                   jax.ShapeDtypeStruct((b, MLA_HEADS * MLA_V, lk), BF16)],
        compiler_params=_params(("arbitrary", "arbitrary")),
        name="kv_up",
    )(ckv_all, kpe_pad, w_k, w_vt)


ATTN_TQ = 512
ATTN_TK = 512
ATTN_HEADS_PER_STEP = 4
CHUNK_SHIFT = CHUNK.bit_length() - 1


def _chunk_end(pos):
    return (lax.shift_right_logical(pos, CHUNK_SHIFT) + 1) * CHUNK


def _attn_kernel(q_ref, k_ref, vt_ref, o_ref, acc_ref, m_ref, l_ref, sa_ref, sb_ref, *, tq, tk, past, lk, nh):
    i = pl.program_id(2)
    acc_ref[...] = jnp.zeros_like(acc_ref)
    m_ref[...] = jnp.full_like(m_ref, NEG_BIG)
    l_ref[...] = jnp.zeros_like(l_ref)
    q0 = past + i * tq
    n_full = lax.div(jnp.minimum(_chunk_end(q0), lk), tk)
    n_end = lax.div(jnp.minimum(_chunk_end(q0 + tq - 1), lk) + tk - 1, tk)
    n_blk = lk // tk
    heads = range(nh)
    ones = jnp.ones((2 * SUBLANES, tk), BF16)

    def key_start(j):
        return pl.multiple_of(jnp.minimum(j, n_blk - 1) * tk, tk) if lk > tk else 0

    def scores(j, dst):
        k0 = key_start(j)
        for h in heads:
            dst[h] = _dg(k_ref[0, pl.ds(k0, tk), h * HEAD_PAD:(h + 1) * HEAD_PAD],
                         q_ref[0, :, h * HEAD_PAD:(h + 1) * HEAD_PAD], NT)

    def step(j, src, dst, masked):
        if dst is not None:
            scores(j + 1, dst)
        k0 = key_start(j)
        st = [src[h] for h in heads]
        if masked:
            kc = lax.shift_right_logical(k0 + lax.broadcasted_iota(jnp.int32, (tk, tq), 0), CHUNK_SHIFT)
            qc = lax.shift_right_logical(q0 + lax.broadcasted_iota(jnp.int32, (tk, tq), 1), CHUNK_SHIFT)
            visible = kc <= qc
            st = [jnp.where(visible, s, NEG_BIG) for s in st]
        m_old = [m_ref[h] for h in heads]
        m_new = [jnp.maximum(m_old[h], jnp.max(st[h], axis=0, keepdims=True)) for h in heads]
        alpha = [jnp.exp2(m_old[h] - m_new[h]) for h in heads]
        pt = [jnp.exp2(st[h] - m_new[h]).astype(BF16) for h in heads]
        pv = [_dg(jnp.concatenate([vt_ref[0, h * MLA_V:(h + 1) * MLA_V, pl.ds(k0, tk)], ones], axis=0), pt[h])
              for h in heads]
        for h in heads:
            acc_ref[h] = alpha[h] * acc_ref[h] + pv[h][:MLA_V]
            m_ref[h] = m_new[h]
            l_ref[h] = alpha[h] * l_ref[h] + pv[h][MLA_V:MLA_V + 1]

    if n_blk == 1:
        scores(0, sa_ref)

        @pl.when(n_full == 1)
        def _():
            step(0, sa_ref, None, False)

        @pl.when(n_full == 0)
        def _():
            step(0, sa_ref, None, True)
    else:
        odd = lax.rem(n_full, 2)

        @pl.when(odd == 1)
        def _():
            scores(0, sb_ref)
            step(0, sb_ref, sa_ref, False)

        @pl.when(odd == 0)
        def _():
            scores(0, sa_ref)

        def pair(p, c):
            j = odd + 2 * p
            step(j, sa_ref, sb_ref, False)
            step(j + 1, sb_ref, sa_ref, False)
            return c

        lax.fori_loop(0, lax.div(n_full, 2), pair, 0)

        @pl.when(n_end > n_full)
        def _():
            step(n_full, sa_ref, None, True)

        def rest(j, c):
            scores(j, sa_ref)
            step(j, sa_ref, None, True)
            return c

        lax.fori_loop(n_full + 1, n_end, rest, 0)
    for h in heads:
        o_ref[0, :, h * MLA_V:(h + 1) * MLA_V] = (acc_ref[h] / l_ref[h]).T.astype(BF16)


def _attention(q, k, vt, past):
    b, lq, _ = q.shape
    lk = k.shape[1]
    lq_pad = -(-lq // LANES) * LANES
    if lq_pad != lq:
        q = jnp.pad(q, ((0, 0), (0, lq_pad - lq), (0, 0)))
    tq = _row_tile(lq_pad, ATTN_TQ)
    tk = ATTN_TK if lk % ATTN_TK == 0 else lk
    nh = ATTN_HEADS_PER_STEP
    kv_bytes = nh * lk * (HEAD_PAD + MLA_V) * jnp.dtype(BF16).itemsize
    kv_mode = pl.Buffered(2 if 2 * kv_bytes <= VMEM_LIMIT // 4 else 1)
    out = pl.pallas_call(
        functools.partial(_attn_kernel, tq=tq, tk=tk, past=past, lk=lk, nh=nh),
        grid=(b, MLA_HEADS // nh, lq_pad // tq),
        in_specs=[pl.BlockSpec((1, tq, nh * HEAD_PAD), lambda bi, h, i: (bi, i, h)),
                  pl.BlockSpec((1, lk, nh * HEAD_PAD), lambda bi, h, i: (bi, 0, h), pipeline_mode=kv_mode),
                  pl.BlockSpec((1, nh * MLA_V, lk), lambda bi, h, i: (bi, h, 0), pipeline_mode=kv_mode)],
        out_specs=pl.BlockSpec((1, tq, nh * MLA_V), lambda bi, h, i: (bi, i, h)),
        out_shape=jax.ShapeDtypeStruct((b, lq_pad, MLA_HEADS * MLA_V), BF16),
        scratch_shapes=[pltpu.VMEM((nh, MLA_V, tq), F32), pltpu.VMEM((nh, 1, tq), F32),
                        pltpu.VMEM((nh, 1, tq), F32), pltpu.VMEM((nh, tk, tq), F32),
                        pltpu.VMEM((nh, tk, tq), F32)],
        compiler_params=_params(("arbitrary", "arbitrary", "arbitrary")),
        name="chunk_causal_attention",
    )(q, k, vt)
    return out[:, :lq]


def _pad_hist(hist):
    return jnp.pad(hist, ((0, 0), (SUBLANES - hist.shape[1], 0), (0, 0)))


def _rot_half_cols(w):
    w1, w2 = jnp.split(w, 2, axis=-1)
    return jnp.concatenate([-w2, w1], axis=-1)


def _pack_weights(p):
    d = D_MODEL
    pk = {}
    w_in = p['gdn_w_in'][0]
    pk['gdn_w_main'] = w_in.astype(BF16)
    pk['gdn_w_ba'] = jnp.pad(w_in[:, GDN_CONV_CH + GDN_V:], ((0, 0), (0, LANES - 2 * GDN_HEADS))).astype(BF16)
    pad_gate = lambda a: jnp.pad(a.reshape(1, GDN_HEADS), ((0, 0), (GDN_HEADS, LANES - 2 * GDN_HEADS)))
    pk['gdn_alog'] = pad_gate(p['gdn_a_log'][0])
    pk['gdn_dtb'] = pad_gate(p['gdn_dt_bias'][0])
    pk['gdn_conv_w'] = p['gdn_conv_w'][0]
    pk['gdn_norm_g'] = p['gdn_norm_g'][0].reshape(1, GDN_DV)
    pk['gdn_w_out'] = p['gdn_w_out'][0].astype(BF16)
    wkv = p['kv_w_down']
    pk['kv_w_down'] = jnp.concatenate([wkv, _rot_half_cols(wkv[:, MLA_KV_LORA:])], axis=-1).astype(BF16)
    pk['kv_norm_g'] = p['kv_norm_g'].reshape(1, MLA_KV_LORA)
    w_up = p['kv_w_up'].reshape(MLA_KV_LORA, MLA_HEADS, MLA_NOPE + MLA_V)
    pk['kv_w_k'] = w_up[..., :MLA_NOPE].reshape(MLA_KV_LORA, MLA_HEADS * MLA_NOPE).astype(BF16)
    pk['kv_w_vt'] = jnp.transpose(w_up[..., MLA_NOPE:], (1, 2, 0)).astype(BF16)
    pk['mla_w_dq'] = p['mla_w_dq'][0].astype(BF16)
    pk['mla_q_norm_g'] = p['mla_q_norm_g'][0].reshape(1, MLA_Q_LORA)
    wuq = p['mla_w_uq'][0].reshape(MLA_Q_LORA, MLA_HEADS, MLA_NOPE + MLA_ROPE)
    wuq = jnp.concatenate([wuq, _rot_half_cols(wuq[..., MLA_NOPE:])], axis=-1)
    pk['mla_w_uq'] = wuq.reshape(MLA_Q_LORA, MLA_HEADS * HEAD_PAD).astype(BF16)
    pk['mla_w_out'] = p['mla_w_out'][0].astype(BF16)
    pk['ffn_w_in'] = p['ffn_w_in'].astype(BF16)
    pk['ffn_b_in'] = p['ffn_b_in'].reshape(DEPTH, 1, 2 * D_FF)
    pk['ffn_conv_w'] = p['ffn_conv_w']
    pk['ffn_conv_b'] = p['ffn_conv_b'].reshape(DEPTH, 1, 2 * D_FF)
    pk['ffn_w_down'] = p['ffn_w_down'].astype(BF16)
    pk['ln_g'] = p['ln_g'].reshape(2 * DEPTH, 1, d)
    pk['ln_b'] = p['ln_b'].reshape(2 * DEPTH, 1, d)
    return pk


def _rope_table(past, l):
    inv = 1.0 / (ROPE_THETA ** (jnp.arange(0, MLA_ROPE, 2, dtype=F32) / MLA_ROPE))
    ang = (past + jnp.arange(l, dtype=jnp.int32)).astype(F32)[:, None] * inv[None, :]
    ang = jnp.concatenate([ang, ang], axis=-1)
    return jnp.concatenate([jnp.cos(ang), jnp.sin(ang)], axis=-1)


def _block_tail(o, z, norm_g, w_out, x, mods, hist, pk, layer):
    y, hist8 = _mix_ffn(o, z, norm_g, x, mods[2 * layer], w_out, pk['ln_g'][2 * layer], pk['ln_b'][2 * layer],
                        mods[2 * layer + 1], pk['ffn_w_in'], pk['ffn_b_in'][layer],
                        pk['ffn_conv_w'][layer], pk['ffn_conv_b'][layer], _pad_hist(hist),
                        pk['ffn_w_down'], pk['ln_g'][2 * layer + 1], pk['ln_b'][2 * layer + 1], layer)
    return y, hist8[:, SUBLANES - (FFN_CONV - 1):]


def _trunk(x, mods, gdn_state, gdn_conv, ffn_conv, ckv_past, kpe_past, pk):
    b, l, _ = x.shape
    past = kpe_past.shape[1]
    c_len = CHUNK if l % CHUNK == 0 else l
    n = l // c_len

    q, k, v, z, gb, hist8 = _gdn_in(x, mods[0], pk['gdn_w_main'], pk['gdn_w_ba'], _pad_hist(gdn_conv[:, 0]),
                                    pk['gdn_conv_w'], pk['gdn_alog'], pk['gdn_dtb'], c_len)
    col = gb.reshape(b, n, c_len, 2 * GDN_HEADS)
    row = jnp.swapaxes(col, 2, 3)
    o, s_end = _gdn_chunk(q, k, v, col, row, gdn_state[:, 0], c_len)
    gdn_conv_out = hist8[:, None, SUBLANES - (GDN_CONV - 1):]
    x, fh0 = _block_tail(o, z, pk['gdn_norm_g'], pk['gdn_w_out'], x, mods, ffn_conv[:, 0], pk, 0)

    proj = _mla_proj(x, mods[2], pk['kv_w_down'], pk['kv_norm_g'], _rope_table(past, l),
                     pk['mla_w_dq'], pk['mla_q_norm_g'], pk['mla_w_uq'], pk['kv_w_k'], pk['kv_w_vt'],
                     with_kv=(past == 0))
    ckv_new, kpe_new, qm = proj[:3]
    if past == 0:
        km, vm = proj[3:]
    else:
        ckv_all = jnp.concatenate([ckv_past, ckv_new], axis=1)
        kpe_all = jnp.concatenate([kpe_past, kpe_new], axis=1)
        kpe_pad = jnp.pad(kpe_all, ((0, 0), (0, 0), (0, LANES - MLA_ROPE)))
        km, vm = _kv_up(ckv_all, kpe_pad, pk['kv_w_k'], pk['kv_w_vt'])
    o = _attention(qm, km, vm, past)
    x, fh1 = _block_tail(o, None, None, pk['mla_w_out'], x, mods, ffn_conv[:, 1], pk, 1)

    return (x, s_end[:, None], gdn_conv_out, jnp.stack([fh0, fh1], axis=1), ckv_new, kpe_new)


def kernel(x_prompt, x_sample, c_prompt, c_sample, state_gdn, state_gdn_conv, state_ffn_conv, cache_ckv, cache_kpe, ada_w, ada_b, ln_g, ln_b, gdn_w_in, gdn_conv_w, gdn_a_log, gdn_dt_bias, gdn_norm_g, gdn_w_out, kv_w_down, kv_norm_g, kv_w_up, mla_w_dq, mla_q_norm_g, mla_w_uq, mla_w_out, ffn_w_in, ffn_b_in, ffn_conv_w, ffn_conv_b, ffn_w_down):
    p = {'ln_g': ln_g, 'ln_b': ln_b, 'gdn_w_in': gdn_w_in, 'gdn_conv_w': gdn_conv_w, 'gdn_a_log': gdn_a_log,
         'gdn_dt_bias': gdn_dt_bias, 'gdn_norm_g': gdn_norm_g, 'gdn_w_out': gdn_w_out,
         'kv_w_down': kv_w_down, 'kv_norm_g': kv_norm_g, 'kv_w_up': kv_w_up,
         'mla_w_dq': mla_w_dq, 'mla_q_norm_g': mla_q_norm_g, 'mla_w_uq': mla_w_uq, 'mla_w_out': mla_w_out,
         'ffn_w_in': ffn_w_in, 'ffn_b_in': ffn_b_in, 'ffn_conv_w': ffn_conv_w,
         'ffn_conv_b': ffn_conv_b, 'ffn_w_down': ffn_w_down}
    pk = _pack_weights(p)
    bp = x_prompt.shape[0]
    mods = _ada_terms(jnp.concatenate([c_prompt, c_sample], axis=0), ada_w, ada_b)[:, :, None, :]
    zeros_like_b = lambda a: jnp.zeros((bp,) + a.shape[1:], a.dtype)
    out_p = _trunk(x_prompt, mods[:, :bp], zeros_like_b(state_gdn), zeros_like_b(state_gdn_conv),
                   zeros_like_b(state_ffn_conv), jnp.zeros((bp, 0, MLA_KV_LORA), cache_ckv.dtype),
                   jnp.zeros((bp, 0, MLA_ROPE), cache_kpe.dtype), pk)
    out_s = _trunk(x_sample, mods[:, bp:], state_gdn, state_gdn_conv, state_ffn_conv, cache_ckv, cache_kpe, pk)
    return (out_p[0], out_s[0]) + out_p[1:] + out_s[1:]
```

```python
import functools

import jax
import jax.numpy as jnp
from jax import lax
from jax.experimental import pallas as pl
from jax.experimental.pallas import tpu as pltpu

F32 = jnp.float32
BF16 = jnp.bfloat16

D_MODEL = 1024
DEPTH = 2
CHUNK = 64
ALPHA = (2.0 * DEPTH) ** 0.25
LN_EPS = 1e-5
RMS_EPS = 1e-6
GDN_HEADS = 8
GDN_DK = 128
GDN_DV = 128
GDN_CONV = 4
GDN_QK = GDN_HEADS * GDN_DK
GDN_V = GDN_HEADS * GDN_DV
GDN_CONV_CH = 2 * GDN_QK + GDN_V
MLA_HEADS = 8
MLA_NOPE = 128
MLA_ROPE = 64
MLA_V = 128
MLA_KV_LORA = 256
MLA_Q_LORA = 384
ROPE_THETA = 10000.0
MLA_SCALE = (MLA_NOPE + MLA_ROPE) ** -0.5
Q_PRESCALE = MLA_SCALE * 1.4426950408889634
D_FF = 2816
FFN_CONV = 3

LANES = 128
SUBLANES = 8
MXU_DIM = 256
HEAD_PAD = 256
VMEM_LIMIT = 56 * 1024 * 1024
NEG_BIG = -1e30

NN = ((1,), (0,))
NT = ((1,), (1,))
TN = ((0,), (0,))


def _row_tile(n, cap):
    if n <= cap:
        return n
    for t in range(cap, 15, -1):
        if n % t == 0 and t % 16 == 0:
            return t
    raise ValueError(f"no row tile for {n}")


def _split3(a):
    hi = a.astype(BF16)
    r = a - hi.astype(F32)
    mid = r.astype(BF16)
    lo = (r - mid.astype(F32)).astype(BF16)
    return hi, mid, lo


def _dg(a, b, dims=NN):
    return lax.dot_general(a, b, (dims, ((), ())), preferred_element_type=F32)


def _dot_hp(a, b, dims=NN):
    a0, a1 = a.astype(BF16), (a - a.astype(BF16).astype(F32)).astype(BF16)
    b0, b1 = b.astype(BF16), (b - b.astype(BF16).astype(F32)).astype(BF16)
    return _dg(a0, b0, dims) + (_dg(a0, b1, dims) + _dg(a1, b0, dims))


def _sigmoid(x):
    return 1.0 / (1.0 + jnp.exp(-x))


def _silu(x):
    return x * _sigmoid(x)


def _layer_norm(y, g, b):
    mu = jnp.mean(y, axis=-1, keepdims=True)
    yc = y - mu
    var = jnp.mean(yc * yc, axis=-1, keepdims=True)
    return yc * lax.rsqrt(var + LN_EPS) * g + b


def _const_spec(shape):
    nd = len(shape)
    return pl.BlockSpec(shape, lambda *_: (0,) * nd, pipeline_mode=pl.Buffered(1))


def _shift_rows(u, carry8, j):
    lead = u.shape[:-2]
    rows, width = u.shape[-2:]
    groups = rows // SUBLANES
    ax = len(lead)
    rot = pltpu.roll(u.reshape(lead + (groups, SUBLANES, width)), j, axis=ax + 1)
    first = jnp.expand_dims(pltpu.roll(carry8, j, axis=ax), ax)
    above = jnp.concatenate([first, lax.slice_in_dim(rot, 0, groups - 1, axis=ax)], axis=ax)
    sub = lax.broadcasted_iota(jnp.int32, rot.shape, ax + 1)
    return jnp.where(sub < j, above, rot).reshape(u.shape)


def _params(sem):
    return pltpu.CompilerParams(dimension_semantics=sem, vmem_limit_bytes=VMEM_LIMIT)


def _ada_kernel(c_ref, w_ref, b_ref, o_ref):
    s = _silu(c_ref[...])
    o_ref[0] = _dot_hp(s, w_ref[0]) + b_ref[0]


def _ada_terms(c_all, ada_w, ada_b):
    n_sub, d, n3 = ada_w.shape
    bc = c_all.shape[0]
    tn = 768
    return pl.pallas_call(
        _ada_kernel,
        grid=(n_sub, n3 // tn),
        in_specs=[
            pl.BlockSpec((bc, d), lambda i, j: (0, 0)),
            pl.BlockSpec((1, d, tn), lambda i, j: (i, 0, j)),
            pl.BlockSpec((1, 1, tn), lambda i, j: (i, 0, j)),
        ],
        out_specs=pl.BlockSpec((1, bc, tn), lambda i, j: (i, 0, j)),
        out_shape=jax.ShapeDtypeStruct((n_sub, bc, n3), F32),
        compiler_params=_params(("arbitrary", "arbitrary")),
        name="ada_terms",
    )(c_all, ada_w, ada_b.reshape(n_sub, 1, n3))


GDN_IN_TILE = 256


def _gdn_in_kernel(x_ref, mod_ref, w_ref, wba_ref, hist_ref, cw_ref, alog_ref, dtb_ref, tri_ref,
                   q_ref, k_ref, v_ref, z_ref, gb_ref, hist_out_ref, carry, *, tm):
    t = pl.program_id(1)

    @pl.when(t == 0)
    def _():
        carry[...] = hist_ref[0]

    shift = mod_ref[0, :, 0:D_MODEL]
    scale = mod_ref[0, :, D_MODEL:2 * D_MODEL]
    hin = (x_ref[0] * (1.0 + scale) + shift).astype(BF16)

    outs = (q_ref, k_ref, v_ref)

    def conv_tile(sec, c):
        lo = sec * GDN_QK + c * GDN_IN_TILE
        cols = slice(lo, lo + GDN_IN_TILE)
        ocols = slice(c * GDN_IN_TILE, (c + 1) * GDN_IN_TILE)
        u = jnp.dot(hin, w_ref[:, cols], preferred_element_type=F32)
        c8 = carry[:, cols]
        acc = u * cw_ref[GDN_CONV - 1:GDN_CONV, cols]
        for j in range(1, GDN_CONV):
            acc = acc + _shift_rows(u, c8, j) * cw_ref[GDN_CONV - 1 - j:GDN_CONV - j, cols]
        carry[:, cols] = u[tm - SUBLANES:tm]
        s = _silu(acc)
        if sec == 2:
            v_ref[0, :, ocols] = s.astype(BF16)
        else:
            post = GDN_DK ** -0.5 if sec == 0 else 1.0
            parts = []
            for h in range(GDN_IN_TILE // GDN_DK):
                sh = s[:, h * GDN_DK:(h + 1) * GDN_DK]
                inv = lax.rsqrt(jnp.sum(sh * sh, axis=-1, keepdims=True) + RMS_EPS)
                parts.append(sh * (inv * post))
            outs[sec][0, :, ocols] = jnp.concatenate(parts, axis=-1).astype(BF16)

    def z_tile(c):
        ocols = slice(c * GDN_IN_TILE, (c + 1) * GDN_IN_TILE)
        cols = slice(GDN_CONV_CH + c * GDN_IN_TILE, GDN_CONV_CH + (c + 1) * GDN_IN_TILE)
        z_ref[0, :, ocols] = jnp.dot(hin, w_ref[:, cols], preferred_element_type=F32).astype(BF16)

    def gates():
        ba = jnp.dot(hin, wba_ref[...], preferred_element_type=F32)
        beta = _sigmoid(ba)
        xs = ba + dtb_ref[...]
        softplus = jnp.maximum(xs, 0.0) + jnp.log(1.0 + jnp.exp(-jnp.abs(xs)))
        g = -jnp.exp(alog_ref[...]) * softplus
        tri = tri_ref[...]
        g0, g1, g2 = _split3(g)
        gsum = (jnp.dot(tri, g0, preferred_element_type=F32)
                + jnp.dot(tri, g1, preferred_element_type=F32)
                + jnp.dot(tri, g2, preferred_element_type=F32))
        lane = lax.broadcasted_iota(jnp.int32, (tm, LANES), 1)
        gb_ref[0] = jnp.where(lane < GDN_HEADS, beta, gsum)[:, :2 * GDN_HEADS]

    for c in range(GDN_QK // GDN_IN_TILE):
        conv_tile(0, c)
        z_tile(c)
        conv_tile(1, c)
        if c == 0:
            gates()
        conv_tile(2, c)
    hist_out_ref[0] = carry[...]


def _gdn_in(x, mod, w_main, w_ba, hist8, conv_w, alog_row, dtb_row, c_len):
    b, l, d = x.shape
    tm = _row_tile(l, 512)
    assert tm % c_len == 0
    idx = jnp.arange(tm, dtype=jnp.int32)
    tri = ((idx[:, None] >= idx[None, :]) & ((idx[:, None] // c_len) == (idx[None, :] // c_len))).astype(BF16)
    tok = lambda w: pl.BlockSpec((1, tm, w), lambda i, t: (i, t, 0))
    return pl.pallas_call(
        functools.partial(_gdn_in_kernel, tm=tm),
        grid=(b, l // tm),
        in_specs=[
            tok(d),
            pl.BlockSpec((1, 1, 3 * d), lambda i, t: (i, 0, 0)),
            _const_spec(w_main.shape),
            _const_spec(w_ba.shape),
            pl.BlockSpec((1, SUBLANES, GDN_CONV_CH), lambda i, t: (i, 0, 0)),
            _const_spec(conv_w.shape),
            _const_spec(alog_row.shape),
            _const_spec(dtb_row.shape),
            _const_spec(tri.shape),
        ],
        out_specs=[tok(GDN_QK), tok(GDN_QK), tok(GDN_V), tok(GDN_V), tok(2 * GDN_HEADS),
                   pl.BlockSpec((1, SUBLANES, GDN_CONV_CH), lambda i, t: (i, 0, 0))],
        out_shape=[
            jax.ShapeDtypeStruct((b, l, GDN_QK), BF16),
            jax.ShapeDtypeStruct((b, l, GDN_QK), BF16),
            jax.ShapeDtypeStruct((b, l, GDN_V), BF16),
            jax.ShapeDtypeStruct((b, l, GDN_V), BF16),
            jax.ShapeDtypeStruct((b, l, 2 * GDN_HEADS), F32),
            jax.ShapeDtypeStruct((b, SUBLANES, GDN_CONV_CH), F32),
        ],
        scratch_shapes=[pltpu.VMEM((SUBLANES, GDN_CONV_CH), F32)],
        compiler_params=_params(("arbitrary", "arbitrary")),
        name="gdn_in",
    )(x, mod, w_main, w_ba, hist8, conv_w, alog_row, dtb_row, tri)


GDN_GROUP = 4


def _bf(xs):
    return [x.astype(BF16) for x in xs]


def _drain(gen):
    for _ in gen:
        pass


def _interleave(major, n_major, minor, n_minor):
    done = 0
    for i in range(n_major):
        next(major, None)
        while done * n_major < (i + 1) * n_minor:
            next(minor, None)
            done += 1
    _drain(major)
    _drain(minor)


def _gdn_chunk_kernel(q_ref, k_ref, v_ref, col_ref, row_ref, s0_ref, o_ref, s_ref, *, c_len, group):
    @pl.when(pl.program_id(1) == 0)
    def _():
        s_ref[...] = s0_ref[...]

    ri = lax.broadcasted_iota(jnp.int32, (c_len, c_len), 0)
    ci = lax.broadcasted_iota(jnp.int32, (c_len, c_len), 1)
    incl = ri >= ci
    strict = ri > ci
    n_sq = c_len.bit_length() - 2
    heads = range(GDN_HEADS)
    rows = lambda g: slice(g * c_len, (g + 1) * c_len)
    lanes = lambda h: slice(h * GDN_DK, (h + 1) * GDN_DK)

    def prepare(chunks, w):
        items = [(g, h) for g in chunks for h in heads]
        each = range(len(items))
        col = {g: col_ref[0, g] for g in chunks}
        row = {g: row_ref[0, g] for g in chunks}
        w['qb'] = [q_ref[0, rows(g), lanes(h)] for g, h in items]
        w['kb'] = [k_ref[0, rows(g), lanes(h)] for g, h in items]
        w['k'] = [x.astype(F32) for x in w['kb']]
        w['v'] = [v_ref[0, rows(g), lanes(h)].astype(F32) for g, h in items]
        w['beta'] = [col[g][:, h:h + 1] for g, h in items]
        w['gc'] = [col[g][:, GDN_HEADS + h:GDN_HEADS + h + 1] for g, h in items]
        gr = [row[g][GDN_HEADS + h:GDN_HEADS + h + 1, :] for g, h in items]
        yield
        w['gam'] = [jnp.where(incl, jnp.exp(jnp.where(incl, w['gc'][i] - gr[i], 0.0)), 0.0) for i in each]
        w['eg'] = [jnp.exp(w['gc'][i]) for i in each]
        yield
        w['sol'] = [jnp.concatenate([w['beta'][i] * w['v'][i], w['beta'][i] * w['k'][i] * w['eg'][i]], axis=-1)
                    for i in each]
        g_last = [w['gc'][i][c_len - 1:c_len, :] for i in each]
        w['k_dec'] = _bf([w['k'][i] * jnp.exp(g_last[i] - w['gc'][i]) for i in each])
        w['decay'] = [jnp.exp(g_last[i]) for i in each]
        yield

    n_prepare = 3

    def solve(w):
        each = range(len(w['qb']))
        qkk = [_dg(jnp.concatenate([w['qb'][i], w['kb'][i]], axis=0), w['kb'][i], NT) for i in each]
        yield
        p = _bf([jnp.where(strict, -(w['beta'][i] * qkk[i][c_len:] * w['gam'][i]), 0.0) for i in each])
        w['att'] = _bf([qkk[i][:c_len] * w['gam'][i] for i in each])
        sol = w['sol']
        solb = _bf(sol)
        sol = [sol[i] + _dg(p[i], solb[i]) for i in each]
        yield
        for _ in range(n_sq):
            p = _bf([_dg(p[i], p[i]) for i in each])
            yield
            solb = _bf(sol)
            sol = [sol[i] + _dg(p[i], solb[i]) for i in each]
            yield
        w['u'] = [sol[i][:, :GDN_DV] for i in each]
        w['wq'] = [jnp.concatenate([sol[i][:, GDN_DV:].astype(BF16),
                                    (w['qb'][i].astype(F32) * w['eg'][i]).astype(BF16)], axis=0)
                   for i in each]

    n_solve = 2 + 2 * n_sq

    state = {'s': [s_ref[0, h] for h in heads]}

    def recur(chunks, w):
        for n, g in enumerate(chunks):
            it = [n * GDN_HEADS + h for h in heads]
            sb = _bf(state['s'])
            ws = [_dg(w['wq'][it[h]], sb[h]) for h in heads]
            yield
            vb = _bf([w['u'][it[h]] - ws[h][:c_len] for h in heads])
            o = [ws[h][c_len:] + _dg(w['att'][it[h]], vb[h]) for h in heads]
            state['s'] = [w['decay'][it[h]] * state['s'][h] + _dg(w['k_dec'][it[h]], vb[h], TN) for h in heads]
            yield
            o_ref[0, rows(g), :] = jnp.concatenate(o, axis=-1).astype(BF16)

    n_waves = 2 if group % 2 == 0 else 1
    per = group // n_waves
    waves = [list(range(n * per, (n + 1) * per)) for n in range(n_waves)]
    data = [{} for _ in waves]
    _drain(prepare(waves[0], data[0]))
    for n in range(n_waves):
        side = []
        n_side = 0
        if n + 1 < n_waves:
            side.append(prepare(waves[n + 1], data[n + 1]))
            n_side += n_prepare
        if n >= 1:
            side.append(recur(waves[n - 1], data[n - 1]))
            n_side += 2 * per

        def chain(gens=side):
            for gen in gens:
                yield from gen

        _interleave(solve(data[n]), n_solve, chain(), n_side)
    _drain(recur(waves[-1], data[-1]))
    for h in heads:
        s_ref[0, h] = state['s'][h]


def _gdn_chunk(q, k, v, col, row, s0, c_len):
    b, l, _ = q.shape
    n = l // c_len
    group = GDN_GROUP if n % GDN_GROUP == 0 else 1
    rows = c_len * group
    tok = pl.BlockSpec((1, rows, GDN_QK), lambda i, t: (i, t, 0))
    st = pl.BlockSpec((1, GDN_HEADS, GDN_DK, GDN_DV), lambda i, t: (i, 0, 0, 0))
    return pl.pallas_call(
        functools.partial(_gdn_chunk_kernel, c_len=c_len, group=group),
        grid=(b, n // group),
        in_specs=[tok, tok, tok,
                  pl.BlockSpec((1, group, c_len, 2 * GDN_HEADS), lambda i, t: (i, t, 0, 0)),
                  pl.BlockSpec((1, group, 2 * GDN_HEADS, c_len), lambda i, t: (i, t, 0, 0)),
                  st],
        out_specs=[tok, st],
        out_shape=[jax.ShapeDtypeStruct((b, l, GDN_V), BF16),
                   jax.ShapeDtypeStruct(s0.shape, F32)],
        compiler_params=_params(("arbitrary", "arbitrary")),
        name="gdn_chunk",
    )(q, k, v, col, row, s0)


FFN_TILE = MXU_DIM
FFN_ROWS = 512
FFN_SUB_TILES = 2


def _mix_ffn_kernel(*refs, gated, nb, tm, n_sub):
    if gated:
        o_ref, z_ref, ng_ref = refs[:3]
        refs = refs[3:]
    else:
        o_ref = refs[0]
        refs = refs[1:]
    (x_ref, mmod_ref, wo_ref, mlg_ref, mlb_ref, fmod_ref, win_ref, bin_ref, cw_ref, cb_ref, hist_ref, wdn_ref,
     lg_ref, lb_ref, y_ref, hist_out_ref, carry, x1_ref, hbuf, fbuf) = refs
    t = pl.program_id(1)

    @pl.when(t == 0)
    def _():
        carry[...] = hist_ref[...]

    sub = tm // n_sub
    subs = range(n_sub)
    rows = lambda s: slice(s * sub, (s + 1) * sub)
    lead = slice(None) if nb > 1 else 0
    per_seq = lambda v: v.reshape(nb, sub, v.shape[-1]) if nb > 1 else v
    slab = lambda v: v.reshape(nb * sub, v.shape[-1]) if nb > 1 else v
    mgate = mmod_ref[lead, :, 2 * D_MODEL:3 * D_MODEL]
    shift = fmod_ref[lead, :, 0:D_MODEL]
    scale = fmod_ref[lead, :, D_MODEL:2 * D_MODEL]
    gate = fmod_ref[lead, :, 2 * D_MODEL:3 * D_MODEL]

    def mixer_input(s):
        if not gated:
            return
        for h in range(GDN_HEADS):
            hs = slice(h * GDN_DV, (h + 1) * GDN_DV)
            oh = o_ref[lead, rows(s), hs].astype(F32)
            r = oh * lax.rsqrt(jnp.mean(oh * oh, axis=-1, keepdims=True) + RMS_EPS) * ng_ref[...]
            hbuf[s, :, hs] = slab(r * _silu(z_ref[lead, rows(s), hs].astype(F32))).astype(BF16)

    def mixer_proj(s):
        a = hbuf[s] if gated else slab(o_ref[lead, rows(s), :])
        return jnp.dot(a, wo_ref[...], preferred_element_type=F32)

    def mixer_norm(s, mix):
        x1 = _layer_norm(ALPHA * x_ref[lead, rows(s), :] + (1.0 + mgate) * per_seq(mix), mlg_ref[...], mlb_ref[...])
        x1_ref[s] = slab(x1)
        hbuf[s] = slab(x1 * (1.0 + scale) + shift).astype(BF16)

    def conv_half(s, lo):
        cols = slice(lo, lo + FFN_TILE)
        u = per_seq(jnp.dot(hbuf[s], win_ref[:, cols], preferred_element_type=F32) + bin_ref[:, cols])
        c8 = carry[lead, :, cols]
        out = u * cw_ref[FFN_CONV - 1:FFN_CONV, cols] + cb_ref[:, cols]
        for j in range(1, FFN_CONV):
            out = out + _shift_rows(u, c8, j) * cw_ref[FFN_CONV - 1 - j:FFN_CONV - j, cols]
        carry[lead, :, cols] = u[..., sub - SUBLANES:sub, :]
        return slab(out)

    def ffn_norm(s, f):
        y = _layer_norm(ALPHA * per_seq(x1_ref[s]) + (1.0 + gate) * per_seq(f), lg_ref[...], lb_ref[...])
        y_ref[lead, rows(s), :] = y

    mixer_input(0)
    mix = {}
    for s in subs:
        mix[s] = mixer_proj(s)
        if s + 1 < n_sub:
            mixer_input(s + 1)
        if s >= 1:
            mixer_norm(s - 1, mix.pop(s - 1))
    mixer_norm(n_sub - 1, mix.pop(n_sub - 1))

    for i in range(D_FF // FFN_TILE):
        lo = i * FFN_TILE
        for s in subs:
            ua = conv_half(s, lo)
            ub = conv_half(s, lo + D_FF)
            fbuf[s, :, lo:lo + FFN_TILE] = (_silu(ua) * ub).astype(BF16)
    hist_out_ref[...] = carry[...]

    down = {}
    for s in subs:
        down[s] = jnp.dot(fbuf[s], wdn_ref[...], preferred_element_type=F32)
        if s >= 1:
            ffn_norm(s - 1, down.pop(s - 1))
    ffn_norm(n_sub - 1, down.pop(n_sub - 1))


def _layer_spec(stacked, layer):
    nd = stacked.ndim - 1
    return pl.BlockSpec((None,) + stacked.shape[1:], lambda *_: (layer,) + (0,) * nd, pipeline_mode=pl.Buffered(1))


def _mix_ffn(o, z, norm_g, x, mix_mod, w_out, mix_ln_g, mix_ln_b,
             ffn_mod, w_in_all, b_in, conv_w, conv_b, hist8, w_down_all, ln_g, ln_b, layer):
    b, l, d = x.shape
    tm = _row_tile(l, FFN_ROWS)
    n_sub = FFN_SUB_TILES if tm == FFN_ROWS else 1
    nb = b if (tm == l and b * tm <= FFN_ROWS and tm % (2 * SUBLANES) == 0) else 1
    gated = z is not None
    tok = pl.BlockSpec((nb, tm, d), lambda i, t: (i, t, 0))
    modspec = pl.BlockSpec((nb, 1, 3 * d), lambda i, t: (i, 0, 0))
    hspec = pl.BlockSpec((nb, SUBLANES, 2 * D_FF), lambda i, t: (i, 0, 0))
    row = _const_spec((1, d))
    head_args, head_specs = ((o, z, norm_g), [tok, tok, _const_spec(norm_g.shape)]) if gated else ((o,), [tok])
    return pl.pallas_call(
        functools.partial(_mix_ffn_kernel, gated=gated, nb=nb, tm=tm, n_sub=n_sub),
        grid=(b // nb, l // tm),
        in_specs=head_specs + [tok, modspec, _const_spec(w_out.shape), row, row,
                               modspec, _layer_spec(w_in_all, layer), _const_spec(b_in.shape),
                               _const_spec(conv_w.shape), _const_spec(conv_b.shape), hspec,
                               _layer_spec(w_down_all, layer), row, row],
        out_specs=[tok, hspec],
        out_shape=[jax.ShapeDtypeStruct((b, l, d), F32),
                   jax.ShapeDtypeStruct((b, SUBLANES, 2 * D_FF), F32)],
        scratch_shapes=[pltpu.VMEM((nb, SUBLANES, 2 * D_FF), F32),
                        pltpu.VMEM((n_sub, nb * tm // n_sub, d), F32),
                        pltpu.VMEM((n_sub, nb * tm // n_sub, d), BF16),
                        pltpu.VMEM((n_sub, nb * tm // n_sub, D_FF), BF16)],
        compiler_params=_params(("arbitrary", "arbitrary")),
        name="mix_ffn_gated" if gated else "mix_ffn",
    )(*head_args, x, mix_mod, w_out, mix_ln_g, mix_ln_b,
      ffn_mod, w_in_all, b_in, conv_w, conv_b, hist8, w_down_all, ln_g, ln_b)


def _rope_pair(pair, cs):
    prod = pair * cs
    return prod + pltpu.roll(prod, MLA_ROPE, axis=1)


def _mla_proj_kernel(x_ref, mod_ref, wkv_ref, kvg_ref, cs_ref, wdq_ref, qg_ref, wuq_ref, *rest, with_kv):
    if with_kv:
        wk_ref, wvt_ref, ckv_ref, kpe_ref, q_ref, k_ref, vt_ref = rest
    else:
        ckv_ref, kpe_ref, q_ref = rest
    x = x_ref[0]
    cs = cs_ref[...]
    lane = lax.broadcasted_iota(jnp.int32, (x.shape[0], LANES), 1)
    kv = jnp.dot(x.astype(BF16), wkv_ref[...], preferred_element_type=F32)
    lat = kv[:, :MLA_KV_LORA]
    ckv = lat * lax.rsqrt(jnp.mean(lat * lat, axis=-1, keepdims=True) + RMS_EPS) * kvg_ref[...]
    kpe = _rope_pair(kv[:, MLA_KV_LORA:], cs)
    ckv_ref[0] = ckv
    kpe_ref[0] = kpe[:, :MLA_ROPE]
    if with_kv:
        lat_b = ckv.astype(BF16)
        kpe_b = jnp.where(lane < MLA_ROPE, kpe, 0.0).astype(BF16)

    shift = mod_ref[0, :, 0:D_MODEL]
    scale = mod_ref[0, :, D_MODEL:2 * D_MODEL]
    hin = (x * (1.0 + scale) + shift).astype(BF16)
    qd = jnp.dot(hin, wdq_ref[...], preferred_element_type=F32)
    qd = qd * lax.rsqrt(jnp.mean(qd * qd, axis=-1, keepdims=True) + RMS_EPS) * qg_ref[...]
    qd = qd.astype(BF16)
    for h in range(MLA_HEADS):
        qh = jnp.dot(qd, wuq_ref[:, h * HEAD_PAD:(h + 1) * HEAD_PAD], preferred_element_type=F32)
        q_ref[0, :, h * HEAD_PAD:h * HEAD_PAD + MLA_NOPE] = (qh[:, :MLA_NOPE] * Q_PRESCALE).astype(BF16)
        pe = jnp.where(lane < MLA_ROPE, _rope_pair(qh[:, MLA_NOPE:], cs) * Q_PRESCALE, 0.0)
        q_ref[0, :, h * HEAD_PAD + MLA_NOPE:(h + 1) * HEAD_PAD] = pe.astype(BF16)
        if with_kv:
            _kv_head(h, lat_b, kpe_b, wk_ref, wvt_ref, k_ref, vt_ref)


def _mla_proj(x, mod, w_kv, kv_g, cossin, w_dq, q_g, w_uq, w_k, w_vt, with_kv):
    b, l, d = x.shape
    tm = _row_tile(l, 512)
    tok = lambda w: pl.BlockSpec((1, tm, w), lambda i, t: (i, t, 0))
    in_specs = [tok(d), pl.BlockSpec((1, 1, 3 * d), lambda i, t: (i, 0, 0)),
                _const_spec(w_kv.shape), _const_spec(kv_g.shape),
                pl.BlockSpec((tm, LANES), lambda i, t: (t, 0)),
                _const_spec(w_dq.shape), _const_spec(q_g.shape), _const_spec(w_uq.shape)]
    out_specs = [tok(MLA_KV_LORA), tok(MLA_ROPE), tok(MLA_HEADS * HEAD_PAD)]
    out_shape = [jax.ShapeDtypeStruct((b, l, MLA_KV_LORA), F32),
                 jax.ShapeDtypeStruct((b, l, MLA_ROPE), F32),
                 jax.ShapeDtypeStruct((b, l, MLA_HEADS * HEAD_PAD), BF16)]
    args = (x, mod, w_kv, kv_g, cossin, w_dq, q_g, w_uq)
    if with_kv:
        in_specs += [_const_spec(w_k.shape), _const_spec(w_vt.shape)]
        out_specs += [tok(MLA_HEADS * HEAD_PAD), pl.BlockSpec((1, MLA_HEADS * MLA_V, tm), lambda i, t: (i, 0, t))]
        out_shape += [jax.ShapeDtypeStruct((b, l, MLA_HEADS * HEAD_PAD), BF16),
                      jax.ShapeDtypeStruct((b, MLA_HEADS * MLA_V, l), BF16)]
        args += (w_k, w_vt)
    return pl.pallas_call(
        functools.partial(_mla_proj_kernel, with_kv=with_kv),
        grid=(b, l // tm),
        in_specs=in_specs,
        out_specs=out_specs,
        out_shape=out_shape,
        compiler_params=_params(("arbitrary", "arbitrary")),
        name="mla_proj_kv" if with_kv else "mla_proj",
    )(*args)


def _kv_head(h, lat, kpe, wk_ref, wvt_ref, k_ref, vt_ref):
    kn = jnp.dot(lat, wk_ref[:, h * MLA_NOPE:(h + 1) * MLA_NOPE], preferred_element_type=F32)
    k_ref[0, :, h * HEAD_PAD:h * HEAD_PAD + MLA_NOPE] = kn.astype(BF16)
    k_ref[0, :, h * HEAD_PAD + MLA_NOPE:(h + 1) * HEAD_PAD] = kpe
    vt_ref[0, h * MLA_V:(h + 1) * MLA_V, :] = _dg(wvt_ref[h], lat, NT).astype(BF16)


def _kv_up_kernel(ckv_ref, kpe_ref, wk_ref, wvt_ref, k_ref, vt_ref):
    lat = ckv_ref[0].astype(BF16)
    kpe = kpe_ref[0].astype(BF16)
    for h in range(MLA_HEADS):
        _kv_head(h, lat, kpe, wk_ref, wvt_ref, k_ref, vt_ref)


def _kv_up(ckv_all, kpe_pad, w_k, w_vt):
    b, lk, _ = ckv_all.shape
    tm = ATTN_TK if lk % ATTN_TK == 0 else lk
    tok = lambda w: pl.BlockSpec((1, tm, w), lambda i, t: (i, t, 0))
    return pl.pallas_call(
        _kv_up_kernel,
        grid=(b, lk // tm),
        in_specs=[tok(MLA_KV_LORA), tok(LANES), _const_spec(w_k.shape), _const_spec(w_vt.shape)],
        out_specs=[tok(MLA_HEADS * HEAD_PAD),
                   pl.BlockSpec((1, MLA_HEADS * MLA_V, tm), lambda i, t: (i, 0, t))],
        out_shape=[jax.ShapeDtypeStruct((b, lk, MLA_HEADS * HEAD_PAD), BF16),
                   jax.ShapeDtypeStruct((b, MLA_HEADS * MLA_V, lk), BF16)],
        compiler_params=_params(("arbitrary", "arbitrary")),
        name="kv_up",
    )(ckv_all, kpe_pad, w_k, w_vt)


ATTN_TQ = 512
ATTN_TK = 512
ATTN_HEADS_PER_STEP = 4
CHUNK_SHIFT = CHUNK.bit_length() - 1


def _chunk_end(pos):
    return (lax.shift_right_logical(pos, CHUNK_SHIFT) + 1) * CHUNK


def _attn_kernel(q_ref, k_ref, vt_ref, o_ref, acc_ref, m_ref, l_ref, sa_ref, sb_ref, *, tq, tk, past, lk, nh):
    i = pl.program_id(2)
    acc_ref[...] = jnp.zeros_like(acc_ref)
    m_ref[...] = jnp.full_like(m_ref, NEG_BIG)
    l_ref[...] = jnp.zeros_like(l_ref)
    q0 = past + i * tq
    n_full = lax.div(jnp.minimum(_chunk_end(q0), lk), tk)
    n_end = lax.div(jnp.minimum(_chunk_end(q0 + tq - 1), lk) + tk - 1, tk)
    n_blk = lk // tk
    heads = range(nh)
    ones = jnp.ones((2 * SUBLANES, tk), BF16)

    def key_start(j):
        return pl.multiple_of(jnp.minimum(j, n_blk - 1) * tk, tk) if lk > tk else 0

    def scores(j, dst):
        k0 = key_start(j)
        for h in heads:
            dst[h] = _dg(k_ref[0, pl.ds(k0, tk), h * HEAD_PAD:(h + 1) * HEAD_PAD],
                         q_ref[0, :, h * HEAD_PAD:(h + 1) * HEAD_PAD], NT)

    def step(j, src, dst, masked):
        if dst is not None:
            scores(j + 1, dst)
        k0 = key_start(j)
        st = [src[h] for h in heads]
        if masked:
            kc = lax.shift_right_logical(k0 + lax.broadcasted_iota(jnp.int32, (tk, tq), 0), CHUNK_SHIFT)
            qc = lax.shift_right_logical(q0 + lax.broadcasted_iota(jnp.int32, (tk, tq), 1), CHUNK_SHIFT)
            visible = kc <= qc
            st = [jnp.where(visible, s, NEG_BIG) for s in st]
        m_old = [m_ref[h] for h in heads]
        m_new = [jnp.maximum(m_old[h], jnp.max(st[h], axis=0, keepdims=True)) for h in heads]
        alpha = [jnp.exp2(m_old[h] - m_new[h]) for h in heads]
        pt = [jnp.exp2(st[h] - m_new[h]).astype(BF16) for h in heads]
        pv = [_dg(jnp.concatenate([vt_ref[0, h * MLA_V:(h + 1) * MLA_V, pl.ds(k0, tk)], ones], axis=0), pt[h])
              for h in heads]
        for h in heads:
            acc_ref[h] = alpha[h] * acc_ref[h] + pv[h][:MLA_V]
            m_ref[h] = m_new[h]
            l_ref[h] = alpha[h] * l_ref[h] + pv[h][MLA_V:MLA_V + 1]

    if n_blk == 1:
        scores(0, sa_ref)

        @pl.when(n_full == 1)
        def _():
            step(0, sa_ref, None, False)

        @pl.when(n_full == 0)
        def _():
            step(0, sa_ref, None, True)
    else:
        odd = lax.rem(n_full, 2)

        @pl.when(odd == 1)
        def _():
            scores(0, sb_ref)
            step(0, sb_ref, sa_ref, False)

        @pl.when(odd == 0)
        def _():
            scores(0, sa_ref)

        def pair(p, c):
            j = odd + 2 * p
            step(j, sa_ref, sb_ref, False)
            step(j + 1, sb_ref, sa_ref, False)
            return c

        lax.fori_loop(0, lax.div(n_full, 2), pair, 0)

        @pl.when(n_end > n_full)
        def _():
            step(n_full, sa_ref, None, True)

        def rest(j, c):
            scores(j, sa_ref)
            step(j, sa_ref, None, True)
            return c

        lax.fori_loop(n_full + 1, n_end, rest, 0)
    for h in heads:
        o_ref[0, :, h * MLA_V:(h + 1) * MLA_V] = (acc_ref[h] / l_ref[h]).T.astype(BF16)


def _attention(q, k, vt, past):
    b, lq, _ = q.shape
    lk = k.shape[1]
    lq_pad = -(-lq // LANES) * LANES
    if lq_pad != lq:
        q = jnp.pad(q, ((0, 0), (0, lq_pad - lq), (0, 0)))
    tq = _row_tile(lq_pad, ATTN_TQ)
    tk = ATTN_TK if lk % ATTN_TK == 0 else lk
    nh = ATTN_HEADS_PER_STEP
    kv_bytes = nh * lk * (HEAD_PAD + MLA_V) * jnp.dtype(BF16).itemsize
    kv_mode = pl.Buffered(2 if 2 * kv_bytes <= VMEM_LIMIT // 4 else 1)
    out = pl.pallas_call(
        functools.partial(_attn_kernel, tq=tq, tk=tk, past=past, lk=lk, nh=nh),
        grid=(b, MLA_HEADS // nh, lq_pad // tq),
        in_specs=[pl.BlockSpec((1, tq, nh * HEAD_PAD), lambda bi, h, i: (bi, i, h)),
                  pl.BlockSpec((1, lk, nh * HEAD_PAD), lambda bi, h, i: (bi, 0, h), pipeline_mode=kv_mode),
                  pl.BlockSpec((1, nh * MLA_V, lk), lambda bi, h, i: (bi, h, 0), pipeline_mode=kv_mode)],
        out_specs=pl.BlockSpec((1, tq, nh * MLA_V), lambda bi, h, i: (bi, i, h)),
        out_shape=jax.ShapeDtypeStruct((b, lq_pad, MLA_HEADS * MLA_V), BF16),
        scratch_shapes=[pltpu.VMEM((nh, MLA_V, tq), F32), pltpu.VMEM((nh, 1, tq), F32),
                        pltpu.VMEM((nh, 1, tq), F32), pltpu.VMEM((nh, tk, tq), F32),
                        pltpu.VMEM((nh, tk, tq), F32)],
        compiler_params=_params(("arbitrary", "arbitrary", "arbitrary")),
        name="chunk_causal_attention",
    )(q, k, vt)
    return out[:, :lq]


def _pad_hist(hist):
    return jnp.pad(hist, ((0, 0), (SUBLANES - hist.shape[1], 0), (0, 0)))


def _rot_half_cols(w):
    w1, w2 = jnp.split(w, 2, axis=-1)
    return jnp.concatenate([-w2, w1], axis=-1)


def _pack_weights(p):
    d = D_MODEL
    pk = {}
    w_in = p['gdn_w_in'][0]
    pk['gdn_w_main'] = w_in.astype(BF16)
    pk['gdn_w_ba'] = jnp.pad(w_in[:, GDN_CONV_CH + GDN_V:], ((0, 0), (0, LANES - 2 * GDN_HEADS))).astype(BF16)
    pad_gate = lambda a: jnp.pad(a.reshape(1, GDN_HEADS), ((0, 0), (GDN_HEADS, LANES - 2 * GDN_HEADS)))
    pk['gdn_alog'] = pad_gate(p['gdn_a_log'][0])
    pk['gdn_dtb'] = pad_gate(p['gdn_dt_bias'][0])
    pk['gdn_conv_w'] = p['gdn_conv_w'][0]
    pk['gdn_norm_g'] = p['gdn_norm_g'][0].reshape(1, GDN_DV)
    pk['gdn_w_out'] = p['gdn_w_out'][0].astype(BF16)
    wkv = p['kv_w_down']
    pk['kv_w_down'] = jnp.concatenate([wkv, _rot_half_cols(wkv[:, MLA_KV_LORA:])], axis=-1).astype(BF16)
    pk['kv_norm_g'] = p['kv_norm_g'].reshape(1, MLA_KV_LORA)
    w_up = p['kv_w_up'].reshape(MLA_KV_LORA, MLA_HEADS, MLA_NOPE + MLA_V)
    pk['kv_w_k'] = w_up[..., :MLA_NOPE].reshape(MLA_KV_LORA, MLA_HEADS * MLA_NOPE).astype(BF16)
    pk['kv_w_vt'] = jnp.transpose(w_up[..., MLA_NOPE:], (1, 2, 0)).astype(BF16)
    pk['mla_w_dq'] = p['mla_w_dq'][0].astype(BF16)
    pk['mla_q_norm_g'] = p['mla_q_norm_g'][0].reshape(1, MLA_Q_LORA)
    wuq = p['mla_w_uq'][0].reshape(MLA_Q_LORA, MLA_HEADS, MLA_NOPE + MLA_ROPE)
    wuq = jnp.concatenate([wuq, _rot_half_cols(wuq[..., MLA_NOPE:])], axis=-1)
    pk['mla_w_uq'] = wuq.reshape(MLA_Q_LORA, MLA_HEADS * HEAD_PAD).astype(BF16)
    pk['mla_w_out'] = p['mla_w_out'][0].astype(BF16)
    pk['ffn_w_in'] = p['ffn_w_in'].astype(BF16)
    pk['ffn_b_in'] = p['ffn_b_in'].reshape(DEPTH, 1, 2 * D_FF)
    pk['ffn_conv_w'] = p['ffn_conv_w']
    pk['ffn_conv_b'] = p['ffn_conv_b'].reshape(DEPTH, 1, 2 * D_FF)
    pk['ffn_w_down'] = p['ffn_w_down'].astype(BF16)
    pk['ln_g'] = p['ln_g'].reshape(2 * DEPTH, 1, d)
    pk['ln_b'] = p['ln_b'].reshape(2 * DEPTH, 1, d)
    return pk


def _rope_table(past, l):
    inv = 1.0 / (ROPE_THETA ** (jnp.arange(0, MLA_ROPE, 2, dtype=F32) / MLA_ROPE))
    ang = (past + jnp.arange(l, dtype=jnp.int32)).astype(F32)[:, None] * inv[None, :]
    ang = jnp.concatenate([ang, ang], axis=-1)
    return jnp.concatenate([jnp.cos(ang), jnp.sin(ang)], axis=-1)


def _block_tail(o, z, norm_g, w_out, x, mods, hist, pk, layer):
    y, hist8 = _mix_ffn(o, z, norm_g, x, mods[2 * layer], w_out, pk['ln_g'][2 * layer], pk['ln_b'][2 * layer],
                        mods[2 * layer + 1], pk['ffn_w_in'], pk['ffn_b_in'][layer],
                        pk['ffn_conv_w'][layer], pk['ffn_conv_b'][layer], _pad_hist(hist),
                        pk['ffn_w_down'], pk['ln_g'][2 * layer + 1], pk['ln_b'][2 * layer + 1], layer)
    return y, hist8[:, SUBLANES - (FFN_CONV - 1):]


def _trunk(x, mods, gdn_state, gdn_conv, ffn_conv, ckv_past, kpe_past, pk):
    b, l, _ = x.shape
    past = kpe_past.shape[1]
    c_len = CHUNK if l % CHUNK == 0 else l
    n = l // c_len

    q, k, v, z, gb, hist8 = _gdn_in(x, mods[0], pk['gdn_w_main'], pk['gdn_w_ba'], _pad_hist(gdn_conv[:, 0]),
                                    pk['gdn_conv_w'], pk['gdn_alog'], pk['gdn_dtb'], c_len)
    col = gb.reshape(b, n, c_len, 2 * GDN_HEADS)
    row = jnp.swapaxes(col, 2, 3)
    o, s_end = _gdn_chunk(q, k, v, col, row, gdn_state[:, 0], c_len)
    gdn_conv_out = hist8[:, None, SUBLANES - (GDN_CONV - 1):]
    x, fh0 = _block_tail(o, z, pk['gdn_norm_g'], pk['gdn_w_out'], x, mods, ffn_conv[:, 0], pk, 0)

    proj = _mla_proj(x, mods[2], pk['kv_w_down'], pk['kv_norm_g'], _rope_table(past, l),
                     pk['mla_w_dq'], pk['mla_q_norm_g'], pk['mla_w_uq'], pk['kv_w_k'], pk['kv_w_vt'],
                     with_kv=(past == 0))
    ckv_new, kpe_new, qm = proj[:3]
    if past == 0:
        km, vm = proj[3:]
    else:
        ckv_all = jnp.concatenate([ckv_past, ckv_new], axis=1)
        kpe_all = jnp.concatenate([kpe_past, kpe_new], axis=1)
        kpe_pad = jnp.pad(kpe_all, ((0, 0), (0, 0), (0, LANES - MLA_ROPE)))
        km, vm = _kv_up(ckv_all, kpe_pad, pk['kv_w_k'], pk['kv_w_vt'])
    o = _attention(qm, km, vm, past)
    x, fh1 = _block_tail(o, None, None, pk['mla_w_out'], x, mods, ffn_conv[:, 1], pk, 1)

    return (x, s_end[:, None], gdn_conv_out, jnp.stack([fh0, fh1], axis=1), ckv_new, kpe_new)


def kernel(x_prompt, x_sample, c_prompt, c_sample, state_gdn, state_gdn_conv, state_ffn_conv, cache_ckv, cache_kpe, ada_w, ada_b, ln_g, ln_b, gdn_w_in, gdn_conv_w, gdn_a_log, gdn_dt_bias, gdn_norm_g, gdn_w_out, kv_w_down, kv_norm_g, kv_w_up, mla_w_dq, mla_q_norm_g, mla_w_uq, mla_w_out, ffn_w_in, ffn_b_in, ffn_conv_w, ffn_conv_b, ffn_w_down):
    p = {'ln_g': ln_g, 'ln_b': ln_b, 'gdn_w_in': gdn_w_in, 'gdn_conv_w': gdn_conv_w, 'gdn_a_log': gdn_a_log,
         'gdn_dt_bias': gdn_dt_bias, 'gdn_norm_g': gdn_norm_g, 'gdn_w_out': gdn_w_out,
         'kv_w_down': kv_w_down, 'kv_norm_g': kv_norm_g, 'kv_w_up': kv_w_up,
         'mla_w_dq': mla_w_dq, 'mla_q_norm_g': mla_q_norm_g, 'mla_w_uq': mla_w_uq, 'mla_w_out': mla_w_out,
         'ffn_w_in': ffn_w_in, 'ffn_b_in': ffn_b_in, 'ffn_conv_w': ffn_conv_w,
         'ffn_conv_b': ffn_conv_b, 'ffn_w_down': ffn_w_down}
    pk = _pack_weights(p)
    bp = x_prompt.shape[0]
    mods = _ada_terms(jnp.concatenate([c_prompt, c_sample], axis=0), ada_w, ada_b)[:, :, None, :]
    zeros_like_b = lambda a: jnp.zeros((bp,) + a.shape[1:], a.dtype)
    out_p = _trunk(x_prompt, mods[:, :bp], zeros_like_b(state_gdn), zeros_like_b(state_gdn_conv),
                   zeros_like_b(state_ffn_conv), jnp.zeros((bp, 0, MLA_KV_LORA), cache_ckv.dtype),
                   jnp.zeros((bp, 0, MLA_ROPE), cache_kpe.dtype), pk)
    out_s = _trunk(x_sample, mods[:, bp:], state_gdn, state_gdn_conv, state_ffn_conv, cache_ckv, cache_kpe, pk)
    return (out_p[0], out_s[0]) + out_p[1:] + out_s[1:]
```

```python
import functools

import jax
import jax.numpy as jnp
from jax import lax
from jax.experimental import pallas as pl
from jax.experimental.pallas import tpu as pltpu

F32 = jnp.float32
BF16 = jnp.bfloat16

D_MODEL = 1024
DEPTH = 2
CHUNK = 64
ALPHA = (2.0 * DEPTH) ** 0.25
LN_EPS = 1e-5
RMS_EPS = 1e-6
GDN_HEADS = 8
GDN_DK = 128
GDN_DV = 128
GDN_CONV = 4
GDN_QK = GDN_HEADS * GDN_DK
GDN_V = GDN_HEADS * GDN_DV
GDN_CONV_CH = 2 * GDN_QK + GDN_V
MLA_HEADS = 8
MLA_NOPE = 128
MLA_ROPE = 64
MLA_V = 128
MLA_KV_LORA = 256
MLA_Q_LORA = 384
ROPE_THETA = 10000.0
MLA_SCALE = (MLA_NOPE + MLA_ROPE) ** -0.5
Q_PRESCALE = MLA_SCALE * 1.4426950408889634
D_FF = 2816
FFN_CONV = 3

LANES = 128
SUBLANES = 8
MXU_DIM = 256
HEAD_PAD = 256
VMEM_LIMIT = 56 * 1024 * 1024
NEG_BIG = -1e30

NN = ((1,), (0,))
NT = ((1,), (1,))
TN = ((0,), (0,))


def _row_tile(n, cap):
    if n <= cap:
        return n
    for t in range(cap, 15, -1):
        if n % t == 0 and t % 16 == 0:
            return t
    raise ValueError(f"no row tile for {n}")


def _split3(a):
    hi = a.astype(BF16)
    r = a - hi.astype(F32)
    mid = r.astype(BF16)
    lo = (r - mid.astype(F32)).astype(BF16)
    return hi, mid, lo


def _dg(a, b, dims=NN):
    return lax.dot_general(a, b, (dims, ((), ())), preferred_element_type=F32)


def _dot_hp(a, b, dims=NN):
    a0, a1 = a.astype(BF16), (a - a.astype(BF16).astype(F32)).astype(BF16)
    b0, b1 = b.astype(BF16), (b - b.astype(BF16).astype(F32)).astype(BF16)
    return _dg(a0, b0, dims) + (_dg(a0, b1, dims) + _dg(a1, b0, dims))


def _sigmoid(x):
    return 1.0 / (1.0 + jnp.exp(-x))


def _silu(x):
    return x * _sigmoid(x)


def _layer_norm(y, g, b):
    mu = jnp.mean(y, axis=-1, keepdims=True)
    yc = y - mu
    var = jnp.mean(yc * yc, axis=-1, keepdims=True)
    return yc * lax.rsqrt(var + LN_EPS) * g + b


def _const_spec(shape):
    nd = len(shape)
    return pl.BlockSpec(shape, lambda *_: (0,) * nd, pipeline_mode=pl.Buffered(1))


def _shift_rows(u, carry8, j):
    lead = u.shape[:-2]
    rows, width = u.shape[-2:]
    groups = rows // SUBLANES
    ax = len(lead)
    rot = pltpu.roll(u.reshape(lead + (groups, SUBLANES, width)), j, axis=ax + 1)
    first = jnp.expand_dims(pltpu.roll(carry8, j, axis=ax), ax)
    above = jnp.concatenate([first, lax.slice_in_dim(rot, 0, groups - 1, axis=ax)], axis=ax)
    sub = lax.broadcasted_iota(jnp.int32, rot.shape, ax + 1)
    return jnp.where(sub < j, above, rot).reshape(u.shape)


def _params(sem):
    return pltpu.CompilerParams(dimension_semantics=sem, vmem_limit_bytes=VMEM_LIMIT)


def _ada_kernel(c_ref, w_ref, b_ref, o_ref):
    s = _silu(c_ref[...])
    o_ref[0] = _dot_hp(s, w_ref[0]) + b_ref[0]


def _ada_terms(c_all, ada_w, ada_b):
    n_sub, d, n3 = ada_w.shape
    bc = c_all.shape[0]
    tn = 768
    return pl.pallas_call(
        _ada_kernel,
        grid=(n_sub, n3 // tn),
        in_specs=[
            pl.BlockSpec((bc, d), lambda i, j: (0, 0)),
            pl.BlockSpec((1, d, tn), lambda i, j: (i, 0, j)),
            pl.BlockSpec((1, 1, tn), lambda i, j: (i, 0, j)),
        ],
        out_specs=pl.BlockSpec((1, bc, tn), lambda i, j: (i, 0, j)),
        out_shape=jax.ShapeDtypeStruct((n_sub, bc, n3), F32),
        compiler_params=_params(("arbitrary", "arbitrary")),
        name="ada_terms",
    )(c_all, ada_w, ada_b.reshape(n_sub, 1, n3))


GDN_IN_TILE = 256


def _gdn_in_kernel(x_ref, mod_ref, w_ref, wba_ref, hist_ref, cw_ref, alog_ref, dtb_ref, tri_ref,
                   q_ref, k_ref, v_ref, z_ref, gb_ref, hist_out_ref, carry, *, tm):
    t = pl.program_id(1)

    @pl.when(t == 0)
    def _():
        carry[...] = hist_ref[0]

    shift = mod_ref[0, :, 0:D_MODEL]
    scale = mod_ref[0, :, D_MODEL:2 * D_MODEL]
    hin = (x_ref[0] * (1.0 + scale) + shift).astype(BF16)

    outs = (q_ref, k_ref, v_ref)

    def conv_tile(sec, c):
        lo = sec * GDN_QK + c * GDN_IN_TILE
        cols = slice(lo, lo + GDN_IN_TILE)
        ocols = slice(c * GDN_IN_TILE, (c + 1) * GDN_IN_TILE)
        u = jnp.dot(hin, w_ref[:, cols], preferred_element_type=F32)
        c8 = carry[:, cols]
        acc = u * cw_ref[GDN_CONV - 1:GDN_CONV, cols]
        for j in range(1, GDN_CONV):
            acc = acc + _shift_rows(u, c8, j) * cw_ref[GDN_CONV - 1 - j:GDN_CONV - j, cols]
        carry[:, cols] = u[tm - SUBLANES:tm]
        s = _silu(acc)
        if sec == 2:
            v_ref[0, :, ocols] = s.astype(BF16)
        else:
            post = GDN_DK ** -0.5 if sec == 0 else 1.0
            parts = []
            for h in range(GDN_IN_TILE // GDN_DK):
                sh = s[:, h * GDN_DK:(h + 1) * GDN_DK]
                inv = lax.rsqrt(jnp.sum(sh * sh, axis=-1, keepdims=True) + RMS_EPS)
                parts.append(sh * (inv * post))
            outs[sec][0, :, ocols] = jnp.concatenate(parts, axis=-1).astype(BF16)

    def z_tile(c):
        ocols = slice(c * GDN_IN_TILE, (c + 1) * GDN_IN_TILE)
        cols = slice(GDN_CONV_CH + c * GDN_IN_TILE, GDN_CONV_CH + (c + 1) * GDN_IN_TILE)
        z_ref[0, :, ocols] = jnp.dot(hin, w_ref[:, cols], preferred_element_type=F32).astype(BF16)

    def gates():
        ba = jnp.dot(hin, wba_ref[...], preferred_element_type=F32)
        beta = _sigmoid(ba)
        xs = ba + dtb_ref[...]
        softplus = jnp.maximum(xs, 0.0) + jnp.log(1.0 + jnp.exp(-jnp.abs(xs)))
        g = -jnp.exp(alog_ref[...]) * softplus
        tri = tri_ref[...]
        g0, g1, g2 = _split3(g)
        gsum = (jnp.dot(tri, g0, preferred_element_type=F32)
                + jnp.dot(tri, g1, preferred_element_type=F32)
                + jnp.dot(tri, g2, preferred_element_type=F32))
        lane = lax.broadcasted_iota(jnp.int32, (tm, LANES), 1)
        gb_ref[0] = jnp.where(lane < GDN_HEADS, beta, gsum)[:, :2 * GDN_HEADS]

    for c in range(GDN_QK // GDN_IN_TILE):
        conv_tile(0, c)
        z_tile(c)
        conv_tile(1, c)
        if c == 0:
            gates()
        conv_tile(2, c)
    hist_out_ref[0] = carry[...]


def _gdn_in(x, mod, w_main, w_ba, hist8, conv_w, alog_row, dtb_row, c_len):
    b, l, d = x.shape
    tm = _row_tile(l, 512)
    assert tm % c_len == 0
    idx = jnp.arange(tm, dtype=jnp.int32)
    tri = ((idx[:, None] >= idx[None, :]) & ((idx[:, None] // c_len) == (idx[None, :] // c_len))).astype(BF16)
    tok = lambda w: pl.BlockSpec((1, tm, w), lambda i, t: (i, t, 0))
    return pl.pallas_call(
        functools.partial(_gdn_in_kernel, tm=tm),
        grid=(b, l // tm),
        in_specs=[
            tok(d),
            pl.BlockSpec((1, 1, 3 * d), lambda i, t: (i, 0, 0)),
            _const_spec(w_main.shape),
            _const_spec(w_ba.shape),
            pl.BlockSpec((1, SUBLANES, GDN_CONV_CH), lambda i, t: (i, 0, 0)),
            _const_spec(conv_w.shape),
            _const_spec(alog_row.shape),
            _const_spec(dtb_row.shape),
            _const_spec(tri.shape),
        ],
        out_specs=[tok(GDN_QK), tok(GDN_QK), tok(GDN_V), tok(GDN_V), tok(2 * GDN_HEADS),
                   pl.BlockSpec((1, SUBLANES, GDN_CONV_CH), lambda i, t: (i, 0, 0))],
        out_shape=[
            jax.ShapeDtypeStruct((b, l, GDN_QK), BF16),
            jax.ShapeDtypeStruct((b, l, GDN_QK), BF16),
            jax.ShapeDtypeStruct((b, l, GDN_V), BF16),
            jax.ShapeDtypeStruct((b, l, GDN_V), BF16),
            jax.ShapeDtypeStruct((b, l, 2 * GDN_HEADS), F32),
            jax.ShapeDtypeStruct((b, SUBLANES, GDN_CONV_CH), F32),
        ],
        scratch_shapes=[pltpu.VMEM((SUBLANES, GDN_CONV_CH), F32)],
        compiler_params=_params(("arbitrary", "arbitrary")),
        name="gdn_in",
    )(x, mod, w_main, w_ba, hist8, conv_w, alog_row, dtb_row, tri)


GDN_GROUP = 4


def _bf(xs):
    return [x.astype(BF16) for x in xs]


def _drain(gen):
    for _ in gen:
        pass


def _interleave(major, n_major, minor, n_minor):
    done = 0
    for i in range(n_major):
        next(major, None)
        while done * n_major < (i + 1) * n_minor:
            next(minor, None)
            done += 1
    _drain(major)
    _drain(minor)


def _gdn_chunk_kernel(q_ref, k_ref, v_ref, z_ref, ng_ref, col_ref, row_ref, s0_ref, o_ref, s_ref, *,
                      c_len, group):
    @pl.when(pl.program_id(1) == 0)
    def _():
        s_ref[...] = s0_ref[...]

    ri = lax.broadcasted_iota(jnp.int32, (c_len, c_len), 0)
    ci = lax.broadcasted_iota(jnp.int32, (c_len, c_len), 1)
    incl = ri >= ci
    strict = ri > ci
    n_sq = c_len.bit_length() - 2
    heads = range(GDN_HEADS)
    rows = lambda g: slice(g * c_len, (g + 1) * c_len)
    lanes = lambda h: slice(h * GDN_DK, (h + 1) * GDN_DK)

    def prepare(chunks, w):
        items = [(g, h) for g in chunks for h in heads]
        each = range(len(items))
        col = {g: col_ref[0, g] for g in chunks}
        row = {g: row_ref[0, g] for g in chunks}
        w['qb'] = [q_ref[0, rows(g), lanes(h)] for g, h in items]
        w['kb'] = [k_ref[0, rows(g), lanes(h)] for g, h in items]
        w['k'] = [x.astype(F32) for x in w['kb']]
        w['v'] = [v_ref[0, rows(g), lanes(h)].astype(F32) for g, h in items]
        w['beta'] = [col[g][:, h:h + 1] for g, h in items]
        w['gc'] = [col[g][:, GDN_HEADS + h:GDN_HEADS + h + 1] for g, h in items]
        gr = [row[g][GDN_HEADS + h:GDN_HEADS + h + 1, :] for g, h in items]
        yield
        w['gam'] = [jnp.where(incl, jnp.exp(jnp.where(incl, w['gc'][i] - gr[i], 0.0)), 0.0) for i in each]
        w['eg'] = [jnp.exp(w['gc'][i]) for i in each]
        yield
        w['sol'] = [jnp.concatenate([w['beta'][i] * w['v'][i], w['beta'][i] * w['k'][i] * w['eg'][i]], axis=-1)
                    for i in each]
        g_last = [w['gc'][i][c_len - 1:c_len, :] for i in each]
        w['k_dec'] = _bf([w['k'][i] * jnp.exp(g_last[i] - w['gc'][i]) for i in each])
        w['decay'] = [jnp.exp(g_last[i]) for i in each]
        yield

    n_prepare = 3

    def solve(w):
        each = range(len(w['qb']))
        qkk = [_dg(jnp.concatenate([w['qb'][i], w['kb'][i]], axis=0), w['kb'][i], NT) for i in each]
        yield
        p = _bf([jnp.where(strict, -(w['beta'][i] * qkk[i][c_len:] * w['gam'][i]), 0.0) for i in each])
        w['att'] = _bf([qkk[i][:c_len] * w['gam'][i] for i in each])
        sol = w['sol']
        solb = _bf(sol)
        sol = [sol[i] + _dg(p[i], solb[i]) for i in each]
        yield
        for _ in range(n_sq):
            p = _bf([_dg(p[i], p[i]) for i in each])
            yield
            solb = _bf(sol)
            sol = [sol[i] + _dg(p[i], solb[i]) for i in each]
            yield
        w['u'] = [sol[i][:, :GDN_DV] for i in each]
        w['wq'] = [jnp.concatenate([sol[i][:, GDN_DV:].astype(BF16),
                                    (w['qb'][i].astype(F32) * w['eg'][i]).astype(BF16)], axis=0)
                   for i in each]

    n_solve = 2 + 2 * n_sq

    state = {'s': [s_ref[0, h] for h in heads]}

    def recur(chunks, w):
        for n, g in enumerate(chunks):
            it = [n * GDN_HEADS + h for h in heads]
            sb = _bf(state['s'])
            ws = [_dg(w['wq'][it[h]], sb[h]) for h in heads]
            yield
            vb = _bf([w['u'][it[h]] - ws[h][:c_len] for h in heads])
            o = [ws[h][c_len:] + _dg(w['att'][it[h]], vb[h]) for h in heads]
            state['s'] = [w['decay'][it[h]] * state['s'][h] + _dg(w['k_dec'][it[h]], vb[h], TN) for h in heads]
            yield
            zg = [_silu(z_ref[0, rows(g), lanes(h)].astype(F32)) for h in heads]
            o = [o[h] * lax.rsqrt(jnp.mean(o[h] * o[h], axis=-1, keepdims=True) + RMS_EPS) * ng_ref[...] * zg[h]
                 for h in heads]
            o_ref[0, rows(g), :] = jnp.concatenate(o, axis=-1).astype(BF16)

    n_waves = 2 if group % 2 == 0 else 1
    per = group // n_waves
    waves = [list(range(n * per, (n + 1) * per)) for n in range(n_waves)]
    data = [{} for _ in waves]
    _drain(prepare(waves[0], data[0]))
    for n in range(n_waves):
        side = []
        n_side = 0
        if n + 1 < n_waves:
            side.append(prepare(waves[n + 1], data[n + 1]))
            n_side += n_prepare
        if n >= 1:
            side.append(recur(waves[n - 1], data[n - 1]))
            n_side += 2 * per

        def chain(gens=side):
            for gen in gens:
                yield from gen

        _interleave(solve(data[n]), n_solve, chain(), n_side)
    _drain(recur(waves[-1], data[-1]))
    for h in heads:
        s_ref[0, h] = state['s'][h]


def _gdn_chunk(q, k, v, z, norm_g, col, row, s0, c_len):
    b, l, _ = q.shape
    n = l // c_len
    group = GDN_GROUP if n % GDN_GROUP == 0 else 1
    rows = c_len * group
    tok = pl.BlockSpec((1, rows, GDN_QK), lambda i, t: (i, t, 0))
    st = pl.BlockSpec((1, GDN_HEADS, GDN_DK, GDN_DV), lambda i, t: (i, 0, 0, 0))
    return pl.pallas_call(
        functools.partial(_gdn_chunk_kernel, c_len=c_len, group=group),
        grid=(b, n // group),
        in_specs=[tok, tok, tok, tok, _const_spec(norm_g.shape),
                  pl.BlockSpec((1, group, c_len, 2 * GDN_HEADS), lambda i, t: (i, t, 0, 0)),
                  pl.BlockSpec((1, group, 2 * GDN_HEADS, c_len), lambda i, t: (i, t, 0, 0)),
                  st],
        out_specs=[tok, st],
        out_shape=[jax.ShapeDtypeStruct((b, l, GDN_V), BF16),
                   jax.ShapeDtypeStruct(s0.shape, F32)],
        compiler_params=_params(("arbitrary", "arbitrary")),
        name="gdn_chunk",
    )(q, k, v, z, norm_g, col, row, s0)


FFN_TILE = MXU_DIM
FFN_ROWS = 512
FFN_SUB_TILES = 2


def _mix_ffn_kernel(o_ref, x_ref, mmod_ref, wo_ref, mlg_ref, mlb_ref, fmod_ref, win_ref, bin_ref, cw_ref, cb_ref,
                    hist_ref, wdn_ref, lg_ref, lb_ref, y_ref, hist_out_ref, carry, x1_ref, hbuf, fbuf, *,
                    nb, tm, n_sub):
    t = pl.program_id(1)

    @pl.when(t == 0)
    def _():
        carry[...] = hist_ref[...]

    sub = tm // n_sub
    subs = range(n_sub)
    rows = lambda s: slice(s * sub, (s + 1) * sub)
    lead = slice(None) if nb > 1 else 0
    per_seq = lambda v: v.reshape(nb, sub, v.shape[-1]) if nb > 1 else v
    slab = lambda v: v.reshape(nb * sub, v.shape[-1]) if nb > 1 else v
    mgate = mmod_ref[lead, :, 2 * D_MODEL:3 * D_MODEL]
    shift = fmod_ref[lead, :, 0:D_MODEL]
    scale = fmod_ref[lead, :, D_MODEL:2 * D_MODEL]
    gate = fmod_ref[lead, :, 2 * D_MODEL:3 * D_MODEL]

    def mixer_proj(s):
        return jnp.dot(slab(o_ref[lead, rows(s), :]), wo_ref[...], preferred_element_type=F32)

    def mixer_norm(s, mix):
        x1 = _layer_norm(ALPHA * x_ref[lead, rows(s), :] + (1.0 + mgate) * per_seq(mix), mlg_ref[...], mlb_ref[...])
        x1_ref[s] = slab(x1)
        hbuf[s] = slab(x1 * (1.0 + scale) + shift).astype(BF16)

    def conv_half(s, lo):
        cols = slice(lo, lo + FFN_TILE)
        u = per_seq(jnp.dot(hbuf[s], win_ref[:, cols], preferred_element_type=F32) + bin_ref[:, cols])
        c8 = carry[lead, :, cols]
        out = u * cw_ref[FFN_CONV - 1:FFN_CONV, cols] + cb_ref[:, cols]
        for j in range(1, FFN_CONV):
            out = out + _shift_rows(u, c8, j) * cw_ref[FFN_CONV - 1 - j:FFN_CONV - j, cols]
        carry[lead, :, cols] = u[..., sub - SUBLANES:sub, :]
        return slab(out)

    def ffn_norm(s, f):
        y = _layer_norm(ALPHA * per_seq(x1_ref[s]) + (1.0 + gate) * per_seq(f), lg_ref[...], lb_ref[...])
        y_ref[lead, rows(s), :] = y

    mix = {}
    for s in subs:
        mix[s] = mixer_proj(s)
        if s >= 1:
            mixer_norm(s - 1, mix.pop(s - 1))
    mixer_norm(n_sub - 1, mix.pop(n_sub - 1))

    for i in range(D_FF // FFN_TILE):
        lo = i * FFN_TILE
        for s in subs:
            ua = conv_half(s, lo)
            ub = conv_half(s, lo + D_FF)
            fbuf[s, :, lo:lo + FFN_TILE] = (_silu(ua) * ub).astype(BF16)
    hist_out_ref[...] = carry[...]

    down = {}
    for s in subs:
        down[s] = jnp.dot(fbuf[s], wdn_ref[...], preferred_element_type=F32)
        if s >= 1:
            ffn_norm(s - 1, down.pop(s - 1))
    ffn_norm(n_sub - 1, down.pop(n_sub - 1))


def _layer_spec(stacked, layer):
    nd = stacked.ndim - 1
    return pl.BlockSpec((None,) + stacked.shape[1:], lambda *_: (layer,) + (0,) * nd, pipeline_mode=pl.Buffered(1))


def _mix_ffn(o, x, mix_mod, w_out, mix_ln_g, mix_ln_b,
             ffn_mod, w_in_all, b_in, conv_w, conv_b, hist8, w_down_all, ln_g, ln_b, layer):
    b, l, d = x.shape
    tm = _row_tile(l, FFN_ROWS)
    n_sub = FFN_SUB_TILES if tm == FFN_ROWS else 1
    nb = b if (tm == l and b * tm <= FFN_ROWS and tm % (2 * SUBLANES) == 0) else 1
    tok = pl.BlockSpec((nb, tm, d), lambda i, t: (i, t, 0))
    modspec = pl.BlockSpec((nb, 1, 3 * d), lambda i, t: (i, 0, 0))
    hspec = pl.BlockSpec((nb, SUBLANES, 2 * D_FF), lambda i, t: (i, 0, 0))
    row = _const_spec((1, d))
    return pl.pallas_call(
        functools.partial(_mix_ffn_kernel, nb=nb, tm=tm, n_sub=n_sub),
        grid=(b // nb, l // tm),
        in_specs=[tok, tok, modspec, _const_spec(w_out.shape), row, row,
                  modspec, _layer_spec(w_in_all, layer), _const_spec(b_in.shape),
                  _const_spec(conv_w.shape), _const_spec(conv_b.shape), hspec,
                  _layer_spec(w_down_all, layer), row, row],
        out_specs=[tok, hspec],
        out_shape=[jax.ShapeDtypeStruct((b, l, d), F32),
                   jax.ShapeDtypeStruct((b, SUBLANES, 2 * D_FF), F32)],
        scratch_shapes=[pltpu.VMEM((nb, SUBLANES, 2 * D_FF), F32),
                        pltpu.VMEM((n_sub, nb * tm // n_sub, d), F32),
                        pltpu.VMEM((n_sub, nb * tm // n_sub, d), BF16),
                        pltpu.VMEM((n_sub, nb * tm // n_sub, D_FF), BF16)],
        compiler_params=_params(("arbitrary", "arbitrary")),
        name="mix_ffn",
    )(o, x, mix_mod, w_out, mix_ln_g, mix_ln_b,
      ffn_mod, w_in_all, b_in, conv_w, conv_b, hist8, w_down_all, ln_g, ln_b)


def _rope_pair(pair, cs):
    prod = pair * cs
    return prod + pltpu.roll(prod, MLA_ROPE, axis=1)


def _mla_proj_kernel(x_ref, mod_ref, wkv_ref, kvg_ref, cs_ref, wdq_ref, qg_ref, wuq_ref, *rest, with_kv):
    if with_kv:
        wk_ref, wvt_ref, ckv_ref, kpe_ref, q_ref, k_ref, vt_ref = rest
    else:
        ckv_ref, kpe_ref, q_ref = rest
    x = x_ref[0]
    cs = cs_ref[...]
    lane = lax.broadcasted_iota(jnp.int32, (x.shape[0], LANES), 1)
    kv = jnp.dot(x.astype(BF16), wkv_ref[...], preferred_element_type=F32)
    lat = kv[:, :MLA_KV_LORA]
    ckv = lat * lax.rsqrt(jnp.mean(lat * lat, axis=-1, keepdims=True) + RMS_EPS) * kvg_ref[...]
    kpe = _rope_pair(kv[:, MLA_KV_LORA:], cs)
    ckv_ref[0] = ckv
    kpe_ref[0] = kpe[:, :MLA_ROPE]
    if with_kv:
        lat_b = ckv.astype(BF16)
        kpe_b = jnp.where(lane < MLA_ROPE, kpe, 0.0).astype(BF16)

    shift = mod_ref[0, :, 0:D_MODEL]
    scale = mod_ref[0, :, D_MODEL:2 * D_MODEL]
    hin = (x * (1.0 + scale) + shift).astype(BF16)
    qd = jnp.dot(hin, wdq_ref[...], preferred_element_type=F32)
    qd = qd * lax.rsqrt(jnp.mean(qd * qd, axis=-1, keepdims=True) + RMS_EPS) * qg_ref[...]
    qd = qd.astype(BF16)
    for h in range(MLA_HEADS):
        qh = jnp.dot(qd, wuq_ref[:, h * HEAD_PAD:(h + 1) * HEAD_PAD], preferred_element_type=F32)
        q_ref[0, :, h * HEAD_PAD:h * HEAD_PAD + MLA_NOPE] = (qh[:, :MLA_NOPE] * Q_PRESCALE).astype(BF16)
        pe = jnp.where(lane < MLA_ROPE, _rope_pair(qh[:, MLA_NOPE:], cs) * Q_PRESCALE, 0.0)
        q_ref[0, :, h * HEAD_PAD + MLA_NOPE:(h + 1) * HEAD_PAD] = pe.astype(BF16)
        if with_kv:
            _kv_head(h, lat_b, kpe_b, wk_ref, wvt_ref, k_ref, vt_ref)


def _mla_proj(x, mod, w_kv, kv_g, cossin, w_dq, q_g, w_uq, w_k, w_vt, with_kv):
    b, l, d = x.shape
    tm = _row_tile(l, 512)
    tok = lambda w: pl.BlockSpec((1, tm, w), lambda i, t: (i, t, 0))
    in_specs = [tok(d), pl.BlockSpec((1, 1, 3 * d), lambda i, t: (i, 0, 0)),
                _const_spec(w_kv.shape), _const_spec(kv_g.shape),
                pl.BlockSpec((tm, LANES), lambda i, t: (t, 0)),
                _const_spec(w_dq.shape), _const_spec(q_g.shape), _const_spec(w_uq.shape)]
    out_specs = [tok(MLA_KV_LORA), tok(MLA_ROPE), tok(MLA_HEADS * HEAD_PAD)]
    out_shape = [jax.ShapeDtypeStruct((b, l, MLA_KV_LORA), F32),
                 jax.ShapeDtypeStruct((b, l, MLA_ROPE), F32),
                 jax.ShapeDtypeStruct((b, l, MLA_HEADS * HEAD_PAD), BF16)]
    args = (x, mod, w_kv, kv_g, cossin, w_dq, q_g, w_uq)
    if with_kv:
        in_specs += [_const_spec(w_k.shape), _const_spec(w_vt.shape)]
        out_specs += [tok(MLA_HEADS * HEAD_PAD), pl.BlockSpec((1, MLA_HEADS * MLA_V, tm), lambda i, t: (i, 0, t))]
        out_shape += [jax.ShapeDtypeStruct((b, l, MLA_HEADS * HEAD_PAD), BF16),
                      jax.ShapeDtypeStruct((b, MLA_HEADS * MLA_V, l), BF16)]
        args += (w_k, w_vt)
    return pl.pallas_call(
        functools.partial(_mla_proj_kernel, with_kv=with_kv),
        grid=(b, l // tm),
        in_specs=in_specs,
        out_specs=out_specs,
        out_shape=out_shape,
        compiler_params=_params(("arbitrary", "arbitrary")),
        name="mla_proj_kv" if with_kv else "mla_proj",
    )(*args)


def _kv_head(h, lat, kpe, wk_ref, wvt_ref, k_ref, vt_ref):
    kn = jnp.dot(lat, wk_ref[:, h * MLA_NOPE:(h + 1) * MLA_NOPE], preferred_element_type=F32)
    k_ref[0, :, h * HEAD_PAD:h * HEAD_PAD + MLA_NOPE] = kn.astype(BF16)
    k_ref[0, :, h * HEAD_PAD + MLA_NOPE:(h + 1) * HEAD_PAD] = kpe
    vt_ref[0, h * MLA_V:(h + 1) * MLA_V, :] = _dg(wvt_ref[h], lat, NT).astype(BF16)


def _kv_up_kernel(ckv_ref, kpe_ref, wk_ref, wvt_ref, k_ref, vt_ref):
    lat = ckv_ref[0].astype(BF16)
    kpe = kpe_ref[0].astype(BF16)
    for h in range(MLA_HEADS):
        _kv_head(h, lat, kpe, wk_ref, wvt_ref, k_ref, vt_ref)


def _kv_up(ckv_all, kpe_pad, w_k, w_vt):
    b, lk, _ = ckv_all.shape
    tm = ATTN_TK if lk % ATTN_TK == 0 else lk
    tok = lambda w: pl.BlockSpec((1, tm, w), lambda i, t: (i, t, 0))
    return pl.pallas_call(
        _kv_up_kernel,
        grid=(b, lk // tm),
        in_specs=[tok(MLA_KV_LORA), tok(LANES), _const_spec(w_k.shape), _const_spec(w_vt.shape)],
        out_specs=[tok(MLA_HEADS * HEAD_PAD),
                   pl.BlockSpec((1, MLA_HEADS * MLA_V, tm), lambda i, t: (i, 0, t))],
        out_shape=[jax.ShapeDtypeStruct((b, lk, MLA_HEADS * HEAD_PAD), BF16),
                   jax.ShapeDtypeStruct((b, MLA_HEADS * MLA_V, lk), BF16)],
        compiler_params=_params(("arbitrary", "arbitrary")),
        name="kv_up",
    )(ckv_all, kpe_pad, w_k, w_vt)


ATTN_TQ = 512
ATTN_TK = 512
ATTN_HEADS_PER_STEP = 4
CHUNK_SHIFT = CHUNK.bit_length() - 1


def _chunk_end(pos):
    return (lax.shift_right_logical(pos, CHUNK_SHIFT) + 1) * CHUNK


def _attn_kernel(q_ref, k_ref, vt_ref, o_ref, acc_ref, m_ref, l_ref, sa_ref, sb_ref, *, tq, tk, past, lk, nh):
    i = pl.program_id(2)
    acc_ref[...] = jnp.zeros_like(acc_ref)
    m_ref[...] = jnp.full_like(m_ref, NEG_BIG)
    l_ref[...] = jnp.zeros_like(l_ref)
    q0 = past + i * tq
    n_full = lax.div(jnp.minimum(_chunk_end(q0), lk), tk)
    n_end = lax.div(jnp.minimum(_chunk_end(q0 + tq - 1), lk) + tk - 1, tk)
    n_blk = lk // tk
    heads = range(nh)
    ones = jnp.ones((2 * SUBLANES, tk), BF16)

    def key_start(j):
        return pl.multiple_of(jnp.minimum(j, n_blk - 1) * tk, tk) if lk > tk else 0

    def scores(j, dst):
        k0 = key_start(j)
        for h in heads:
            dst[h] = _dg(k_ref[0, pl.ds(k0, tk), h * HEAD_PAD:(h + 1) * HEAD_PAD],
                         q_ref[0, :, h * HEAD_PAD:(h + 1) * HEAD_PAD], NT)

    def step(j, src, dst, masked):
        if dst is not None:
            scores(j + 1, dst)
        k0 = key_start(j)
        st = [src[h] for h in heads]
        if masked:
            kc = lax.shift_right_logical(k0 + lax.broadcasted_iota(jnp.int32, (tk, tq), 0), CHUNK_SHIFT)
            qc = lax.shift_right_logical(q0 + lax.broadcasted_iota(jnp.int32, (tk, tq), 1), CHUNK_SHIFT)
            visible = kc <= qc
            st = [jnp.where(visible, s, NEG_BIG) for s in st]
        m_old = [m_ref[h] for h in heads]
        m_new = [jnp.maximum(m_old[h], jnp.max(st[h], axis=0, keepdims=True)) for h in heads]
        alpha = [jnp.exp2(m_old[h] - m_new[h]) for h in heads]
        pt = [jnp.exp2(st[h] - m_new[h]).astype(BF16) for h in heads]
        pv = [_dg(jnp.concatenate([vt_ref[0, h * MLA_V:(h + 1) * MLA_V, pl.ds(k0, tk)], ones], axis=0), pt[h])
              for h in heads]
        for h in heads:
            acc_ref[h] = alpha[h] * acc_ref[h] + pv[h][:MLA_V]
            m_ref[h] = m_new[h]
            l_ref[h] = alpha[h] * l_ref[h] + pv[h][MLA_V:MLA_V + 1]

    if n_blk == 1:
        scores(0, sa_ref)

        @pl.when(n_full == 1)
        def _():
            step(0, sa_ref, None, False)

        @pl.when(n_full == 0)
        def _():
            step(0, sa_ref, None, True)
    else:
        odd = lax.rem(n_full, 2)

        @pl.when(odd == 1)
        def _():
            scores(0, sb_ref)
            step(0, sb_ref, sa_ref, False)

        @pl.when(odd == 0)
        def _():
            scores(0, sa_ref)

        def pair(p, c):
            j = odd + 2 * p
            step(j, sa_ref, sb_ref, False)
            step(j + 1, sb_ref, sa_ref, False)
            return c

        lax.fori_loop(0, lax.div(n_full, 2), pair, 0)

        @pl.when(n_end > n_full)
        def _():
            step(n_full, sa_ref, None, True)

        def rest(j, c):
            scores(j, sa_ref)
            step(j, sa_ref, None, True)
            return c

        lax.fori_loop(n_full + 1, n_end, rest, 0)
    for h in heads:
        o_ref[0, :, h * MLA_V:(h + 1) * MLA_V] = (acc_ref[h] / l_ref[h]).T.astype(BF16)


def _attention(q, k, vt, past):
    b, lq, _ = q.shape
    lk = k.shape[1]
    lq_pad = -(-lq // LANES) * LANES
    if lq_pad != lq:
        q = jnp.pad(q, ((0, 0), (0, lq_pad - lq), (0, 0)))
    tq = _row_tile(lq_pad, ATTN_TQ)
    tk = ATTN_TK if lk % ATTN_TK == 0 else lk
    nh = ATTN_HEADS_PER_STEP
    kv_bytes = nh * lk * (HEAD_PAD + MLA_V) * jnp.dtype(BF16).itemsize
    kv_mode = pl.Buffered(2 if 2 * kv_bytes <= VMEM_LIMIT // 4 else 1)
    out = pl.pallas_call(
        functools.partial(_attn_kernel, tq=tq, tk=tk, past=past, lk=lk, nh=nh),
        grid=(b, MLA_HEADS // nh, lq_pad // tq),
        in_specs=[pl.BlockSpec((1, tq, nh * HEAD_PAD), lambda bi, h, i: (bi, i, h)),
                  pl.BlockSpec((1, lk, nh * HEAD_PAD), lambda bi, h, i: (bi, 0, h), pipeline_mode=kv_mode),
                  pl.BlockSpec((1, nh * MLA_V, lk), lambda bi, h, i: (bi, h, 0), pipeline_mode=kv_mode)],
        out_specs=pl.BlockSpec((1, tq, nh * MLA_V), lambda bi, h, i: (bi, i, h)),
        out_shape=jax.ShapeDtypeStruct((b, lq_pad, MLA_HEADS * MLA_V), BF16),
        scratch_shapes=[pltpu.VMEM((nh, MLA_V, tq), F32), pltpu.VMEM((nh, 1, tq), F32),
                        pltpu.VMEM((nh, 1, tq), F32), pltpu.VMEM((nh, tk, tq), F32),
                        pltpu.VMEM((nh, tk, tq), F32)],
        compiler_params=_params(("arbitrary", "arbitrary", "arbitrary")),
        name="chunk_causal_attention",
    )(q, k, vt)
    return out[:, :lq]


def _pad_hist(hist):
    return jnp.pad(hist, ((0, 0), (SUBLANES - hist.shape[1], 0), (0, 0)))


def _rot_half_cols(w):
    w1, w2 = jnp.split(w, 2, axis=-1)
    return jnp.concatenate([-w2, w1], axis=-1)


def _pack_weights(p):
    d = D_MODEL
    pk = {}
    w_in = p['gdn_w_in'][0]
    pk['gdn_w_main'] = w_in.astype(BF16)
    pk['gdn_w_ba'] = jnp.pad(w_in[:, GDN_CONV_CH + GDN_V:], ((0, 0), (0, LANES - 2 * GDN_HEADS))).astype(BF16)
    pad_gate = lambda a: jnp.pad(a.reshape(1, GDN_HEADS), ((0, 0), (GDN_HEADS, LANES - 2 * GDN_HEADS)))
    pk['gdn_alog'] = pad_gate(p['gdn_a_log'][0])
    pk['gdn_dtb'] = pad_gate(p['gdn_dt_bias'][0])
    pk['gdn_conv_w'] = p['gdn_conv_w'][0]
    pk['gdn_norm_g'] = p['gdn_norm_g'][0].reshape(1, GDN_DV)
    pk['gdn_w_out'] = p['gdn_w_out'][0].astype(BF16)
    wkv = p['kv_w_down']
    pk['kv_w_down'] = jnp.concatenate([wkv, _rot_half_cols(wkv[:, MLA_KV_LORA:])], axis=-1).astype(BF16)
    pk['kv_norm_g'] = p['kv_norm_g'].reshape(1, MLA_KV_LORA)
    w_up = p['kv_w_up'].reshape(MLA_KV_LORA, MLA_HEADS, MLA_NOPE + MLA_V)
    pk['kv_w_k'] = w_up[..., :MLA_NOPE].reshape(MLA_KV_LORA, MLA_HEADS * MLA_NOPE).astype(BF16)
    pk['kv_w_vt'] = jnp.transpose(w_up[..., MLA_NOPE:], (1, 2, 0)).astype(BF16)
    pk['mla_w_dq'] = p['mla_w_dq'][0].astype(BF16)
    pk['mla_q_norm_g'] = p['mla_q_norm_g'][0].reshape(1, MLA_Q_LORA)
    wuq = p['mla_w_uq'][0].reshape(MLA_Q_LORA, MLA_HEADS, MLA_NOPE + MLA_ROPE)
    wuq = jnp.concatenate([wuq, _rot_half_cols(wuq[..., MLA_NOPE:])], axis=-1)
    pk['mla_w_uq'] = wuq.reshape(MLA_Q_LORA, MLA_HEADS * HEAD_PAD).astype(BF16)
    pk['mla_w_out'] = p['mla_w_out'][0].astype(BF16)
    pk['ffn_w_in'] = p['ffn_w_in'].astype(BF16)
    pk['ffn_b_in'] = p['ffn_b_in'].reshape(DEPTH, 1, 2 * D_FF)
    pk['ffn_conv_w'] = p['ffn_conv_w']
    pk['ffn_conv_b'] = p['ffn_conv_b'].reshape(DEPTH, 1, 2 * D_FF)
    pk['ffn_w_down'] = p['ffn_w_down'].astype(BF16)
    pk['ln_g'] = p['ln_g'].reshape(2 * DEPTH, 1, d)
    pk['ln_b'] = p['ln_b'].reshape(2 * DEPTH, 1, d)
    return pk


def _rope_table(past, l):
    inv = 1.0 / (ROPE_THETA ** (jnp.arange(0, MLA_ROPE, 2, dtype=F32) / MLA_ROPE))
    ang = (past + jnp.arange(l, dtype=jnp.int32)).astype(F32)[:, None] * inv[None, :]
    ang = jnp.concatenate([ang, ang], axis=-1)
    return jnp.concatenate([jnp.cos(ang), jnp.sin(ang)], axis=-1)


def _block_tail(o, w_out, x, mods, hist, pk, layer):
    y, hist8 = _mix_ffn(o, x, mods[2 * layer], w_out, pk['ln_g'][2 * layer], pk['ln_b'][2 * layer],
                        mods[2 * layer + 1], pk['ffn_w_in'], pk['ffn_b_in'][layer],
                        pk['ffn_conv_w'][layer], pk['ffn_conv_b'][layer], _pad_hist(hist),
                        pk['ffn_w_down'], pk['ln_g'][2 * layer + 1], pk['ln_b'][2 * layer + 1], layer)
    return y, hist8[:, SUBLANES - (FFN_CONV - 1):]


def _trunk(x, mods, gdn_state, gdn_conv, ffn_conv, ckv_past, kpe_past, pk):
    b, l, _ = x.shape
    past = kpe_past.shape[1]
    c_len = CHUNK if l % CHUNK == 0 else l
    n = l // c_len

    q, k, v, z, gb, hist8 = _gdn_in(x, mods[0], pk['gdn_w_main'], pk['gdn_w_ba'], _pad_hist(gdn_conv[:, 0]),
                                    pk['gdn_conv_w'], pk['gdn_alog'], pk['gdn_dtb'], c_len)
    col = gb.reshape(b, n, c_len, 2 * GDN_HEADS)
    row = jnp.swapaxes(col, 2, 3)
    o, s_end = _gdn_chunk(q, k, v, z, pk['gdn_norm_g'], col, row, gdn_state[:, 0], c_len)
    gdn_conv_out = hist8[:, None, SUBLANES - (GDN_CONV - 1):]
    x, fh0 = _block_tail(o, pk['gdn_w_out'], x, mods, ffn_conv[:, 0], pk, 0)

    proj = _mla_proj(x, mods[2], pk['kv_w_down'], pk['kv_norm_g'], _rope_table(past, l),
                     pk['mla_w_dq'], pk['mla_q_norm_g'], pk['mla_w_uq'], pk['kv_w_k'], pk['kv_w_vt'],
                     with_kv=(past == 0))
    ckv_new, kpe_new, qm = proj[:3]
    if past == 0:
        km, vm = proj[3:]
    else:
        ckv_all = jnp.concatenate([ckv_past, ckv_new], axis=1)
        kpe_all = jnp.concatenate([kpe_past, kpe_new], axis=1)
        kpe_pad = jnp.pad(kpe_all, ((0, 0), (0, 0), (0, LANES - MLA_ROPE)))
        km, vm = _kv_up(ckv_all, kpe_pad, pk['kv_w_k'], pk['kv_w_vt'])
    o = _attention(qm, km, vm, past)
    x, fh1 = _block_tail(o, pk['mla_w_out'], x, mods, ffn_conv[:, 1], pk, 1)

    return (x, s_end[:, None], gdn_conv_out, jnp.stack([fh0, fh1], axis=1), ckv_new, kpe_new)


def kernel(x_prompt, x_sample, c_prompt, c_sample, state_gdn, state_gdn_conv, state_ffn_conv, cache_ckv, cache_kpe, ada_w, ada_b, ln_g, ln_b, gdn_w_in, gdn_conv_w, gdn_a_log, gdn_dt_bias, gdn_norm_g, gdn_w_out, kv_w_down, kv_norm_g, kv_w_up, mla_w_dq, mla_q_norm_g, mla_w_uq, mla_w_out, ffn_w_in, ffn_b_in, ffn_conv_w, ffn_conv_b, ffn_w_down):
    p = {'ln_g': ln_g, 'ln_b': ln_b, 'gdn_w_in': gdn_w_in, 'gdn_conv_w': gdn_conv_w, 'gdn_a_log': gdn_a_log,
         'gdn_dt_bias': gdn_dt_bias, 'gdn_norm_g': gdn_norm_g, 'gdn_w_out': gdn_w_out,
         'kv_w_down': kv_w_down, 'kv_norm_g': kv_norm_g, 'kv_w_up': kv_w_up,
         'mla_w_dq': mla_w_dq, 'mla_q_norm_g': mla_q_norm_g, 'mla_w_uq': mla_w_uq, 'mla_w_out': mla_w_out,
         'ffn_w_in': ffn_w_in, 'ffn_b_in': ffn_b_in, 'ffn_conv_w': ffn_conv_w,
         'ffn_conv_b': ffn_conv_b, 'ffn_w_down': ffn_w_down}
    pk = _pack_weights(p)
    bp = x_prompt.shape[0]
    mods = _ada_terms(jnp.concatenate([c_prompt, c_sample], axis=0), ada_w, ada_b)[:, :, None, :]
    zeros_like_b = lambda a: jnp.zeros((bp,) + a.shape[1:], a.dtype)
    out_p = _trunk(x_prompt, mods[:, :bp], zeros_like_b(state_gdn), zeros_like_b(state_gdn_conv),
                   zeros_like_b(state_ffn_conv), jnp.zeros((bp, 0, MLA_KV_LORA), cache_ckv.dtype),
                   jnp.zeros((bp, 0, MLA_ROPE), cache_kpe.dtype), pk)
    out_s = _trunk(x_sample, mods[:, bp:], state_gdn, state_gdn_conv, state_ffn_conv, cache_ckv, cache_kpe, pk)
    return (out_p[0], out_s[0]) + out_p[1:] + out_s[1:]
```

```python
import functools

import jax
import jax.numpy as jnp
from jax import lax
from jax.experimental import pallas as pl
from jax.experimental.pallas import tpu as pltpu

F32 = jnp.float32
BF16 = jnp.bfloat16

D_MODEL = 1024
DEPTH = 2
CHUNK = 64
ALPHA = (2.0 * DEPTH) ** 0.25
LN_EPS = 1e-5
RMS_EPS = 1e-6
GDN_HEADS = 8
GDN_DK = 128
GDN_DV = 128
GDN_CONV = 4
GDN_QK = GDN_HEADS * GDN_DK
GDN_V = GDN_HEADS * GDN_DV
GDN_CONV_CH = 2 * GDN_QK + GDN_V
MLA_HEADS = 8
MLA_NOPE = 128
MLA_ROPE = 64
MLA_V = 128
MLA_KV_LORA = 256
MLA_Q_LORA = 384
ROPE_THETA = 10000.0
MLA_SCALE = (MLA_NOPE + MLA_ROPE) ** -0.5
Q_PRESCALE = MLA_SCALE * 1.4426950408889634
D_FF = 2816
FFN_CONV = 3

LANES = 128
SUBLANES = 8
BF16_ROWS = 2 * SUBLANES
MXU_DIM = 256
HEAD_PAD = 256
VMEM_LIMIT = 56 * 1024 * 1024
PROJ_ROWS = 512
NEG_BIG = -1e30

NN = ((1,), (0,))
NT = ((1,), (1,))
TN = ((0,), (0,))


def _row_tile(n, cap):
    if n <= cap:
        return n
    for t in range(cap, BF16_ROWS - 1, -1):
        if n % t == 0 and t % BF16_ROWS == 0:
            return t
    raise ValueError(f"no row tile for {n}")


def _split3(a):
    hi = a.astype(BF16)
    r = a - hi.astype(F32)
    mid = r.astype(BF16)
    lo = (r - mid.astype(F32)).astype(BF16)
    return hi, mid, lo


def _dg(a, b, dims=NN):
    return lax.dot_general(a, b, (dims, ((), ())), preferred_element_type=F32)


def _dot_hp(a, b, dims=NN):
    a0, a1 = a.astype(BF16), (a - a.astype(BF16).astype(F32)).astype(BF16)
    b0, b1 = b.astype(BF16), (b - b.astype(BF16).astype(F32)).astype(BF16)
    return _dg(a0, b0, dims) + (_dg(a0, b1, dims) + _dg(a1, b0, dims))


def _sigmoid(x):
    return 1.0 / (1.0 + jnp.exp(-x))


def _silu(x):
    return x * _sigmoid(x)


def _layer_norm(y, g, b):
    mu = jnp.mean(y, axis=-1, keepdims=True)
    yc = y - mu
    var = jnp.mean(yc * yc, axis=-1, keepdims=True)
    return yc * lax.rsqrt(var + LN_EPS) * g + b


def _const_spec(shape):
    nd = len(shape)
    return pl.BlockSpec(shape, lambda *_: (0,) * nd, pipeline_mode=pl.Buffered(1))


def _shift_rows(u, carry8, j):
    lead = u.shape[:-2]
    rows, width = u.shape[-2:]
    groups = rows // SUBLANES
    ax = len(lead)
    rot = pltpu.roll(u.reshape(lead + (groups, SUBLANES, width)), j, axis=ax + 1)
    first = jnp.expand_dims(pltpu.roll(carry8, j, axis=ax), ax)
    above = jnp.concatenate([first, lax.slice_in_dim(rot, 0, groups - 1, axis=ax)], axis=ax)
    sub = lax.broadcasted_iota(jnp.int32, rot.shape, ax + 1)
    return jnp.where(sub < j, above, rot).reshape(u.shape)


def _params(sem):
    return pltpu.CompilerParams(dimension_semantics=sem, vmem_limit_bytes=VMEM_LIMIT)


ADA_COLS = 768


def _ada_kernel(c_ref, w_ref, b_ref, o_ref):
    s = _silu(c_ref[...])
    o_ref[0] = _dot_hp(s, w_ref[0]) + b_ref[0]


def _ada_terms(c_all, ada_w, ada_b):
    n_sub, d, n3 = ada_w.shape
    bc = c_all.shape[0]
    tn = ADA_COLS
    return pl.pallas_call(
        _ada_kernel,
        grid=(n_sub, n3 // tn),
        in_specs=[
            pl.BlockSpec((bc, d), lambda i, j: (0, 0)),
            pl.BlockSpec((1, d, tn), lambda i, j: (i, 0, j)),
            pl.BlockSpec((1, 1, tn), lambda i, j: (i, 0, j)),
        ],
        out_specs=pl.BlockSpec((1, bc, tn), lambda i, j: (i, 0, j)),
        out_shape=jax.ShapeDtypeStruct((n_sub, bc, n3), F32),
        compiler_params=_params(("arbitrary", "arbitrary")),
        name="ada_terms",
    )(c_all, ada_w, ada_b.reshape(n_sub, 1, n3))


GDN_IN_TILE = 256


def _gdn_in_kernel(x_ref, mod_ref, w_ref, wba_ref, hist_ref, cw_ref, alog_ref, dtb_ref, tri_ref,
                   q_ref, k_ref, v_ref, z_ref, gb_ref, hist_out_ref, carry, *, tm):
    t = pl.program_id(1)

    @pl.when(t == 0)
    def _():
        carry[...] = hist_ref[0]

    shift = mod_ref[0, :, 0:D_MODEL]
    scale = mod_ref[0, :, D_MODEL:2 * D_MODEL]
    hin = (x_ref[0] * (1.0 + scale) + shift).astype(BF16)

    outs = (q_ref, k_ref, v_ref)

    def conv_tile(sec, c):
        lo = sec * GDN_QK + c * GDN_IN_TILE
        cols = slice(lo, lo + GDN_IN_TILE)
        ocols = slice(c * GDN_IN_TILE, (c + 1) * GDN_IN_TILE)
        u = jnp.dot(hin, w_ref[:, cols], preferred_element_type=F32)
        c8 = carry[:, cols]
        acc = u * cw_ref[GDN_CONV - 1:GDN_CONV, cols]
        for j in range(1, GDN_CONV):
            acc = acc + _shift_rows(u, c8, j) * cw_ref[GDN_CONV - 1 - j:GDN_CONV - j, cols]
        carry[:, cols] = u[tm - SUBLANES:tm]
        s = _silu(acc)
        if sec == 2:
            v_ref[0, :, ocols] = s.astype(BF16)
        else:
            post = GDN_DK ** -0.5 if sec == 0 else 1.0
            parts = []
            for h in range(GDN_IN_TILE // GDN_DK):
                sh = s[:, h * GDN_DK:(h + 1) * GDN_DK]
                inv = lax.rsqrt(jnp.sum(sh * sh, axis=-1, keepdims=True) + RMS_EPS)
                parts.append(sh * (inv * post))
            outs[sec][0, :, ocols] = jnp.concatenate(parts, axis=-1).astype(BF16)

    def z_tile(c):
        ocols = slice(c * GDN_IN_TILE, (c + 1) * GDN_IN_TILE)
        cols = slice(GDN_CONV_CH + c * GDN_IN_TILE, GDN_CONV_CH + (c + 1) * GDN_IN_TILE)
        z_ref[0, :, ocols] = jnp.dot(hin, w_ref[:, cols], preferred_element_type=F32).astype(BF16)

    def gates():
        ba = jnp.dot(hin, wba_ref[...], preferred_element_type=F32)
        beta = _sigmoid(ba)
        xs = ba + dtb_ref[...]
        softplus = jnp.maximum(xs, 0.0) + jnp.log(1.0 + jnp.exp(-jnp.abs(xs)))
        g = -jnp.exp(alog_ref[...]) * softplus
        tri = tri_ref[...]
        g0, g1, g2 = _split3(g)
        gsum = (jnp.dot(tri, g0, preferred_element_type=F32)
                + jnp.dot(tri, g1, preferred_element_type=F32)
                + jnp.dot(tri, g2, preferred_element_type=F32))
        lane = lax.broadcasted_iota(jnp.int32, (tm, LANES), 1)
        gb_ref[0] = jnp.where(lane < GDN_HEADS, beta, gsum)[:, :2 * GDN_HEADS]

    for c in range(GDN_QK // GDN_IN_TILE):
        conv_tile(0, c)
        z_tile(c)
        conv_tile(1, c)
        if c == 0:
            gates()
        conv_tile(2, c)
    hist_out_ref[0] = carry[...]


def _gdn_in(x, mod, w_main, w_ba, hist8, conv_w, alog_row, dtb_row, c_len):
    b, l, d = x.shape
    tm = _row_tile(l, PROJ_ROWS)
    assert tm % c_len == 0
    idx = jnp.arange(tm, dtype=jnp.int32)
    tri = ((idx[:, None] >= idx[None, :]) & ((idx[:, None] // c_len) == (idx[None, :] // c_len))).astype(BF16)
    tok = lambda w: pl.BlockSpec((1, tm, w), lambda i, t: (i, t, 0))
    return pl.pallas_call(
        functools.partial(_gdn_in_kernel, tm=tm),
        grid=(b, l // tm),
        in_specs=[
            tok(d),
            pl.BlockSpec((1, 1, 3 * d), lambda i, t: (i, 0, 0)),
            _const_spec(w_main.shape),
            _const_spec(w_ba.shape),
            pl.BlockSpec((1, SUBLANES, GDN_CONV_CH), lambda i, t: (i, 0, 0)),
            _const_spec(conv_w.shape),
            _const_spec(alog_row.shape),
            _const_spec(dtb_row.shape),
            _const_spec(tri.shape),
        ],
        out_specs=[tok(GDN_QK), tok(GDN_QK), tok(GDN_V), tok(GDN_V), tok(2 * GDN_HEADS),
                   pl.BlockSpec((1, SUBLANES, GDN_CONV_CH), lambda i, t: (i, 0, 0))],
        out_shape=[
            jax.ShapeDtypeStruct((b, l, GDN_QK), BF16),
            jax.ShapeDtypeStruct((b, l, GDN_QK), BF16),
            jax.ShapeDtypeStruct((b, l, GDN_V), BF16),
            jax.ShapeDtypeStruct((b, l, GDN_V), BF16),
            jax.ShapeDtypeStruct((b, l, 2 * GDN_HEADS), F32),
            jax.ShapeDtypeStruct((b, SUBLANES, GDN_CONV_CH), F32),
        ],
        scratch_shapes=[pltpu.VMEM((SUBLANES, GDN_CONV_CH), F32)],
        compiler_params=_params(("arbitrary", "arbitrary")),
        name="gdn_in",
    )(x, mod, w_main, w_ba, hist8, conv_w, alog_row, dtb_row, tri)


GDN_GROUP = 4


def _bf(xs):
    return [x.astype(BF16) for x in xs]


def _drain(gen):
    for _ in gen:
        pass


def _interleave(major, n_major, minor, n_minor):
    done = 0
    for i in range(n_major):
        next(major, None)
        while done * n_major < (i + 1) * n_minor:
            next(minor, None)
            done += 1
    _drain(major)
    _drain(minor)


def _gdn_chunk_kernel(q_ref, k_ref, v_ref, z_ref, ng_ref, col_ref, row_ref, s0_ref, o_ref, s_ref, *,
                      c_len, group):
    @pl.when(pl.program_id(1) == 0)
    def _():
        s_ref[...] = s0_ref[...]

    ri = lax.broadcasted_iota(jnp.int32, (c_len, c_len), 0)
    ci = lax.broadcasted_iota(jnp.int32, (c_len, c_len), 1)
    incl = ri >= ci
    strict = ri > ci
    n_sq = c_len.bit_length() - 2
    eye = jnp.where(ri == ci, 1.0, 0.0)
    corner = [(lax.shift_right_logical(ri, l + 1) == lax.shift_right_logical(ci, l + 1))
              & ((ri & (1 << l)) != 0) & ((ci & (1 << l)) == 0) for l in range(n_sq + 1)]
    heads = range(GDN_HEADS)
    rows = lambda g: slice(g * c_len, (g + 1) * c_len)
    lanes = lambda h: slice(h * GDN_DK, (h + 1) * GDN_DK)

    def prepare(chunks, w):
        items = [(g, h) for g in chunks for h in heads]
        each = range(len(items))
        col = {g: col_ref[0, g] for g in chunks}
        row = {g: row_ref[0, g] for g in chunks}
        w['qb'] = [q_ref[0, rows(g), lanes(h)] for g, h in items]
        w['kb'] = [k_ref[0, rows(g), lanes(h)] for g, h in items]
        w['k'] = [x.astype(F32) for x in w['kb']]
        w['v'] = [v_ref[0, rows(g), lanes(h)].astype(F32) for g, h in items]
        wide = (c_len, GDN_DK)
        w['beta'] = [jnp.broadcast_to(col[g][:, h:h + 1], wide) for g, h in items]
        w['gc'] = [jnp.broadcast_to(col[g][:, GDN_HEADS + h:GDN_HEADS + h + 1], wide) for g, h in items]
        gr = [row[g][GDN_HEADS + h:GDN_HEADS + h + 1, :] for g, h in items]
        yield
        w['gam'] = [jnp.where(incl, jnp.exp(jnp.where(incl, w['gc'][i][:, :c_len] - gr[i], 0.0)), 0.0) for i in each]
        w['eg'] = [jnp.exp(w['gc'][i]) for i in each]
        yield
        w['sol'] = [jnp.concatenate([w['beta'][i] * w['v'][i], w['beta'][i] * w['k'][i] * w['eg'][i]], axis=-1)
                    for i in each]
        g_last = [w['gc'][i][c_len - 1:c_len, :] for i in each]
        w['k_dec'] = _bf([w['k'][i] * jnp.exp(g_last[i] - w['gc'][i]) for i in each])
        w['decay'] = [jnp.exp(g_last[i]) for i in each]
        yield

    n_prepare = 3

    def solve(w):
        each = range(len(w['qb']))
        qkk = [_dg(jnp.concatenate([w['qb'][i], w['kb'][i]], axis=0), w['kb'][i], NT) for i in each]
        yield
        a = [jnp.where(strict, w['beta'][i][:, :c_len] * qkk[i][c_len:] * w['gam'][i], 0.0) for i in each]
        w['att'] = _bf([qkk[i][:c_len] * w['gam'][i] for i in each])
        t = [eye - jnp.where(corner[0], a[i], 0.0) for i in each]
        yield
        for lvl in range(1, n_sq + 1):
            a_s = _bf([jnp.where(corner[lvl], a[i], 0.0) for i in each])
            tb = _bf(t)
            at = _bf([_dg(a_s[i], tb[i]) for i in each])
            yield
            t = [t[i] - _dg(tb[i], at[i]) for i in each]
            yield
        tb = _bf(t)
        solb = _bf(w['sol'])
        sol = [_dg(tb[i], solb[i]) for i in each]
        w['u'] = [sol[i][:, :GDN_DV] for i in each]
        w['wq'] = [jnp.concatenate([sol[i][:, GDN_DV:].astype(BF16),
                                    (w['qb'][i].astype(F32) * w['eg'][i]).astype(BF16)], axis=0)
                   for i in each]

    n_solve = 2 + 2 * n_sq

    state = {'s': [s_ref[0, h] for h in heads]}

    def recur(chunks, w):
        for n, g in enumerate(chunks):
            it = [n * GDN_HEADS + h for h in heads]
            sb = _bf(state['s'])
            ws = [_dg(w['wq'][it[h]], sb[h]) for h in heads]
            yield
            vb = _bf([w['u'][it[h]] - ws[h][:c_len] for h in heads])
            o = [ws[h][c_len:] + _dg(w['att'][it[h]], vb[h]) for h in heads]
            state['s'] = [w['decay'][it[h]] * state['s'][h] + _dg(w['k_dec'][it[h]], vb[h], TN) for h in heads]
            yield
            zg = [_silu(z_ref[0, rows(g), lanes(h)].astype(F32)) for h in heads]
            o = [o[h] * lax.rsqrt(jnp.mean(o[h] * o[h], axis=-1, keepdims=True) + RMS_EPS) * ng_ref[...] * zg[h]
                 for h in heads]
            o_ref[0, rows(g), :] = jnp.concatenate(o, axis=-1).astype(BF16)

    n_waves = 2 if group % 2 == 0 else 1
    per = group // n_waves
    waves = [list(range(n * per, (n + 1) * per)) for n in range(n_waves)]
    data = [{} for _ in waves]
    _drain(prepare(waves[0], data[0]))
    for n in range(n_waves):
        side = []
        n_side = 0
        if n + 1 < n_waves:
            side.append(prepare(waves[n + 1], data[n + 1]))
            n_side += n_prepare
        if n >= 1:
            side.append(recur(waves[n - 1], data[n - 1]))
            n_side += 2 * per

        def chain(gens=side):
            for gen in gens:
                yield from gen

        _interleave(solve(data[n]), n_solve, chain(), n_side)
    _drain(recur(waves[-1], data[-1]))
    for h in heads:
        s_ref[0, h] = state['s'][h]


def _gdn_chunk(q, k, v, z, norm_g, col, row, s0, c_len):
    b, l, _ = q.shape
    n = l // c_len
    group = GDN_GROUP if n % GDN_GROUP == 0 else 1
    rows = c_len * group
    tok = pl.BlockSpec((1, rows, GDN_QK), lambda i, t: (i, t, 0))
    st = pl.BlockSpec((1, GDN_HEADS, GDN_DK, GDN_DV), lambda i, t: (i, 0, 0, 0))
    return pl.pallas_call(
        functools.partial(_gdn_chunk_kernel, c_len=c_len, group=group),
        grid=(b, n // group),
        in_specs=[tok, tok, tok, tok, _const_spec(norm_g.shape),
                  pl.BlockSpec((1, group, c_len, 2 * GDN_HEADS), lambda i, t: (i, t, 0, 0)),
                  pl.BlockSpec((1, group, 2 * GDN_HEADS, c_len), lambda i, t: (i, t, 0, 0)),
                  st],
        out_specs=[tok, st],
        out_shape=[jax.ShapeDtypeStruct((b, l, GDN_V), BF16),
                   jax.ShapeDtypeStruct(s0.shape, F32)],
        compiler_params=_params(("arbitrary", "arbitrary")),
        name="gdn_chunk",
    )(q, k, v, z, norm_g, col, row, s0)


FFN_TILE = MXU_DIM
FFN_ROWS = 512
FFN_SUB_TILES = 2


def _mix_ffn_kernel(o_ref, x_ref, mmod_ref, wo_ref, mlg_ref, mlb_ref, fmod_ref, win_ref, bin_ref, cw_ref, cb_ref,
                    hist_ref, wdn_ref, lg_ref, lb_ref, y_ref, hist_out_ref, carry, x1_ref, hbuf, fbuf, *,
                    nb, tm, n_sub):
    t = pl.program_id(1)

    @pl.when(t == 0)
    def _():
        carry[...] = hist_ref[...]

    sub = tm // n_sub
    subs = range(n_sub)
    rows = lambda s: slice(s * sub, (s + 1) * sub)
    lead = slice(None) if nb > 1 else 0
    per_seq = lambda v: v.reshape(nb, sub, v.shape[-1]) if nb > 1 else v
    slab = lambda v: v.reshape(nb * sub, v.shape[-1]) if nb > 1 else v
    mgate = mmod_ref[lead, :, 2 * D_MODEL:3 * D_MODEL]
    shift = fmod_ref[lead, :, 0:D_MODEL]
    scale = fmod_ref[lead, :, D_MODEL:2 * D_MODEL]
    gate = fmod_ref[lead, :, 2 * D_MODEL:3 * D_MODEL]

    def mixer_proj(s):
        return jnp.dot(slab(o_ref[lead, rows(s), :]), wo_ref[...], preferred_element_type=F32)

    def mixer_norm(s, mix):
        x1 = _layer_norm(ALPHA * x_ref[lead, rows(s), :] + (1.0 + mgate) * per_seq(mix), mlg_ref[...], mlb_ref[...])
        x1_ref[s] = slab(x1)
        hbuf[s] = slab(x1 * (1.0 + scale) + shift).astype(BF16)

    def conv_half(s, lo):
        cols = slice(lo, lo + FFN_TILE)
        u = per_seq(jnp.dot(hbuf[s], win_ref[:, cols], preferred_element_type=F32) + bin_ref[:, cols])
        c8 = carry[lead, :, cols]
        out = u * cw_ref[FFN_CONV - 1:FFN_CONV, cols] + cb_ref[:, cols]
        for j in range(1, FFN_CONV):
            out = out + _shift_rows(u, c8, j) * cw_ref[FFN_CONV - 1 - j:FFN_CONV - j, cols]
        carry[lead, :, cols] = u[..., sub - SUBLANES:sub, :]
        return slab(out)

    def ffn_norm(s, f):
        y = _layer_norm(ALPHA * per_seq(x1_ref[s]) + (1.0 + gate) * per_seq(f), lg_ref[...], lb_ref[...])
        y_ref[lead, rows(s), :] = y

    mix = {}
    for s in subs:
        mix[s] = mixer_proj(s)
        if s >= 1:
            mixer_norm(s - 1, mix.pop(s - 1))
    mixer_norm(n_sub - 1, mix.pop(n_sub - 1))

    for i in range(D_FF // FFN_TILE):
        lo = i * FFN_TILE
        for s in subs:
            ua = conv_half(s, lo)
            ub = conv_half(s, lo + D_FF)
            fbuf[s, :, lo:lo + FFN_TILE] = (_silu(ua) * ub).astype(BF16)
    hist_out_ref[...] = carry[...]

    down = {}
    for s in subs:
        down[s] = jnp.dot(fbuf[s], wdn_ref[...], preferred_element_type=F32)
        if s >= 1:
            ffn_norm(s - 1, down.pop(s - 1))
    ffn_norm(n_sub - 1, down.pop(n_sub - 1))


def _layer_spec(stacked, layer):
    nd = stacked.ndim - 1
    return pl.BlockSpec((None,) + stacked.shape[1:], lambda *_: (layer,) + (0,) * nd, pipeline_mode=pl.Buffered(1))


def _mix_ffn(o, x, mix_mod, w_out, mix_ln_g, mix_ln_b,
             ffn_mod, w_in_all, b_in, conv_w, conv_b, hist8, w_down_all, ln_g, ln_b, layer):
    b, l, d = x.shape
    tm = _row_tile(l, FFN_ROWS)
    n_sub = FFN_SUB_TILES if tm == FFN_ROWS else 1
    nb = b if (tm == l and b * tm <= FFN_ROWS and tm % BF16_ROWS == 0) else 1
    tok = pl.BlockSpec((nb, tm, d), lambda i, t: (i, t, 0))
    modspec = pl.BlockSpec((nb, 1, 3 * d), lambda i, t: (i, 0, 0))
    hspec = pl.BlockSpec((nb, SUBLANES, 2 * D_FF), lambda i, t: (i, 0, 0))
    row = _const_spec((1, d))
    return pl.pallas_call(
        functools.partial(_mix_ffn_kernel, nb=nb, tm=tm, n_sub=n_sub),
        grid=(b // nb, l // tm),
        in_specs=[tok, tok, modspec, _const_spec(w_out.shape), row, row,
                  modspec, _layer_spec(w_in_all, layer), _const_spec(b_in.shape),
                  _const_spec(conv_w.shape), _const_spec(conv_b.shape), hspec,
                  _layer_spec(w_down_all, layer), row, row],
        out_specs=[tok, hspec],
        out_shape=[jax.ShapeDtypeStruct((b, l, d), F32),
                   jax.ShapeDtypeStruct((b, SUBLANES, 2 * D_FF), F32)],
        scratch_shapes=[pltpu.VMEM((nb, SUBLANES, 2 * D_FF), F32),
                        pltpu.VMEM((n_sub, nb * tm // n_sub, d), F32),
                        pltpu.VMEM((n_sub, nb * tm // n_sub, d), BF16),
                        pltpu.VMEM((n_sub, nb * tm // n_sub, D_FF), BF16)],
        compiler_params=_params(("arbitrary", "arbitrary")),
        name="mix_ffn",
    )(o, x, mix_mod, w_out, mix_ln_g, mix_ln_b,
      ffn_mod, w_in_all, b_in, conv_w, conv_b, hist8, w_down_all, ln_g, ln_b)


def _rope_pair(pair, cs):
    prod = pair * cs
    return prod + pltpu.roll(prod, MLA_ROPE, axis=1)


def _mla_proj_kernel(x_ref, mod_ref, wkv_ref, kvg_ref, cs_ref, wdq_ref, qg_ref, wuq_ref, *rest, with_kv):
    if with_kv:
        wk_ref, wvt_ref, ckv_ref, kpe_ref, q_ref, k_ref, vt_ref = rest
    else:
        ckv_ref, kpe_ref, q_ref = rest
    x = x_ref[0]
    cs = cs_ref[...]
    lane = lax.broadcasted_iota(jnp.int32, (x.shape[0], LANES), 1)
    kv = jnp.dot(x.astype(BF16), wkv_ref[...], preferred_element_type=F32)
    lat = kv[:, :MLA_KV_LORA]
    ckv = lat * lax.rsqrt(jnp.mean(lat * lat, axis=-1, keepdims=True) + RMS_EPS) * kvg_ref[...]
    kpe = _rope_pair(kv[:, MLA_KV_LORA:], cs)
    ckv_ref[0] = ckv
    kpe_ref[0] = kpe[:, :MLA_ROPE]
    if with_kv:
        lat_b = ckv.astype(BF16)
        kpe_b = jnp.where(lane < MLA_ROPE, kpe, 0.0).astype(BF16)

    shift = mod_ref[0, :, 0:D_MODEL]
    scale = mod_ref[0, :, D_MODEL:2 * D_MODEL]
    hin = (x * (1.0 + scale) + shift).astype(BF16)
    qd = jnp.dot(hin, wdq_ref[...], preferred_element_type=F32)
    qd = qd * lax.rsqrt(jnp.mean(qd * qd, axis=-1, keepdims=True) + RMS_EPS) * qg_ref[...]
    qd = qd.astype(BF16)
    for h in range(MLA_HEADS):
        qh = jnp.dot(qd, wuq_ref[:, h * HEAD_PAD:(h + 1) * HEAD_PAD], preferred_element_type=F32)
        q_ref[0, :, h * HEAD_PAD:h * HEAD_PAD + MLA_NOPE] = (qh[:, :MLA_NOPE] * Q_PRESCALE).astype(BF16)
        pe = jnp.where(lane < MLA_ROPE, _rope_pair(qh[:, MLA_NOPE:], cs) * Q_PRESCALE, 0.0)
        q_ref[0, :, h * HEAD_PAD + MLA_NOPE:(h + 1) * HEAD_PAD] = pe.astype(BF16)
        if with_kv:
            _kv_head(h, lat_b, kpe_b, wk_ref, wvt_ref, k_ref, vt_ref)


def _mla_proj(x, mod, w_kv, kv_g, cossin, w_dq, q_g, w_uq, w_k, w_vt, with_kv):
    b, l, d = x.shape
    tm = _row_tile(l, PROJ_ROWS)
    tok = lambda w: pl.BlockSpec((1, tm, w), lambda i, t: (i, t, 0))
    in_specs = [tok(d), pl.BlockSpec((1, 1, 3 * d), lambda i, t: (i, 0, 0)),
                _const_spec(w_kv.shape), _const_spec(kv_g.shape),
                pl.BlockSpec((tm, LANES), lambda i, t: (t, 0)),
                _const_spec(w_dq.shape), _const_spec(q_g.shape), _const_spec(w_uq.shape)]
    out_specs = [tok(MLA_KV_LORA), tok(MLA_ROPE), tok(MLA_HEADS * HEAD_PAD)]
    out_shape = [jax.ShapeDtypeStruct((b, l, MLA_KV_LORA), F32),
                 jax.ShapeDtypeStruct((b, l, MLA_ROPE), F32),
                 jax.ShapeDtypeStruct((b, l, MLA_HEADS * HEAD_PAD), BF16)]
    args = (x, mod, w_kv, kv_g, cossin, w_dq, q_g, w_uq)
    if with_kv:
        in_specs += [_const_spec(w_k.shape), _const_spec(w_vt.shape)]
        out_specs += [tok(MLA_HEADS * HEAD_PAD), pl.BlockSpec((1, MLA_HEADS * MLA_V, tm), lambda i, t: (i, 0, t))]
        out_shape += [jax.ShapeDtypeStruct((b, l, MLA_HEADS * HEAD_PAD), BF16),
                      jax.ShapeDtypeStruct((b, MLA_HEADS * MLA_V, l), BF16)]
        args += (w_k, w_vt)
    return pl.pallas_call(
        functools.partial(_mla_proj_kernel, with_kv=with_kv),
        grid=(b, l // tm),
        in_specs=in_specs,
        out_specs=out_specs,
        out_shape=out_shape,
        compiler_params=_params(("arbitrary", "arbitrary")),
        name="mla_proj_kv" if with_kv else "mla_proj",
    )(*args)


def _kv_head(h, lat, kpe, wk_ref, wvt_ref, k_ref, vt_ref):
    kn = jnp.dot(lat, wk_ref[:, h * MLA_NOPE:(h + 1) * MLA_NOPE], preferred_element_type=F32)
    k_ref[0, :, h * HEAD_PAD:h * HEAD_PAD + MLA_NOPE] = kn.astype(BF16)
    k_ref[0, :, h * HEAD_PAD + MLA_NOPE:(h + 1) * HEAD_PAD] = kpe
    vt_ref[0, h * MLA_V:(h + 1) * MLA_V, :] = _dg(wvt_ref[h], lat, NT).astype(BF16)


def _kv_up_kernel(ckv_ref, kpe_ref, wk_ref, wvt_ref, k_ref, vt_ref):
    lat = ckv_ref[0].astype(BF16)
    kpe = kpe_ref[0].astype(BF16)
    for h in range(MLA_HEADS):
        _kv_head(h, lat, kpe, wk_ref, wvt_ref, k_ref, vt_ref)


def _kv_up(ckv_all, kpe_pad, w_k, w_vt):
    b, lk, _ = ckv_all.shape
    tm = ATTN_TK if lk % ATTN_TK == 0 else lk
    tok = lambda w: pl.BlockSpec((1, tm, w), lambda i, t: (i, t, 0))
    return pl.pallas_call(
        _kv_up_kernel,
        grid=(b, lk // tm),
        in_specs=[tok(MLA_KV_LORA), tok(LANES), _const_spec(w_k.shape), _const_spec(w_vt.shape)],
        out_specs=[tok(MLA_HEADS * HEAD_PAD),
                   pl.BlockSpec((1, MLA_HEADS * MLA_V, tm), lambda i, t: (i, 0, t))],
        out_shape=[jax.ShapeDtypeStruct((b, lk, MLA_HEADS * HEAD_PAD), BF16),
                   jax.ShapeDtypeStruct((b, MLA_HEADS * MLA_V, lk), BF16)],
        compiler_params=_params(("arbitrary", "arbitrary")),
        name="kv_up",
    )(ckv_all, kpe_pad, w_k, w_vt)


ATTN_TQ = 512
ATTN_TK = 512
ATTN_HEADS_PER_STEP = 4
CHUNK_SHIFT = CHUNK.bit_length() - 1


def _chunk_end(pos):
    return (lax.shift_right_logical(pos, CHUNK_SHIFT) + 1) * CHUNK


def _attn_kernel(q_ref, k_ref, vt_ref, o_ref, acc_ref, m_ref, l_ref, sa_ref, sb_ref, *, tq, tk, past, lk, nh):
    i = pl.program_id(2)
    acc_ref[...] = jnp.zeros_like(acc_ref)
    m_ref[...] = jnp.full_like(m_ref, NEG_BIG)
    l_ref[...] = jnp.zeros_like(l_ref)
    q0 = past + i * tq
    n_full = lax.div(jnp.minimum(_chunk_end(q0), lk), tk)
    n_end = lax.div(jnp.minimum(_chunk_end(q0 + tq - 1), lk) + tk - 1, tk)
    n_blk = lk // tk
    heads = range(nh)
    ones = jnp.ones((BF16_ROWS, tk), BF16)

    def key_start(j):
        return pl.multiple_of(jnp.minimum(j, n_blk - 1) * tk, tk) if lk > tk else 0

    def scores(j, dst):
        k0 = key_start(j)
        for h in heads:
            dst[h] = _dg(k_ref[0, pl.ds(k0, tk), h * HEAD_PAD:(h + 1) * HEAD_PAD],
                         q_ref[0, :, h * HEAD_PAD:(h + 1) * HEAD_PAD], NT)

    def step(j, src, dst, masked):
        if dst is not None:
            scores(j + 1, dst)
        k0 = key_start(j)
        st = [src[h] for h in heads]
        if masked:
            kc = lax.shift_right_logical(k0 + lax.broadcasted_iota(jnp.int32, (tk, tq), 0), CHUNK_SHIFT)
            qc = lax.shift_right_logical(q0 + lax.broadcasted_iota(jnp.int32, (tk, tq), 1), CHUNK_SHIFT)
            visible = kc <= qc
            st = [jnp.where(visible, s, NEG_BIG) for s in st]
        m_old = [m_ref[h] for h in heads]
        m_new = [jnp.maximum(m_old[h], jnp.max(st[h], axis=0, keepdims=True)) for h in heads]
        alpha = [jnp.exp2(m_old[h] - m_new[h]) for h in heads]
        pt = [jnp.exp2(st[h] - m_new[h]).astype(BF16) for h in heads]
        pv = [_dg(jnp.concatenate([vt_ref[0, h * MLA_V:(h + 1) * MLA_V, pl.ds(k0, tk)], ones], axis=0), pt[h])
              for h in heads]
        for h in heads:
            acc_ref[h] = alpha[h] * acc_ref[h] + pv[h][:MLA_V]
            m_ref[h] = m_new[h]
            l_ref[h] = alpha[h] * l_ref[h] + pv[h][MLA_V:MLA_V + 1]

    if n_blk == 1:
        scores(0, sa_ref)

        @pl.when(n_full == 1)
        def _():
            step(0, sa_ref, None, False)

        @pl.when(n_full == 0)
        def _():
            step(0, sa_ref, None, True)
    else:
        odd = lax.rem(n_full, 2)

        @pl.when(odd == 1)
        def _():
            scores(0, sb_ref)
            step(0, sb_ref, sa_ref, False)

        @pl.when(odd == 0)
        def _():
            scores(0, sa_ref)

        def pair(p, c):
            j = odd + 2 * p
            step(j, sa_ref, sb_ref, False)
            step(j + 1, sb_ref, sa_ref, False)
            return c

        lax.fori_loop(0, lax.div(n_full, 2), pair, 0)

        @pl.when(n_end > n_full)
        def _():
            step(n_full, sa_ref, None, True)

        def rest(j, c):
            scores(j, sa_ref)
            step(j, sa_ref, None, True)
            return c

        lax.fori_loop(n_full + 1, n_end, rest, 0)
    for h in heads:
        o_ref[0, :, h * MLA_V:(h + 1) * MLA_V] = (acc_ref[h] / l_ref[h]).T.astype(BF16)


def _attention(q, k, vt, past):
    b, lq, _ = q.shape
    lk = k.shape[1]
    lq_pad = -(-lq // LANES) * LANES
    if lq_pad != lq:
        q = jnp.pad(q, ((0, 0), (0, lq_pad - lq), (0, 0)))
    tq = _row_tile(lq_pad, ATTN_TQ)
    tk = ATTN_TK if lk % ATTN_TK == 0 else lk
    nh = ATTN_HEADS_PER_STEP
    kv_bytes = nh * lk * (HEAD_PAD + MLA_V) * jnp.dtype(BF16).itemsize
    kv_mode = pl.Buffered(2 if 2 * kv_bytes <= VMEM_LIMIT // 4 else 1)
    out = pl.pallas_call(
        functools.partial(_attn_kernel, tq=tq, tk=tk, past=past, lk=lk, nh=nh),
        grid=(b, MLA_HEADS // nh, lq_pad // tq),
        in_specs=[pl.BlockSpec((1, tq, nh * HEAD_PAD), lambda bi, h, i: (bi, i, h)),
                  pl.BlockSpec((1, lk, nh * HEAD_PAD), lambda bi, h, i: (bi, 0, h), pipeline_mode=kv_mode),
                  pl.BlockSpec((1, nh * MLA_V, lk), lambda bi, h, i: (bi, h, 0), pipeline_mode=kv_mode)],
        out_specs=pl.BlockSpec((1, tq, nh * MLA_V), lambda bi, h, i: (bi, i, h)),
        out_shape=jax.ShapeDtypeStruct((b, lq_pad, MLA_HEADS * MLA_V), BF16),
        scratch_shapes=[pltpu.VMEM((nh, MLA_V, tq), F32), pltpu.VMEM((nh, 1, tq), F32),
                        pltpu.VMEM((nh, 1, tq), F32), pltpu.VMEM((nh, tk, tq), F32),
                        pltpu.VMEM((nh, tk, tq), F32)],
        compiler_params=_params(("arbitrary", "arbitrary", "arbitrary")),
        name="chunk_causal_attention",
    )(q, k, vt)
    return out[:, :lq]


def _pad_hist(hist):
    return jnp.pad(hist, ((0, 0), (SUBLANES - hist.shape[1], 0), (0, 0)))


def _rot_half_cols(w):
    w1, w2 = jnp.split(w, 2, axis=-1)
    return jnp.concatenate([-w2, w1], axis=-1)


def _pack_weights(p):
    d = D_MODEL
    pk = {}
    w_in = p['gdn_w_in'][0]
    pk['gdn_w_main'] = w_in.astype(BF16)
    pk['gdn_w_ba'] = jnp.pad(w_in[:, GDN_CONV_CH + GDN_V:], ((0, 0), (0, LANES - 2 * GDN_HEADS))).astype(BF16)
    pad_gate = lambda a: jnp.pad(a.reshape(1, GDN_HEADS), ((0, 0), (GDN_HEADS, LANES - 2 * GDN_HEADS)))
    pk['gdn_alog'] = pad_gate(p['gdn_a_log'][0])
    pk['gdn_dtb'] = pad_gate(p['gdn_dt_bias'][0])
    pk['gdn_conv_w'] = p['gdn_conv_w'][0]
    pk['gdn_norm_g'] = p['gdn_norm_g'][0].reshape(1, GDN_DV)
    pk['gdn_w_out'] = p['gdn_w_out'][0].astype(BF16)
    wkv = p['kv_w_down']
    pk['kv_w_down'] = jnp.concatenate([wkv, _rot_half_cols(wkv[:, MLA_KV_LORA:])], axis=-1).astype(BF16)
    pk['kv_norm_g'] = p['kv_norm_g'].reshape(1, MLA_KV_LORA)
    w_up = p['kv_w_up'].reshape(MLA_KV_LORA, MLA_HEADS, MLA_NOPE + MLA_V)
    pk['kv_w_k'] = w_up[..., :MLA_NOPE].reshape(MLA_KV_LORA, MLA_HEADS * MLA_NOPE).astype(BF16)
    pk['kv_w_vt'] = jnp.transpose(w_up[..., MLA_NOPE:], (1, 2, 0)).astype(BF16)
    pk['mla_w_dq'] = p['mla_w_dq'][0].astype(BF16)
    pk['mla_q_norm_g'] = p['mla_q_norm_g'][0].reshape(1, MLA_Q_LORA)
    wuq = p['mla_w_uq'][0].reshape(MLA_Q_LORA, MLA_HEADS, MLA_NOPE + MLA_ROPE)
    wuq = jnp.concatenate([wuq, _rot_half_cols(wuq[..., MLA_NOPE:])], axis=-1)
    pk['mla_w_uq'] = wuq.reshape(MLA_Q_LORA, MLA_HEADS * HEAD_PAD).astype(BF16)
    pk['mla_w_out'] = p['mla_w_out'][0].astype(BF16)
    pk['ffn_w_in'] = p['ffn_w_in'].astype(BF16)
    pk['ffn_b_in'] = p['ffn_b_in'].reshape(DEPTH, 1, 2 * D_FF)
    pk['ffn_conv_w'] = p['ffn_conv_w']
    pk['ffn_conv_b'] = p['ffn_conv_b'].reshape(DEPTH, 1, 2 * D_FF)
    pk['ffn_w_down'] = p['ffn_w_down'].astype(BF16)
    pk['ln_g'] = p['ln_g'].reshape(2 * DEPTH, 1, d)
    pk['ln_b'] = p['ln_b'].reshape(2 * DEPTH, 1, d)
    return pk


def _rope_table(past, l):
    inv = 1.0 / (ROPE_THETA ** (jnp.arange(0, MLA_ROPE, 2, dtype=F32) / MLA_ROPE))
    ang = (past + jnp.arange(l, dtype=jnp.int32)).astype(F32)[:, None] * inv[None, :]
    ang = jnp.concatenate([ang, ang], axis=-1)
    return jnp.concatenate([jnp.cos(ang), jnp.sin(ang)], axis=-1)


def _block_tail(o, w_out, x, mods, hist, pk, layer):
    y, hist8 = _mix_ffn(o, x, mods[2 * layer], w_out, pk['ln_g'][2 * layer], pk['ln_b'][2 * layer],
                        mods[2 * layer + 1], pk['ffn_w_in'], pk['ffn_b_in'][layer],
                        pk['ffn_conv_w'][layer], pk['ffn_conv_b'][layer], _pad_hist(hist),
                        pk['ffn_w_down'], pk['ln_g'][2 * layer + 1], pk['ln_b'][2 * layer + 1], layer)
    return y, hist8[:, SUBLANES - (FFN_CONV - 1):]


def _trunk(x, mods, gdn_state, gdn_conv, ffn_conv, ckv_past, kpe_past, pk):
    b, l, _ = x.shape
    past = kpe_past.shape[1]
    c_len = CHUNK if l % CHUNK == 0 else l
    n = l // c_len

    q, k, v, z, gb, hist8 = _gdn_in(x, mods[0], pk['gdn_w_main'], pk['gdn_w_ba'], _pad_hist(gdn_conv[:, 0]),
                                    pk['gdn_conv_w'], pk['gdn_alog'], pk['gdn_dtb'], c_len)
    col = gb.reshape(b, n, c_len, 2 * GDN_HEADS)
    row = jnp.swapaxes(col, 2, 3)
    o, s_end = _gdn_chunk(q, k, v, z, pk['gdn_norm_g'], col, row, gdn_state[:, 0], c_len)
    gdn_conv_out = hist8[:, None, SUBLANES - (GDN_CONV - 1):]
    x, fh0 = _block_tail(o, pk['gdn_w_out'], x, mods, ffn_conv[:, 0], pk, 0)

    proj = _mla_proj(x, mods[2], pk['kv_w_down'], pk['kv_norm_g'], _rope_table(past, l),
                     pk['mla_w_dq'], pk['mla_q_norm_g'], pk['mla_w_uq'], pk['kv_w_k'], pk['kv_w_vt'],
                     with_kv=(past == 0))
    ckv_new, kpe_new, qm = proj[:3]
    if past == 0:
        km, vm = proj[3:]
    else:
        ckv_all = jnp.concatenate([ckv_past, ckv_new], axis=1)
        kpe_all = jnp.concatenate([kpe_past, kpe_new], axis=1)
        kpe_pad = jnp.pad(kpe_all, ((0, 0), (0, 0), (0, LANES - MLA_ROPE)))
        km, vm = _kv_up(ckv_all, kpe_pad, pk['kv_w_k'], pk['kv_w_vt'])
    o = _attention(qm, km, vm, past)
    x, fh1 = _block_tail(o, pk['mla_w_out'], x, mods, ffn_conv[:, 1], pk, 1)

    return (x, s_end[:, None], gdn_conv_out, jnp.stack([fh0, fh1], axis=1), ckv_new, kpe_new)


def kernel(x_prompt, x_sample, c_prompt, c_sample, state_gdn, state_gdn_conv, state_ffn_conv, cache_ckv, cache_kpe, ada_w, ada_b, ln_g, ln_b, gdn_w_in, gdn_conv_w, gdn_a_log, gdn_dt_bias, gdn_norm_g, gdn_w_out, kv_w_down, kv_norm_g, kv_w_up, mla_w_dq, mla_q_norm_g, mla_w_uq, mla_w_out, ffn_w_in, ffn_b_in, ffn_conv_w, ffn_conv_b, ffn_w_down):
    p = {'ln_g': ln_g, 'ln_b': ln_b, 'gdn_w_in': gdn_w_in, 'gdn_conv_w': gdn_conv_w, 'gdn_a_log': gdn_a_log,
         'gdn_dt_bias': gdn_dt_bias, 'gdn_norm_g': gdn_norm_g, 'gdn_w_out': gdn_w_out,
         'kv_w_down': kv_w_down, 'kv_norm_g': kv_norm_g, 'kv_w_up': kv_w_up,
         'mla_w_dq': mla_w_dq, 'mla_q_norm_g': mla_q_norm_g, 'mla_w_uq': mla_w_uq, 'mla_w_out': mla_w_out,
         'ffn_w_in': ffn_w_in, 'ffn_b_in': ffn_b_in, 'ffn_conv_w': ffn_conv_w,
         'ffn_conv_b': ffn_conv_b, 'ffn_w_down': ffn_w_down}
    pk = _pack_weights(p)
    bp = x_prompt.shape[0]
    mods = _ada_terms(jnp.concatenate([c_prompt, c_sample], axis=0), ada_w, ada_b)[:, :, None, :]
    zeros_like_b = lambda a: jnp.zeros((bp,) + a.shape[1:], a.dtype)
    out_p = _trunk(x_prompt, mods[:, :bp], zeros_like_b(state_gdn), zeros_like_b(state_gdn_conv),
                   zeros_like_b(state_ffn_conv), jnp.zeros((bp, 0, MLA_KV_LORA), cache_ckv.dtype),
                   jnp.zeros((bp, 0, MLA_ROPE), cache_kpe.dtype), pk)
    out_s = _trunk(x_sample, mods[:, bp:], state_gdn, state_gdn_conv, state_ffn_conv, cache_ckv, cache_kpe, pk)
    return (out_p[0], out_s[0]) + out_p[1:] + out_s[1:]
```
